```python
import jax, jax.numpy as jnp
from jax import lax
import numpy as np

D_MODEL = 1024
BATCH = 4
SEQ = 8192
DEPTH = 1

RWKV_HEAD_DIM = 64
RWKV_WIDTH = 1024
RWKV_HEADS = RWKV_WIDTH // RWKV_HEAD_DIM
DECAY_LORA = 64
ICLR_LORA = 64
GATE_LORA = 160
GN_EPS = 64e-5
GMLP_WIDTH = 1024
GMLP_GROUPS = 8
GMLP_GROUP_DIM = GMLP_WIDTH // GMLP_GROUPS
CHUNK = 128
N_EXPERTS = 256
TOP_K = 8
N_EXPERT_GROUPS = 8
TOPK_GROUPS = 4
EXPERT_FF = 256
SHARED_FF = 256
ROUTED_SCALE = 2.5
ROW_BLOCK = 128
LN_EPS = 1e-5
ALPHA = (2 * DEPTH) ** 0.25
BETA = (8 * DEPTH) ** -0.25
C_RWKV = 3 * RWKV_WIDTH + DECAY_LORA + ICLR_LORA + GATE_LORA
C_GMLP = 2 * GMLP_WIDTH
C_GATES = 2 * D_MODEL
C_IN = C_RWKV + C_GMLP + C_GATES

kernel_name = 'rwkv7_gmlp_gated_hybrid_moe_deepnorm'


def _layer_norm(x, w, b):
    xf = x.astype(jnp.float32)
    mu = jnp.mean(xf, axis=-1, keepdims=True)
    var = jnp.mean(jnp.square(xf - mu), axis=-1, keepdims=True)
    return ((xf - mu) * lax.rsqrt(var + LN_EPS) * w + b).astype(x.dtype)


def _token_shift(p):
    return jnp.pad(p[:, :-1], ((0, 0), (1, 0), (0, 0)))


def _rwkv7_scan(r, w, k, v, a, b):
    def step(state, inp):
        r_t, w_t, k_t, v_t, a_t, b_t = inp
        sa = jnp.einsum('bhij,bhj->bhi', state, a_t)
        state = (state * w_t[:, :, None, :] + sa[..., None] * b_t[:, :, None, :]
                 + v_t[..., None] * k_t[:, :, None, :])
        return state, jnp.einsum('bhij,bhj->bhi', state, r_t)
    bsz, _, nh, nd = r.shape
    xs = tuple(jnp.swapaxes(t, 0, 1) for t in (r, w, k, v, a, b))
    _, ys = lax.scan(step, jnp.zeros((bsz, nh, nd, nd), jnp.float32), xs)
    return jnp.swapaxes(ys, 0, 1)


def _rwkv7_time_mix(p, mu, w0, decay_lora, a0, iclr_lora, gate_lora, k_k, k_a, r_k, gn_w, gn_b):
    bsz, seq, _ = p.shape
    pf = p.astype(jnp.float32)
    pm = pf + mu * (_token_shift(pf) - pf)
    o = np.cumsum([0, RWKV_WIDTH, RWKV_WIDTH, RWKV_WIDTH, DECAY_LORA, ICLR_LORA, GATE_LORA])
    r, k, v, xw, xa, xg = (pm[..., o[i]:o[i + 1]] for i in range(6))
    w = -jax.nn.softplus(-(w0 + jnp.tanh(xw) @ decay_lora)) - 0.5
    decay = jnp.exp(-jnp.exp(w))
    a = jax.nn.sigmoid(a0 + xa @ iclr_lora)
    g = jax.nn.sigmoid(xg) @ gate_lora
    heads = lambda t: t.reshape(bsz, seq, RWKV_HEADS, RWKV_HEAD_DIM)
    kk = heads(k * k_k)
    kk = kk / jnp.maximum(jnp.linalg.norm(kk, axis=-1, keepdims=True), 1e-12)
    k = k * (1.0 + (a - 1.0) * k_a)
    r_h, k_h, v_h, a_h = heads(r), heads(k), heads(v), heads(a)
    y = _rwkv7_scan(r_h, heads(decay), k_h, v_h, -kk, kk * a_h)
    mean = jnp.mean(y, axis=-1, keepdims=True)
    var = jnp.mean(jnp.square(y - mean), axis=-1, keepdims=True)
    y = ((y - mean) * lax.rsqrt(var + GN_EPS)).reshape(bsz, seq, RWKV_WIDTH) * gn_w + gn_b
    bonus = (jnp.sum(r_h * k_h * r_k, axis=-1, keepdims=True) * v_h).reshape(bsz, seq, RWKV_WIDTH)
    return ((y + bonus) * g).astype(p.dtype)


def _gmlp_chunk_mix(p, ln_w, ln_b, spatial_w, spatial_b):
    bsz, seq, _ = p.shape
    z = jax.nn.gelu(p, approximate=False)
    u, v = jnp.split(z, 2, axis=-1)
    v = _layer_norm(v, ln_w, ln_b)
    v = v.reshape(bsz, seq // CHUNK, CHUNK, GMLP_GROUPS, GMLP_GROUP_DIM)
    causal = jnp.tril(jnp.ones((CHUNK, CHUNK), dtype=bool))
    w_s = jnp.where(causal, spatial_w, 0.0).astype(p.dtype)
    sv = jnp.einsum('gts,bcsgd->bctgd', w_s, v) + spatial_b.T[:, :, None].astype(p.dtype)
    return u * sv.reshape(bsz, seq, GMLP_WIDTH)


def _moe_ffn(h, w_router, router_bias, w_exp_gate_up, w_exp_down, w_sh_gate_up, w_sh_down):
    n_tok, d = h.shape
    n_assign = n_tok * TOP_K
    scores = jax.nn.sigmoid(h.astype(jnp.float32) @ w_router.astype(jnp.float32))
    choice = scores + router_bias.astype(jnp.float32)
    grp_top2 = lax.top_k(choice.reshape(n_tok, N_EXPERT_GROUPS, N_EXPERTS // N_EXPERT_GROUPS), 2)[0]
    _, top_grp = lax.top_k(jnp.sum(grp_top2, axis=-1), TOPK_GROUPS)
    grp_mask = jnp.sum(jax.nn.one_hot(top_grp, N_EXPERT_GROUPS), axis=1) > 0
    exp_mask = jnp.repeat(grp_mask, N_EXPERTS // N_EXPERT_GROUPS, axis=1)
    _, top_e = lax.top_k(jnp.where(exp_mask, choice, -jnp.inf), TOP_K)
    gate = jnp.take_along_axis(scores, top_e, axis=1)
    gate = gate / jnp.sum(gate, axis=-1, keepdims=True) * ROUTED_SCALE
    eid = top_e.reshape(-1)
    order = jnp.argsort(eid)
    e_sorted = eid[order]
    counts = jnp.bincount(eid, length=N_EXPERTS)
    padded = (counts + ROW_BLOCK - 1) // ROW_BLOCK * ROW_BLOCK
    pad_end = jnp.cumsum(padded)
    start = jnp.cumsum(counts) - counts
    dest = (pad_end - padded)[e_sorted] + jnp.arange(n_assign) - start[e_sorted]
    n_blocks = (n_assign + N_EXPERTS * (ROW_BLOCK - 1) + ROW_BLOCK - 1) // ROW_BLOCK
    n_rows = n_blocks * ROW_BLOCK
    row_tok = jnp.zeros((n_rows,), jnp.int32).at[dest].set((order // TOP_K).astype(jnp.int32))
    row_gate = jnp.zeros((n_rows,), jnp.float32).at[dest].set(gate.reshape(-1)[order])
    block_expert = jnp.minimum(
        jnp.searchsorted(pad_end, jnp.arange(n_blocks) * ROW_BLOCK, side='right'), N_EXPERTS - 1)

    def expert_block(args):
        tok, g_row, e = args
        gu = h[tok] @ w_exp_gate_up[e]
        g_, u_ = jnp.split(gu, 2, axis=-1)
        return ((jax.nn.silu(g_) * u_) @ w_exp_down[e]) * g_row[:, None].astype(h.dtype)

    rows = lax.map(expert_block, (row_tok.reshape(n_blocks, ROW_BLOCK),
                                  row_gate.reshape(n_blocks, ROW_BLOCK), block_expert))
    routed = jax.ops.segment_sum(rows.reshape(n_rows, d), row_tok, num_segments=n_tok)
    sg, su = jnp.split(h @ w_sh_gate_up, 2, axis=-1)
    return routed + (jax.nn.silu(sg) * su) @ w_sh_down


def setup_inputs(seed: int = 0) -> dict:
    key = jax.random.key(seed)
    ks = jax.random.split(key, 32)
    nrm = lambda k, shape, scale: jax.random.normal(k, shape, jnp.float32) * scale
    L = DEPTH
    return {
        'x': nrm(ks[0], (BATCH, SEQ, D_MODEL), 1.0),
        'w_in': nrm(ks[1], (L, D_MODEL, C_IN), D_MODEL ** -0.5),
        'shift_mu': jax.random.uniform(ks[2], (L, C_RWKV), jnp.float32),
        'rwkv_w0': jax.random.uniform(ks[3], (L, RWKV_WIDTH), jnp.float32, -6.0, 1.0),
        'rwkv_decay_lora': nrm(ks[4], (L, DECAY_LORA, RWKV_WIDTH), 0.1 * DECAY_LORA ** -0.5),
        'rwkv_a0': nrm(ks[5], (L, RWKV_WIDTH), 0.5),
        'rwkv_iclr_lora': nrm(ks[6], (L, ICLR_LORA, RWKV_WIDTH), ICLR_LORA ** -0.5),
        'rwkv_gate_lora': nrm(ks[7], (L, GATE_LORA, RWKV_WIDTH), GATE_LORA ** -0.5),
        'rwkv_k_k': 0.85 + nrm(ks[8], (L, RWKV_WIDTH), 0.05),
        'rwkv_k_a': 1.0 + nrm(ks[9], (L, RWKV_WIDTH), 0.05),
        'rwkv_r_k': nrm(ks[10], (L, RWKV_HEADS, RWKV_HEAD_DIM), 0.1),
        'rwkv_gn_w': 1.0 + nrm(ks[11], (L, RWKV_WIDTH), 0.05),
        'rwkv_gn_b': nrm(ks[12], (L, RWKV_WIDTH), 0.02),
        'gmlp_ln_w': 1.0 + nrm(ks[13], (L, GMLP_WIDTH), 0.05),
        'gmlp_ln_b': nrm(ks[14], (L, GMLP_WIDTH), 0.02),
        'gmlp_spatial_w': nrm(ks[15], (L, GMLP_GROUPS, CHUNK, CHUNK), 0.5 * CHUNK ** -0.5),
        'gmlp_spatial_b': 1.0 + nrm(ks[16], (L, GMLP_GROUPS, CHUNK), 0.1),
        'w_o_rwkv': nrm(ks[17], (L, RWKV_WIDTH, D_MODEL), RWKV_WIDTH ** -0.5),
        'w_o_gmlp': nrm(ks[18], (L, GMLP_WIDTH, D_MODEL), GMLP_WIDTH ** -0.5),
        'w_out': nrm(ks[19], (L, D_MODEL, D_MODEL), BETA * D_MODEL ** -0.5),
        'ln1_w': 1.0 + nrm(ks[20], (L, D_MODEL), 0.05),
        'ln1_b': nrm(ks[21], (L, D_MODEL), 0.02),
        'w_router': nrm(ks[22], (L, D_MODEL, N_EXPERTS), D_MODEL ** -0.5),
        'router_bias': nrm(ks[23], (L, N_EXPERTS), 0.01),
        'w_exp_gate_up': nrm(ks[24], (L, N_EXPERTS, D_MODEL, 2 * EXPERT_FF), D_MODEL ** -0.5),
        'w_exp_down': nrm(ks[25], (L, N_EXPERTS, EXPERT_FF, D_MODEL), BETA * EXPERT_FF ** -0.5),
        'w_sh_gate_up': nrm(ks[26], (L, D_MODEL, 2 * SHARED_FF), D_MODEL ** -0.5),
        'w_sh_down': nrm(ks[27], (L, SHARED_FF, D_MODEL), BETA * SHARED_FF ** -0.5),
        'ln2_w': 1.0 + nrm(ks[28], (L, D_MODEL), 0.05),
        'ln2_b': nrm(ks[29], (L, D_MODEL), 0.02),
    }


def reference(x, w_in, shift_mu, rwkv_w0, rwkv_decay_lora, rwkv_a0, rwkv_iclr_lora, rwkv_gate_lora,
              rwkv_k_k, rwkv_k_a, rwkv_r_k, rwkv_gn_w, rwkv_gn_b, gmlp_ln_w, gmlp_ln_b,
              gmlp_spatial_w, gmlp_spatial_b, w_o_rwkv, w_o_gmlp, w_out, ln1_w, ln1_b,
              w_router, router_bias, w_exp_gate_up, w_exp_down, w_sh_gate_up, w_sh_down,
              ln2_w, ln2_b):
    bsz, seq, d = x.shape
    h = x
    for l in range(DEPTH):
        p = h @ w_in[l]
        y_a = _rwkv7_time_mix(p[..., :C_RWKV], shift_mu[l], rwkv_w0[l], rwkv_decay_lora[l], rwkv_a0[l],
                              rwkv_iclr_lora[l], rwkv_gate_lora[l], rwkv_k_k[l], rwkv_k_a[l],
                              rwkv_r_k[l], rwkv_gn_w[l], rwkv_gn_b[l]) @ w_o_rwkv[l]
        y_b = _gmlp_chunk_mix(p[..., C_RWKV:C_RWKV + C_GMLP], gmlp_ln_w[l], gmlp_ln_b[l],
                              gmlp_spatial_w[l], gmlp_spatial_b[l]) @ w_o_gmlp[l]
        gate_a, gate_b = jnp.split(jax.nn.sigmoid(p[..., C_RWKV + C_GMLP:]), 2, axis=-1)
        mix = (gate_a * y_a + gate_b * y_b) @ w_out[l]
        h = _layer_norm(ALPHA * h + mix, ln1_w[l], ln1_b[l])
        f = _moe_ffn(h.reshape(bsz * seq, d), w_router[l], router_bias[l], w_exp_gate_up[l],
                     w_exp_down[l], w_sh_gate_up[l], w_sh_down[l]).reshape(bsz, seq, d)
        h = _layer_norm(ALPHA * h + f, ln2_w[l], ln2_b[l])
    return h
```

```python
import functools

import numpy as np
import jax
import jax.numpy as jnp
from jax import lax
from jax.experimental import pallas as pl
from jax.experimental.pallas import tpu as pltpu

F32 = jnp.float32
BF = jnp.bfloat16

D_MODEL = 1024
HEAD_DIM = 64
N_HEADS = 16
DECAY_LORA = 64
ICLR_LORA = 64
GATE_LORA = 160
GN_EPS = 64e-5
GMLP_GROUPS = 8
GMLP_CHUNK = 128
N_EXPERTS = 256
TOP_K = 8
N_EXPERT_GROUPS = 8
TOPK_GROUPS = 4
EXPERT_FF = 256
SHARED_FF = 256
ROUTED_SCALE = 2.5
LN_EPS = 1e-5
DEPTH = 1
ALPHA = (2 * DEPTH) ** 0.25

LANES = 128
V7X_VMEM_LIMIT_BYTES = 56 * 1024 * 1024

C_RKV = 3 * D_MODEL
C_LORA = DECAY_LORA + ICLR_LORA + GATE_LORA
C_LORA_PAD = 384
C_RWKV_PAD = C_RKV + C_LORA_PAD
C_GMLP = 2 * D_MODEL
C_GATES = 2 * D_MODEL
C_IN_PAD = C_RWKV_PAD + C_GMLP + C_GATES

SCAN_CHUNK = 64
QUAD = 4 * HEAD_DIM
N_QUADS = D_MODEL // QUAD
TM = 128
ROUTE_TM = 256
FFN_ROWS = 256
FINAL_TM = 128


def _dot(a, b):
    return jnp.dot(a, b, preferred_element_type=F32)


def _dot_nt(a, b):
    return lax.dot_general(a, b, (((1,), (1,)), ((), ())), preferred_element_type=F32)


def _dot_tn(a, b):
    return lax.dot_general(a, b, (((0,), (0,)), ((), ())), preferred_element_type=F32)


def _split2(z):
    hi = z.astype(BF)
    lo = (z - hi.astype(F32)).astype(BF)
    return hi, lo


def _split3(z):
    hi = z.astype(BF)
    r1 = z - hi.astype(F32)
    mid = r1.astype(BF)
    lo = (r1 - mid.astype(F32)).astype(BF)
    return hi, mid, lo


def _headsum(z, g256):
    outs = []
    for j in range(N_QUADS):
        hi, lo = _split2(z[:, QUAD * j:QUAD * (j + 1)])
        outs.append(_dot(hi, g256) + _dot(lo, g256))
    return jnp.concatenate(outs, axis=1)


def _layer_norm(v, w, b):
    mu = jnp.mean(v, axis=-1, keepdims=True)
    d = v - mu
    var = jnp.mean(d * d, axis=-1, keepdims=True)
    return d * lax.rsqrt(var + LN_EPS) * w + b


def _const_spec(shape):
    nd = len(shape)
    return pl.BlockSpec(shape, lambda *_: (0,) * nd, pipeline_mode=pl.Buffered(1))


def _front_kernel(x_ref, win_ref, mu_ref, w0_ref, a0_ref, w1_ref, w2_ref, kk_ref, ka_ref, rk_ref,
                  g256_ref, l128_ref, e128_ref, lnw_ref, lnb_ref, ws_ref, sb_ref, wog_ref,
                  rt_ref, at_ref, bt_ref, kt_ref, vv_ref, bh_ref, kh_ref, wc_ref,
                  g_ref, bg_ref, ga_ref, zb_ref, prev_ref, *, tiles_per_seq):
    i = pl.program_id(0)
    first = (i % tiles_per_seq) == 0
    xb = x_ref[...].astype(BF)

    p = _dot(xb, win_ref[:, 0:C_RWKV_PAD])
    row = lax.broadcasted_iota(jnp.int32, (TM, 1), 0)
    prev = jnp.where(first, 0.0, prev_ref[...])
    sh = jnp.where(row == 0, prev, pltpu.roll(p, 1, 0))
    prev_ref[...] = p[TM - 1:TM, :]
    pm = p + mu_ref[...] * (sh - p)
    r = pm[:, 0:D_MODEL]
    k = pm[:, D_MODEL:2 * D_MODEL]
    v = pm[:, 2 * D_MODEL:3 * D_MODEL]

    l1 = pm[:, C_RKV:C_RKV + LANES]
    lane = lax.broadcasted_iota(jnp.int32, (1, LANES), 1)
    act1 = jnp.where(lane < DECAY_LORA, jnp.tanh(l1), l1).astype(BF)
    o1 = _dot(act1, w1_ref[...])
    w = -jax.nn.softplus(-(w0_ref[...] + o1[:, 0:D_MODEL])) - 0.5
    logd = -jnp.exp(w)
    a = jax.nn.sigmoid(a0_ref[...] + o1[:, D_MODEL:2 * D_MODEL])
    act2 = jax.nn.sigmoid(pm[:, C_RKV + LANES:C_RWKV_PAD]).astype(BF)
    g = _dot(act2, w2_ref[...])

    g256 = g256_ref[...]
    kkr = k * kk_ref[...]
    nrm = jnp.sqrt(_headsum(kkr * kkr, g256))
    kk = kkr / jnp.maximum(nrm, 1e-12)
    kmod = k * (1.0 + (a - 1.0) * ka_ref[...])
    bonus = _headsum(r * kmod * rk_ref[...], g256) * v
    kka = kk * a

    h3 = _split3(logd)
    cum = sum(_dot(l128_ref[...], part) for part in h3)
    cend = sum(_dot(e128_ref[...], part) for part in h3)
    e_in = jnp.exp(cum)
    e_ex = jnp.exp(cum - logd)
    e_inv = jnp.exp(-cum)
    e_end = jnp.exp(cend - cum)

    outs = (
        (rt_ref, r * e_in),
        (at_ref, -kk * e_ex),
        (bt_ref, kka * e_inv),
        (kt_ref, kmod * e_inv),
        (vv_ref, v),
        (bh_ref, kka * e_end),
        (kh_ref, kmod * e_end),
    )
    for ref, val in outs:
        vb = val.astype(BF)
        for q in range(N_QUADS):
            ref[q] = vb[:, QUAD * q:QUAD * (q + 1)]
    ewc = jnp.exp(cend)
    wc_ref[0] = jnp.concatenate(
        [ewc[SCAN_CHUNK * c:SCAN_CHUNK * c + 1, QUAD * q:QUAD * (q + 1)]
         for c in range(TM // SCAN_CHUNK) for q in range(N_QUADS)], axis=0)
    g_ref[...] = g.astype(BF)
    bg_ref[...] = (bonus * g).astype(BF)

    pg = _dot(xb, win_ref[:, C_RWKV_PAD:C_RWKV_PAD + C_GMLP])
    z = 0.5 * pg * (1.0 + lax.erf(pg * np.float32(np.sqrt(0.5))))
    u = z[:, 0:D_MODEL]
    vn = _layer_norm(z[:, D_MODEL:], lnw_ref[...], lnb_ref[...]).astype(BF)
    trow = lax.broadcasted_iota(jnp.int32, (GMLP_CHUNK, GMLP_CHUNK), 0)
    tcol = lax.broadcasted_iota(jnp.int32, (GMLP_CHUNK, GMLP_CHUNK), 1)
    causal = tcol <= trow
    svs = []
    for gi in range(GMLP_GROUPS):
        wsg = jnp.where(causal, ws_ref[gi], 0.0).astype(BF)
        svs.append(_dot(wsg, vn[:, LANES * gi:LANES * (gi + 1)]))
    sv = jnp.concatenate(svs, axis=1) + sb_ref[...]
    yb = _dot((u * sv).astype(BF), wog_ref[...])

    pgate = _dot(xb, win_ref[:, C_RWKV_PAD + C_GMLP:C_IN_PAD])
    ga_ref[...] = jax.nn.sigmoid(pgate[:, 0:D_MODEL]).astype(BF)
    zb_ref[...] = (jax.nn.sigmoid(pgate[:, D_MODEL:]) * yb).astype(BF)


def _front(x2, win, mu, w0, a0, w1, w2, k_k, k_a, r_k, g256, l128, e128, lnw, lnb, ws, sb, wog, seq):
    n_tok = x2.shape[0]
    n_tiles = n_tok // TM
    row = lambda i: (i, 0)
    quad_spec = pl.BlockSpec((N_QUADS, TM, QUAD), lambda i: (0, i, 0))
    quad_shape = jax.ShapeDtypeStruct((N_QUADS, n_tok, QUAD), BF)
    full_spec = pl.BlockSpec((TM, D_MODEL), row)
    full_shape = jax.ShapeDtypeStruct((n_tok, D_MODEL), BF)
    n_wc = (TM // SCAN_CHUNK) * N_QUADS
    consts = (win, mu, w0, a0, w1, w2, k_k, k_a, r_k, g256, l128, e128, lnw, lnb, ws, sb, wog)
    return pl.pallas_call(
        functools.partial(_front_kernel, tiles_per_seq=seq // TM),
        grid=(n_tiles,),
        in_specs=[pl.BlockSpec((TM, D_MODEL), row)] + [_const_spec(c.shape) for c in consts],
        out_specs=[quad_spec] * 7 + [pl.BlockSpec((1, n_wc, QUAD), lambda i: (i, 0, 0))] + [full_spec] * 4,
        out_shape=[quad_shape] * 7 + [jax.ShapeDtypeStruct((n_tiles, n_wc, QUAD), F32)] + [full_shape] * 4,
        scratch_shapes=[pltpu.VMEM((1, C_RWKV_PAD), F32)],
        compiler_params=pltpu.CompilerParams(
            dimension_semantics=("arbitrary",), vmem_limit_bytes=V7X_VMEM_LIMIT_BYTES),
        name="front",
    )(x2, *consts)


def _scan_kernel(rt_ref, at_ref, bt_ref, kt_ref, vv_ref, bh_ref, kh_ref, wc_ref,
                 g_ref, bg_ref, ga_ref, zb_ref, x_ref, gnw_ref, gnb_ref, g256_ref,
                 wor_ref, wout_ref, ln1w_ref, ln1b_ref,
                 h_ref, hp_ref, s_ref, y_ref, *, tiles_per_seq):
    i = pl.program_id(0)

    @pl.when((i % tiles_per_seq) == 0)
    def _():
        s_ref[...] = jnp.zeros_like(s_ref)

    lane = lax.broadcasted_iota(jnp.int32, (1, QUAD), 1)
    lane_head = lane // HEAD_DIM
    lane_pos = lane % HEAD_DIM
    trow = lax.broadcasted_iota(jnp.int32, (SCAN_CHUNK, 1), 0)
    strict = lane_pos < trow
    incl = lane_pos <= trow
    row_head = lax.broadcasted_iota(jnp.int32, (QUAD, 1), 0) // HEAD_DIM
    diag_blocks = row_head == lane_head

    def bd4(m):
        zero = jnp.zeros_like(m)
        return jnp.concatenate([jnp.where(lane_head == j, m, zero) for j in range(4)], axis=0)

    for q in range(N_QUADS):
        state = s_ref[q]
        for c in range(TM // SCAN_CHUNK):
            sl = slice(SCAN_CHUNK * c, SCAN_CHUNK * (c + 1))
            at = at_ref[q, sl, :]
            rt = rt_ref[q, sl, :]
            bt = bt_ref[q, sl, :]
            kt = kt_ref[q, sl, :]
            vv = vv_ref[q, sl, :]
            ra = jnp.concatenate([at, rt], axis=0)
            gb = _dot_nt(ra, bd4(bt))
            gk = _dot_nt(ra, bd4(kt))
            a_ab = jnp.where(strict, gb[:SCAN_CHUNK], 0.0)
            a_ak = jnp.where(strict, gk[:SCAN_CHUNK], 0.0)
            a_rb = jnp.where(incl, gb[SCAN_CHUNK:], 0.0)
            a_rk = jnp.where(incl, gk[SCAN_CHUNK:], 0.0)

            pw = a_ab.astype(BF)
            bdp = bd4(pw)
            tm = a_ab
            for _ in range(5):
                p2 = _dot(pw, bdp)
                pw = p2.astype(BF)
                bdp = bd4(pw)
                tm = tm + p2 + _dot(tm.astype(BF), bdp)
            tb = tm.astype(BF)
            bdv = bd4(vv)
            aw = at.astype(F32) + _dot(tb, bd4(at))
            akv = _dot(a_ak.astype(BF), bdv)
            u0 = akv + _dot(tb, bd4(akv.astype(BF)))

            sb16 = state.astype(BF)
            u = u0 + _dot_nt(aw.astype(BF), sb16)
            ub = u.astype(BF)
            y = _dot_nt(rt, sb16) + _dot(a_rb.astype(BF), bd4(ub)) + _dot(a_rk.astype(BF), bdv)
            y_ref[sl, QUAD * q:QUAD * (q + 1)] = y
            upd = _dot_tn(ub, bh_ref[q, sl, :]) + _dot_tn(vv, kh_ref[q, sl, :])
            wc = wc_ref[0, c * N_QUADS + q:c * N_QUADS + q + 1, :]
            state = state * wc + jnp.where(diag_blocks, upd, 0.0)
        s_ref[q] = state

    g256 = g256_ref[...]
    y = y_ref[...]
    inv_n = 1.0 / HEAD_DIM
    d = y - _headsum(y, g256) * inv_n
    var = _headsum(d * d, g256) * inv_n
    yn = d * lax.rsqrt(var + GN_EPS) * gnw_ref[...] + gnb_ref[...]
    ya = _dot((yn * g_ref[...].astype(F32) + bg_ref[...].astype(F32)).astype(BF), wor_ref[...])
    mix_in = ga_ref[...].astype(F32) * ya + zb_ref[...].astype(F32)
    mix = _dot(mix_in.astype(BF), wout_ref[...])
    h = _layer_norm(ALPHA * x_ref[...] + mix, ln1w_ref[...], ln1b_ref[...])
    h_ref[...] = h
    hb = pltpu.bitcast(h.astype(BF).astype(F32), jnp.uint32)
    half = D_MODEL // 2
    hp_ref[...] = (hb[:, :half] >> 16) | (hb[:, half:] & jnp.uint32(0xFFFF0000))


def _scan(quads, wc, g, bg, ga, zb, x2, gnw, gnb, g256, wor, wout, ln1w, ln1b, seq):
    n_tok = x2.shape[0]
    n_tiles = n_tok // TM
    row = lambda i: (i, 0)
    quad_spec = pl.BlockSpec((N_QUADS, TM, QUAD), lambda i: (0, i, 0))
    full_spec = pl.BlockSpec((TM, D_MODEL), row)
    n_wc = wc.shape[1]
    consts = (gnw, gnb, g256, wor, wout, ln1w, ln1b)
    return pl.pallas_call(
        functools.partial(_scan_kernel, tiles_per_seq=seq // TM),
        grid=(n_tiles,),
        in_specs=[quad_spec] * 7 + [pl.BlockSpec((1, n_wc, QUAD), lambda i: (i, 0, 0))]
        + [full_spec] * 5 + [_const_spec(c.shape) for c in consts],
        out_specs=[full_spec, pl.BlockSpec((TM, D_MODEL // 2), row)],
        out_shape=[jax.ShapeDtypeStruct((n_tok, D_MODEL), F32),
                   jax.ShapeDtypeStruct((n_tok, D_MODEL // 2), jnp.uint32)],
        scratch_shapes=[pltpu.VMEM((N_QUADS, QUAD, QUAD), F32), pltpu.VMEM((TM, D_MODEL), F32)],
        compiler_params=pltpu.CompilerParams(
            dimension_semantics=("arbitrary",), vmem_limit_bytes=V7X_VMEM_LIMIT_BYTES),
        name="scan",
    )(*quads, wc, g, bg, ga, zb, x2, *consts)


def _first_max(x, idx, n):
    m = jnp.max(x, axis=0, keepdims=True)
    first = jnp.min(jnp.where(x == m, idx, n), axis=0, keepdims=True)
    return m, first


def _route_kernel(h_ref, wrh_ref, wrl_ref, bias_ref, e_ref, gate_ref):
    hh, hl = _split2(h_ref[...])
    wrh = wrh_ref[...]
    logits = _dot_nt(wrh, hh) + _dot_nt(wrh, hl) + _dot_nt(wrl_ref[...], hh)
    scores = jax.nn.sigmoid(logits)
    choice = scores + bias_ref[...]
    n_tok = choice.shape[1]
    per = N_EXPERTS // N_EXPERT_GROUPS
    eidx = lax.broadcasted_iota(jnp.int32, (N_EXPERTS, 1), 0)
    gidx = lax.broadcasted_iota(jnp.int32, (N_EXPERT_GROUPS, 1), 0)
    lidx = lax.broadcasted_iota(jnp.int32, (per, 1), 0)
    neg = -jnp.inf

    gsc = []
    for gi in range(N_EXPERT_GROUPS):
        blk = choice[per * gi:per * (gi + 1), :]
        m1, i1 = _first_max(blk, lidx, per)
        m2 = jnp.max(jnp.where(lidx == i1, neg, blk), axis=0, keepdims=True)
        gsc.append(m1 + m2)
    gscore = jnp.concatenate(gsc, axis=0)
    gsel = jnp.zeros(gscore.shape, jnp.bool_)
    for _ in range(TOPK_GROUPS):
        _, gi1 = _first_max(gscore, gidx, N_EXPERT_GROUPS)
        hit = gidx == gi1
        gsel = gsel | hit
        gscore = jnp.where(hit, neg, gscore)
    emask = jnp.concatenate(
        [jnp.broadcast_to(gsel[gi:gi + 1, :], (per, n_tok)) for gi in range(N_EXPERT_GROUPS)], axis=0)
    cand = jnp.where(emask, choice, neg)
    es, gs = [], []
    for _ in range(TOP_K):
        _, e1 = _first_max(cand, eidx, N_EXPERTS)
        hit = eidx == e1
        es.append(e1)
        gs.append(jnp.sum(jnp.where(hit, scores, 0.0), axis=0, keepdims=True))
        cand = jnp.where(hit, neg, cand)
    gates = jnp.concatenate(gs, axis=0)
    gates = gates / jnp.sum(gates, axis=0, keepdims=True) * ROUTED_SCALE
    e_ref[...] = jnp.concatenate(es, axis=0)
    gate_ref[...] = gates


def _route(h, wrh, wrl, bias_col):
    n_tok = h.shape[0]
    col = lambda i: (0, i)
    return pl.pallas_call(
        _route_kernel,
        grid=(n_tok // ROUTE_TM,),
        in_specs=[pl.BlockSpec((ROUTE_TM, D_MODEL), lambda i: (i, 0)),
                  _const_spec(wrh.shape), _const_spec(wrl.shape), _const_spec(bias_col.shape)],
        out_specs=[pl.BlockSpec((TOP_K, ROUTE_TM), col), pl.BlockSpec((TOP_K, ROUTE_TM), col)],
        out_shape=[jax.ShapeDtypeStruct((TOP_K, n_tok), jnp.int32),
                   jax.ShapeDtypeStruct((TOP_K, n_tok), F32)],
        compiler_params=pltpu.CompilerParams(dimension_semantics=("arbitrary",)),
        name="route",
    )(h, wrh, wrl, bias_col)


GATHER_ROWS = 256


def _gather_kernel(tok_ref, src_ref, out_ref, sem):
    def row_copy(j):
        return pltpu.make_async_copy(src_ref.at[pl.ds(tok_ref[0, 0, j], 1), :],
                                     out_ref.at[pl.ds(j, 1), :], sem)

    def issue(j, carry):
        row_copy(j).start()
        return carry

    def drain(j, carry):
        row_copy(j).wait()
        return carry

    lax.fori_loop(0, GATHER_ROWS, issue, 0)
    lax.fori_loop(0, GATHER_ROWS, drain, 0)


def _gather_rows(row_tok, src):
    n_rows = row_tok.shape[0]
    width = src.shape[1]
    n_blk = n_rows // GATHER_ROWS
    return pl.pallas_call(
        _gather_kernel,
        grid=(n_blk,),
        in_specs=[pl.BlockSpec((1, 1, GATHER_ROWS), lambda i: (i, 0, 0), memory_space=pltpu.SMEM),
                  pl.BlockSpec(memory_space=pl.ANY)],
        out_specs=pl.BlockSpec((GATHER_ROWS, width), lambda i: (i, 0)),
        out_shape=jax.ShapeDtypeStruct((n_rows, width), src.dtype),
        scratch_shapes=[pltpu.SemaphoreType.DMA],
        compiler_params=pltpu.CompilerParams(dimension_semantics=("arbitrary",)),
        name="gather",
    )(row_tok.reshape(n_blk, 1, GATHER_ROWS), src)


def _unpack_pair(words):
    lo = pltpu.bitcast(words << 16, F32).astype(BF)
    hi = pltpu.bitcast(words & jnp.uint32(0xFFFF0000), F32).astype(BF)
    return lo, hi


def _pack_pair(val):
    bits = pltpu.bitcast(val.astype(BF).astype(F32), jnp.uint32)
    half = val.shape[1] // 2
    return (bits[:, :half] >> 16) | (bits[:, half:] & jnp.uint32(0xFFFF0000))


def _ffn_kernel(be_ref, x_ref, wgu_ref, wd_ref, o_ref, wgu16_ref, wd16_ref):
    i = pl.program_id(0)
    changed = jnp.logical_or(i == 0, be_ref[i] != be_ref[jnp.maximum(i - 1, 0)])

    @pl.when(changed)
    def _():
        wgu16_ref[...] = wgu_ref[0].astype(BF)
        wd16_ref[...] = wd_ref[0].astype(BF)

    half = D_MODEL // 2
    lo, hi = _unpack_pair(x_ref[...])
    gu = _dot(lo, wgu16_ref[0:half, :]) + _dot(hi, wgu16_ref[half:, :])
    act = (jax.nn.silu(gu[:, :EXPERT_FF]) * gu[:, EXPERT_FF:]).astype(BF)
    o_ref[...] = _pack_pair(_dot(act, wd16_ref[...]))


def _ffn(block_expert, xs, wgu, wd):
    n_rows = xs.shape[0]
    n_blk = n_rows // FFN_ROWS
    half = D_MODEL // 2
    grid_spec = pltpu.PrefetchScalarGridSpec(
        num_scalar_prefetch=1,
        grid=(n_blk,),
        in_specs=[pl.BlockSpec((FFN_ROWS, half), lambda i, be: (i, 0)),
                  pl.BlockSpec((1, D_MODEL, 2 * EXPERT_FF), lambda i, be: (be[i], 0, 0)),
                  pl.BlockSpec((1, EXPERT_FF, D_MODEL), lambda i, be: (be[i], 0, 0))],
        out_specs=pl.BlockSpec((FFN_ROWS, half), lambda i, be: (i, 0)),
        scratch_shapes=[pltpu.VMEM((D_MODEL, 2 * EXPERT_FF), BF), pltpu.VMEM((EXPERT_FF, D_MODEL), BF)],
    )
    return pl.pallas_call(
        _ffn_kernel,
        grid_spec=grid_spec,
        out_shape=jax.ShapeDtypeStruct((n_rows, half), jnp.uint32),
        compiler_params=pltpu.CompilerParams(dimension_semantics=("arbitrary",)),
        name="ffn",
    )(block_expert, xs, wgu, wd)


def _final_kernel(pos_ref, rows_ref, gate_ref, h_ref, wsgu_ref, wsd_ref, ln2w_ref, ln2b_ref,
                  o_ref, buf_ref, sem):
    def row_copy(kk, j):
        return pltpu.make_async_copy(rows_ref.at[pl.ds(pos_ref[0, kk, j], 1), :],
                                     buf_ref.at[kk, pl.ds(j, 1), :], sem)

    def issue(j, carry):
        for kk in range(TOP_K):
            row_copy(kk, j).start()
        return carry

    def drain(j, carry):
        for kk in range(TOP_K):
            row_copy(kk, j).wait()
        return carry

    lax.fori_loop(0, FINAL_TM, issue, 0)

    h = h_ref[...]
    hb = h.astype(BF)
    gu = _dot(hb, wsgu_ref[...])
    act = (jax.nn.silu(gu[:, :SHARED_FF]) * gu[:, SHARED_FF:]).astype(BF)
    shared = _dot(act, wsd_ref[...])

    lax.fori_loop(0, FINAL_TM, drain, 0)
    half = D_MODEL // 2
    lo_acc = jnp.zeros((FINAL_TM, half), F32)
    hi_acc = jnp.zeros((FINAL_TM, half), F32)
    gates = gate_ref[...]
    for kk in range(TOP_K):
        words = buf_ref[kk]
        gk = gates[:, kk:kk + 1]
        lo_acc = lo_acc + gk * pltpu.bitcast(words << 16, F32)
        hi_acc = hi_acc + gk * pltpu.bitcast(words & jnp.uint32(0xFFFF0000), F32)
    routed = jnp.concatenate([lo_acc, hi_acc], axis=1)
    o_ref[...] = _layer_norm(ALPHA * h + routed + shared, ln2w_ref[...], ln2b_ref[...])


def _final(pos, rows, gates_t, h, wsgu, wsd, ln2w, ln2b):
    n_tok = h.shape[0]
    n_tiles = n_tok // FINAL_TM
    half = D_MODEL // 2
    row = lambda i: (i, 0)
    consts = (wsgu, wsd, ln2w, ln2b)
    return pl.pallas_call(
        _final_kernel,
        grid=(n_tiles,),
        in_specs=[pl.BlockSpec((1, TOP_K, FINAL_TM), lambda i: (i, 0, 0), memory_space=pltpu.SMEM),
                  pl.BlockSpec(memory_space=pl.ANY),
                  pl.BlockSpec((FINAL_TM, TOP_K), row),
                  pl.BlockSpec((FINAL_TM, D_MODEL), row)] + [_const_spec(c.shape) for c in consts],
        out_specs=pl.BlockSpec((FINAL_TM, D_MODEL), row),
        out_shape=jax.ShapeDtypeStruct((n_tok, D_MODEL), F32),
        scratch_shapes=[pltpu.VMEM((TOP_K, FINAL_TM, half), jnp.uint32), pltpu.SemaphoreType.DMA],
        compiler_params=pltpu.CompilerParams(dimension_semantics=("arbitrary",)),
        name="final",
    )(pos, rows, gates_t, h, *consts)


def _np_consts():
    idx = np.arange(QUAD)
    g256 = (idx[:, None] // HEAD_DIM == idx[None, :] // HEAD_DIM).astype(np.float32)
    t = np.arange(TM)
    same = t[:, None] // SCAN_CHUNK == t[None, :] // SCAN_CHUNK
    l128 = (same & (t[None, :] <= t[:, None])).astype(np.float32)
    e128 = same.astype(np.float32)
    return jnp.asarray(g256, BF), jnp.asarray(l128, BF), jnp.asarray(e128, BF)


def _dispatch_plan(top_e):
    n_tok = top_e.shape[1]
    n_assign = n_tok * TOP_K
    eid = top_e.reshape(-1)
    order = jnp.argsort(eid).astype(jnp.int32)
    e_sorted = eid[order]
    counts = jnp.bincount(eid, length=N_EXPERTS).astype(jnp.int32)
    padded = (counts + FFN_ROWS - 1) // FFN_ROWS * FFN_ROWS
    pad_end = jnp.cumsum(padded)
    start = jnp.cumsum(counts) - counts
    dest = (pad_end - padded)[e_sorted] + jnp.arange(n_assign, dtype=jnp.int32) - start[e_sorted]
    n_blocks = (n_assign + N_EXPERTS * (FFN_ROWS - 1) + FFN_ROWS - 1) // FFN_ROWS
    n_rows = n_blocks * FFN_ROWS
    row_tok = jnp.zeros((n_rows,), jnp.int32).at[dest].set(order % n_tok)
    pos = jnp.zeros((n_assign,), jnp.int32).at[order].set(dest)
    block_expert = jnp.minimum(
        jnp.searchsorted(pad_end, jnp.arange(n_blocks, dtype=jnp.int32) * FFN_ROWS, side='right'),
        N_EXPERTS - 1).astype(jnp.int32)
    return row_tok, pos.reshape(TOP_K, n_tok), block_expert


def _layer(x2, seq, w_in, shift_mu, w0, decay_lora, a0, iclr_lora, gate_lora, k_k, k_a, r_k, gn_w, gn_b,
           lnw, lnb, spatial_w, spatial_b, w_o_rwkv, w_o_gmlp, w_out, ln1_w, ln1_b,
           w_router, router_bias, w_exp_gate_up, w_exp_down, w_sh_gate_up, w_sh_down, ln2_w, ln2_b):
    n_tok = x2.shape[0]
    c_rwkv = C_RKV + C_LORA
    pad = C_LORA_PAD - C_LORA
    win = jnp.concatenate([w_in[:, :c_rwkv], jnp.zeros((D_MODEL, pad), F32), w_in[:, c_rwkv:]], axis=1).astype(BF)
    mu = jnp.pad(shift_mu, (0, pad)).reshape(1, C_RWKV_PAD)
    row1 = lambda t: t.reshape(1, -1)
    w1 = jnp.zeros((LANES, 2 * D_MODEL), F32)
    w1 = w1.at[:DECAY_LORA, :D_MODEL].set(decay_lora).at[DECAY_LORA:, D_MODEL:].set(iclr_lora).astype(BF)
    w2 = jnp.zeros((C_LORA_PAD - LANES, D_MODEL), F32).at[:GATE_LORA].set(gate_lora).astype(BF)
    g256, l128, e128 = _np_consts()
    sb_map = jnp.repeat(spatial_b.T, D_MODEL // GMLP_GROUPS, axis=1)

    outs = _front(x2, win, mu, row1(w0), row1(a0), w1, w2, row1(k_k), row1(k_a), row1(r_k),
                  g256, l128, e128, row1(lnw), row1(lnb), spatial_w, sb_map, w_o_gmlp.astype(BF), seq)
    quads, wc, (g, bg, ga, zb) = outs[:7], outs[7], outs[8:]
    h1, h1p = _scan(quads, wc, g, bg, ga, zb, x2, row1(gn_w), row1(gn_b), g256,
                    w_o_rwkv.astype(BF), w_out.astype(BF), row1(ln1_w), row1(ln1_b), seq)

    wrt = w_router.T
    wrh = wrt.astype(BF)
    wrl = (wrt - wrh.astype(F32)).astype(BF)
    top_e, gates = _route(h1, wrh, wrl, router_bias.reshape(N_EXPERTS, 1))
    row_tok, pos, block_expert = _dispatch_plan(top_e)
    xs = _gather_rows(row_tok, h1p)
    rows = _ffn(block_expert, xs, w_exp_gate_up, w_exp_down)
    pos3 = pos.reshape(TOP_K, n_tok // FINAL_TM, FINAL_TM).transpose(1, 0, 2)
    return _final(pos3, rows, gates.T, h1, w_sh_gate_up.astype(BF), w_sh_down.astype(BF),
                  row1(ln2_w), row1(ln2_b))


def kernel(x, w_in, shift_mu, rwkv_w0, rwkv_decay_lora, rwkv_a0, rwkv_iclr_lora, rwkv_gate_lora, rwkv_k_k, rwkv_k_a, rwkv_r_k, rwkv_gn_w, rwkv_gn_b, gmlp_ln_w, gmlp_ln_b, gmlp_spatial_w, gmlp_spatial_b, w_o_rwkv, w_o_gmlp, w_out, ln1_w, ln1_b, w_router, router_bias, w_exp_gate_up, w_exp_down, w_sh_gate_up, w_sh_down, ln2_w, ln2_b):
    bsz, seq, d = x.shape
    h = x.reshape(bsz * seq, d)
    for l in range(DEPTH):
        h = _layer(h, seq, w_in[l], shift_mu[l], rwkv_w0[l], rwkv_decay_lora[l], rwkv_a0[l],
                   rwkv_iclr_lora[l], rwkv_gate_lora[l], rwkv_k_k[l], rwkv_k_a[l], rwkv_r_k[l],
                   rwkv_gn_w[l], rwkv_gn_b[l], gmlp_ln_w[l], gmlp_ln_b[l], gmlp_spatial_w[l],
                   gmlp_spatial_b[l], w_o_rwkv[l], w_o_gmlp[l], w_out[l], ln1_w[l], ln1_b[l],
                   w_router[l], router_bias[l], w_exp_gate_up[l], w_exp_down[l], w_sh_gate_up[l],
                   w_sh_down[l], ln2_w[l], ln2_b[l])
    return h.reshape(bsz, seq, d)
```

```python
import functools

import numpy as np
import jax
import jax.numpy as jnp
from jax import lax
from jax.experimental import pallas as pl
from jax.experimental.pallas import tpu as pltpu
from jax.experimental.pallas import tpu_sc as plsc

F32 = jnp.float32
BF = jnp.bfloat16

D_MODEL = 1024
HEAD_DIM = 64
N_HEADS = 16
DECAY_LORA = 64
ICLR_LORA = 64
GATE_LORA = 160
GN_EPS = 64e-5
GMLP_GROUPS = 8
GMLP_CHUNK = 128
N_EXPERTS = 256
TOP_K = 8
N_EXPERT_GROUPS = 8
TOPK_GROUPS = 4
EXPERT_FF = 256
SHARED_FF = 256
ROUTED_SCALE = 2.5
LN_EPS = 1e-5
DEPTH = 1
ALPHA = (2 * DEPTH) ** 0.25

LANES = 128
V7X_VMEM_LIMIT_BYTES = 56 * 1024 * 1024

C_RKV = 3 * D_MODEL
C_LORA = DECAY_LORA + ICLR_LORA + GATE_LORA
C_LORA_PAD = 384
C_RWKV_PAD = C_RKV + C_LORA_PAD
C_GMLP = 2 * D_MODEL
C_GATES = 2 * D_MODEL
C_IN_PAD = C_RWKV_PAD + C_GMLP + C_GATES

SCAN_CHUNK = 64
QUAD = 4 * HEAD_DIM
N_QUADS = D_MODEL // QUAD
TM = 128
ROUTE_TM = 256
FFN_ROWS = 256
FINAL_TM = 128


def _dot(a, b):
    return jnp.dot(a, b, preferred_element_type=F32)


def _dot_nt(a, b):
    return lax.dot_general(a, b, (((1,), (1,)), ((), ())), preferred_element_type=F32)


def _dot_tn(a, b):
    return lax.dot_general(a, b, (((0,), (0,)), ((), ())), preferred_element_type=F32)


def _split2(z):
    hi = z.astype(BF)
    lo = (z - hi.astype(F32)).astype(BF)
    return hi, lo


def _split3(z):
    hi = z.astype(BF)
    r1 = z - hi.astype(F32)
    mid = r1.astype(BF)
    lo = (r1 - mid.astype(F32)).astype(BF)
    return hi, mid, lo


def _headsum(z, g256):
    outs = []
    for j in range(N_QUADS):
        hi, lo = _split2(z[:, QUAD * j:QUAD * (j + 1)])
        outs.append(_dot(hi, g256) + _dot(lo, g256))
    return jnp.concatenate(outs, axis=1)


def _layer_norm(v, w, b):
    mu = jnp.mean(v, axis=-1, keepdims=True)
    d = v - mu
    var = jnp.mean(d * d, axis=-1, keepdims=True)
    return d * lax.rsqrt(var + LN_EPS) * w + b


def _const_spec(shape):
    nd = len(shape)
    return pl.BlockSpec(shape, lambda *_: (0,) * nd, pipeline_mode=pl.Buffered(1))


def _front_kernel(x_ref, win_ref, mu_ref, w0_ref, a0_ref, w1_ref, w2_ref, kk_ref, ka_ref, rk_ref,
                  g256_ref, l128_ref, e128_ref, lnw_ref, lnb_ref, ws_ref, sb_ref, wog_ref,
                  rt_ref, at_ref, bt_ref, kt_ref, vv_ref, bh_ref, kh_ref, wc_ref,
                  g_ref, bg_ref, ga_ref, zb_ref, prev_ref, *, tiles_per_seq):
    i = pl.program_id(0)
    first = (i % tiles_per_seq) == 0
    xb = x_ref[...].astype(BF)

    p = _dot(xb, win_ref[:, 0:C_RWKV_PAD])
    row = lax.broadcasted_iota(jnp.int32, (TM, 1), 0)
    prev = jnp.where(first, 0.0, prev_ref[...])
    sh = jnp.where(row == 0, prev, pltpu.roll(p, 1, 0))
    prev_ref[...] = p[TM - 1:TM, :]
    pm = p + mu_ref[...] * (sh - p)
    r = pm[:, 0:D_MODEL]
    k = pm[:, D_MODEL:2 * D_MODEL]
    v = pm[:, 2 * D_MODEL:3 * D_MODEL]

    l1 = pm[:, C_RKV:C_RKV + LANES]
    lane = lax.broadcasted_iota(jnp.int32, (1, LANES), 1)
    act1 = jnp.where(lane < DECAY_LORA, jnp.tanh(l1), l1).astype(BF)
    o1 = _dot(act1, w1_ref[...])
    w = -jax.nn.softplus(-(w0_ref[...] + o1[:, 0:D_MODEL])) - 0.5
    logd = -jnp.exp(w)
    a = jax.nn.sigmoid(a0_ref[...] + o1[:, D_MODEL:2 * D_MODEL])
    act2 = jax.nn.sigmoid(pm[:, C_RKV + LANES:C_RWKV_PAD]).astype(BF)
    g = _dot(act2, w2_ref[...])

    g256 = g256_ref[...]
    kkr = k * kk_ref[...]
    nrm = jnp.sqrt(_headsum(kkr * kkr, g256))
    kk = kkr / jnp.maximum(nrm, 1e-12)
    kmod = k * (1.0 + (a - 1.0) * ka_ref[...])
    bonus = _headsum(r * kmod * rk_ref[...], g256) * v
    kka = kk * a

    h3 = _split3(logd)
    cum = sum(_dot(l128_ref[...], part) for part in h3)
    cend = sum(_dot(e128_ref[...], part) for part in h3)
    e_in = jnp.exp(cum)
    e_ex = jnp.exp(cum - logd)
    e_inv = jnp.exp(-cum)
    e_end = jnp.exp(cend - cum)

    outs = (
        (rt_ref, r * e_in),
        (at_ref, -kk * e_ex),
        (bt_ref, kka * e_inv),
        (kt_ref, kmod * e_inv),
        (vv_ref, v),
        (bh_ref, kka * e_end),
        (kh_ref, kmod * e_end),
    )
    for ref, val in outs:
        vb = val.astype(BF)
        for q in range(N_QUADS):
            ref[q] = vb[:, QUAD * q:QUAD * (q + 1)]
    ewc = jnp.exp(cend)
    wc_ref[0] = jnp.concatenate(
        [ewc[SCAN_CHUNK * c:SCAN_CHUNK * c + 1, QUAD * q:QUAD * (q + 1)]
         for c in range(TM // SCAN_CHUNK) for q in range(N_QUADS)], axis=0)
    g_ref[...] = g.astype(BF)
    bg_ref[...] = (bonus * g).astype(BF)

    pg = _dot(xb, win_ref[:, C_RWKV_PAD:C_RWKV_PAD + C_GMLP])
    z = 0.5 * pg * (1.0 + lax.erf(pg * np.float32(np.sqrt(0.5))))
    u = z[:, 0:D_MODEL]
    vn = _layer_norm(z[:, D_MODEL:], lnw_ref[...], lnb_ref[...]).astype(BF)
    trow = lax.broadcasted_iota(jnp.int32, (GMLP_CHUNK, GMLP_CHUNK), 0)
    tcol = lax.broadcasted_iota(jnp.int32, (GMLP_CHUNK, GMLP_CHUNK), 1)
    causal = tcol <= trow
    svs = []
    for gi in range(GMLP_GROUPS):
        wsg = jnp.where(causal, ws_ref[gi], 0.0).astype(BF)
        svs.append(_dot(wsg, vn[:, LANES * gi:LANES * (gi + 1)]))
    sv = jnp.concatenate(svs, axis=1) + sb_ref[...]
    yb = _dot((u * sv).astype(BF), wog_ref[...])

    pgate = _dot(xb, win_ref[:, C_RWKV_PAD + C_GMLP:C_IN_PAD])
    ga_ref[...] = jax.nn.sigmoid(pgate[:, 0:D_MODEL]).astype(BF)
    zb_ref[...] = (jax.nn.sigmoid(pgate[:, D_MODEL:]) * yb).astype(BF)


def _front(x2, win, mu, w0, a0, w1, w2, k_k, k_a, r_k, g256, l128, e128, lnw, lnb, ws, sb, wog, seq):
    n_tok = x2.shape[0]
    n_tiles = n_tok // TM
    row = lambda i: (i, 0)
    quad_spec = pl.BlockSpec((N_QUADS, TM, QUAD), lambda i: (0, i, 0))
    quad_shape = jax.ShapeDtypeStruct((N_QUADS, n_tok, QUAD), BF)
    full_spec = pl.BlockSpec((TM, D_MODEL), row)
    full_shape = jax.ShapeDtypeStruct((n_tok, D_MODEL), BF)
    n_wc = (TM // SCAN_CHUNK) * N_QUADS
    consts = (win, mu, w0, a0, w1, w2, k_k, k_a, r_k, g256, l128, e128, lnw, lnb, ws, sb, wog)
    return pl.pallas_call(
        functools.partial(_front_kernel, tiles_per_seq=seq // TM),
        grid=(n_tiles,),
        in_specs=[pl.BlockSpec((TM, D_MODEL), row)] + [_const_spec(c.shape) for c in consts],
        out_specs=[quad_spec] * 7 + [pl.BlockSpec((1, n_wc, QUAD), lambda i: (i, 0, 0))] + [full_spec] * 4,
        out_shape=[quad_shape] * 7 + [jax.ShapeDtypeStruct((n_tiles, n_wc, QUAD), F32)] + [full_shape] * 4,
        scratch_shapes=[pltpu.VMEM((1, C_RWKV_PAD), F32)],
        compiler_params=pltpu.CompilerParams(
            dimension_semantics=("arbitrary",), vmem_limit_bytes=V7X_VMEM_LIMIT_BYTES),
        name="front",
    )(x2, *consts)


def _scan_kernel(rt_ref, at_ref, bt_ref, kt_ref, vv_ref, bh_ref, kh_ref, wc_ref,
                 g_ref, bg_ref, ga_ref, zb_ref, x_ref, gnw_ref, gnb_ref, g256_ref,
                 wor_ref, wout_ref, ln1w_ref, ln1b_ref,
                 h_ref, hp_ref, s_ref, y_ref, *, tiles_per_seq):
    i = pl.program_id(0)

    @pl.when((i % tiles_per_seq) == 0)
    def _():
        s_ref[...] = jnp.zeros_like(s_ref)

    lane = lax.broadcasted_iota(jnp.int32, (1, QUAD), 1)
    lane_head = lane // HEAD_DIM
    lane_pos = lane % HEAD_DIM
    trow = lax.broadcasted_iota(jnp.int32, (SCAN_CHUNK, 1), 0)
    strict = lane_pos < trow
    incl = lane_pos <= trow
    row_head = lax.broadcasted_iota(jnp.int32, (QUAD, 1), 0) // HEAD_DIM
    diag_blocks = row_head == lane_head

    def bd4(m):
        zero = jnp.zeros_like(m)
        return jnp.concatenate([jnp.where(lane_head == j, m, zero) for j in range(4)], axis=0)

    for q in range(N_QUADS):
        state = s_ref[q]
        for c in range(TM // SCAN_CHUNK):
            sl = slice(SCAN_CHUNK * c, SCAN_CHUNK * (c + 1))
            at = at_ref[q, sl, :]
            rt = rt_ref[q, sl, :]
            bt = bt_ref[q, sl, :]
            kt = kt_ref[q, sl, :]
            vv = vv_ref[q, sl, :]
            ra = jnp.concatenate([at, rt], axis=0)
            gb = _dot_nt(ra, bd4(bt))
            gk = _dot_nt(ra, bd4(kt))
            a_ab = jnp.where(strict, gb[:SCAN_CHUNK], 0.0)
            a_ak = jnp.where(strict, gk[:SCAN_CHUNK], 0.0)
            a_rb = jnp.where(incl, gb[SCAN_CHUNK:], 0.0)
            a_rk = jnp.where(incl, gk[SCAN_CHUNK:], 0.0)

            pw = a_ab.astype(BF)
            bdp = bd4(pw)
            tm = a_ab
            for _ in range(5):
                p2 = _dot(pw, bdp)
                pw = p2.astype(BF)
                bdp = bd4(pw)
                tm = tm + p2 + _dot(tm.astype(BF), bdp)
            tb = tm.astype(BF)
            bdv = bd4(vv)
            aw = at.astype(F32) + _dot(tb, bd4(at))
            akv = _dot(a_ak.astype(BF), bdv)
            u0 = akv + _dot(tb, bd4(akv.astype(BF)))

            sb16 = state.astype(BF)
            u = u0 + _dot_nt(aw.astype(BF), sb16)
            ub = u.astype(BF)
            y = _dot_nt(rt, sb16) + _dot(a_rb.astype(BF), bd4(ub)) + _dot(a_rk.astype(BF), bdv)
            y_ref[sl, QUAD * q:QUAD * (q + 1)] = y
            upd = _dot_tn(ub, bh_ref[q, sl, :]) + _dot_tn(vv, kh_ref[q, sl, :])
            wc = wc_ref[0, c * N_QUADS + q:c * N_QUADS + q + 1, :]
            state = state * wc + jnp.where(diag_blocks, upd, 0.0)
        s_ref[q] = state

    g256 = g256_ref[...]
    y = y_ref[...]
    inv_n = 1.0 / HEAD_DIM
    d = y - _headsum(y, g256) * inv_n
    var = _headsum(d * d, g256) * inv_n
    yn = d * lax.rsqrt(var + GN_EPS) * gnw_ref[...] + gnb_ref[...]
    ya = _dot((yn * g_ref[...].astype(F32) + bg_ref[...].astype(F32)).astype(BF), wor_ref[...])
    mix_in = ga_ref[...].astype(F32) * ya + zb_ref[...].astype(F32)
    mix = _dot(mix_in.astype(BF), wout_ref[...])
    h = _layer_norm(ALPHA * x_ref[...] + mix, ln1w_ref[...], ln1b_ref[...])
    h_ref[...] = h
    hb = pltpu.bitcast(h.astype(BF).astype(F32), jnp.uint32)
    half = D_MODEL // 2
    hp_ref[...] = (hb[:, :half] >> 16) | (hb[:, half:] & jnp.uint32(0xFFFF0000))


def _scan(quads, wc, g, bg, ga, zb, x2, gnw, gnb, g256, wor, wout, ln1w, ln1b, seq):
    n_tok = x2.shape[0]
    n_tiles = n_tok // TM
    row = lambda i: (i, 0)
    quad_spec = pl.BlockSpec((N_QUADS, TM, QUAD), lambda i: (0, i, 0))
    full_spec = pl.BlockSpec((TM, D_MODEL), row)
    n_wc = wc.shape[1]
    consts = (gnw, gnb, g256, wor, wout, ln1w, ln1b)
    return pl.pallas_call(
        functools.partial(_scan_kernel, tiles_per_seq=seq // TM),
        grid=(n_tiles,),
        in_specs=[quad_spec] * 7 + [pl.BlockSpec((1, n_wc, QUAD), lambda i: (i, 0, 0))]
        + [full_spec] * 5 + [_const_spec(c.shape) for c in consts],
        out_specs=[full_spec, pl.BlockSpec((TM, D_MODEL // 2), row)],
        out_shape=[jax.ShapeDtypeStruct((n_tok, D_MODEL), F32),
                   jax.ShapeDtypeStruct((n_tok, D_MODEL // 2), jnp.uint32)],
        scratch_shapes=[pltpu.VMEM((N_QUADS, QUAD, QUAD), F32), pltpu.VMEM((TM, D_MODEL), F32)],
        compiler_params=pltpu.CompilerParams(
            dimension_semantics=("arbitrary",), vmem_limit_bytes=V7X_VMEM_LIMIT_BYTES),
        name="scan",
    )(*quads, wc, g, bg, ga, zb, x2, *consts)


def _first_max(x, idx, n):
    m = jnp.max(x, axis=0, keepdims=True)
    first = jnp.min(jnp.where(x == m, idx, n), axis=0, keepdims=True)
    return m, first


def _route_kernel(h_ref, wrh_ref, wrl_ref, bias_ref, ustrict_ref, e_ref, gate_ref, rank_ref, cnt_ref,
                  carry_ref):
    @pl.when(pl.program_id(0) == 0)
    def _():
        carry_ref[...] = jnp.zeros_like(carry_ref)

    hh, hl = _split2(h_ref[...])
    wrh = wrh_ref[...]
    logits = _dot_nt(wrh, hh) + _dot_nt(wrh, hl) + _dot_nt(wrl_ref[...], hh)
    scores = jax.nn.sigmoid(logits)
    choice = scores + bias_ref[...]
    n_tok = choice.shape[1]
    per = N_EXPERTS // N_EXPERT_GROUPS
    eidx = lax.broadcasted_iota(jnp.int32, (N_EXPERTS, 1), 0)
    gidx = lax.broadcasted_iota(jnp.int32, (N_EXPERT_GROUPS, 1), 0)
    lidx = lax.broadcasted_iota(jnp.int32, (per, 1), 0)
    neg = -jnp.inf

    gsc = []
    for gi in range(N_EXPERT_GROUPS):
        blk = choice[per * gi:per * (gi + 1), :]
        m1, i1 = _first_max(blk, lidx, per)
        m2 = jnp.max(jnp.where(lidx == i1, neg, blk), axis=0, keepdims=True)
        gsc.append(m1 + m2)
    gscore = jnp.concatenate(gsc, axis=0)
    gsel = jnp.zeros(gscore.shape, jnp.bool_)
    for _ in range(TOPK_GROUPS):
        _, gi1 = _first_max(gscore, gidx, N_EXPERT_GROUPS)
        hit = gidx == gi1
        gsel = gsel | hit
        gscore = jnp.where(hit, neg, gscore)
    emask = jnp.concatenate(
        [jnp.broadcast_to(gsel[gi:gi + 1, :], (per, n_tok)) for gi in range(N_EXPERT_GROUPS)], axis=0)
    cand = jnp.where(emask, choice, neg)
    es, gs, hits = [], [], []
    for _ in range(TOP_K):
        _, e1 = _first_max(cand, eidx, N_EXPERTS)
        hit = eidx == e1
        es.append(e1)
        hits.append(hit)
        gs.append(jnp.sum(jnp.where(hit, scores, 0.0), axis=0, keepdims=True))
        cand = jnp.where(hit, neg, cand)
    gates = jnp.concatenate(gs, axis=0)
    gates = gates / jnp.sum(gates, axis=0, keepdims=True) * ROUTED_SCALE
    e_ref[...] = jnp.concatenate(es, axis=0)
    gate_ref[...] = gates

    chosen = functools.reduce(jnp.logical_or, hits)
    sel = jnp.where(chosen, 1.0, 0.0)
    before = _dot(sel.astype(BF), ustrict_ref[...]) + carry_ref[:, 0:1]
    rank_ref[...] = jnp.concatenate(
        [jnp.sum(jnp.where(hit, before, 0.0), axis=0, keepdims=True) for hit in hits],
        axis=0).astype(jnp.int32)
    total = before[:, n_tok - 1:n_tok] + sel[:, n_tok - 1:n_tok]
    carry_ref[...] = jnp.broadcast_to(total, carry_ref.shape)
    cnt_ref[...] = jnp.broadcast_to(total, cnt_ref.shape).astype(jnp.int32)


def _route(h, wrh, wrl, bias_col, ustrict):
    n_tok = h.shape[0]
    col = lambda i: (0, i)
    tok_spec = pl.BlockSpec((TOP_K, ROUTE_TM), col)
    return pl.pallas_call(
        _route_kernel,
        grid=(n_tok // ROUTE_TM,),
        in_specs=[pl.BlockSpec((ROUTE_TM, D_MODEL), lambda i: (i, 0)),
                  _const_spec(wrh.shape), _const_spec(wrl.shape), _const_spec(bias_col.shape),
                  _const_spec(ustrict.shape)],
        out_specs=[tok_spec, tok_spec, tok_spec, pl.BlockSpec((N_EXPERTS, LANES), lambda i: (0, 0))],
        out_shape=[jax.ShapeDtypeStruct((TOP_K, n_tok), jnp.int32),
                   jax.ShapeDtypeStruct((TOP_K, n_tok), F32),
                   jax.ShapeDtypeStruct((TOP_K, n_tok), jnp.int32),
                   jax.ShapeDtypeStruct((N_EXPERTS, LANES), jnp.int32)],
        scratch_shapes=[pltpu.VMEM((N_EXPERTS, LANES), F32)],
        compiler_params=pltpu.CompilerParams(dimension_semantics=("arbitrary",)),
        name="route",
    )(h, wrh, wrl, bias_col, ustrict)


def _rows_kernel(e_ref, rank_ref, base_ref, pos_ref):
    eidx = lax.broadcasted_iota(jnp.int32, (N_EXPERTS, 1), 0)
    base = base_ref[...]
    pos_ref[...] = rank_ref[...] + jnp.concatenate(
        [jnp.sum(jnp.where(eidx == e_ref[kk:kk + 1, :], base, 0), axis=0, keepdims=True)
         for kk in range(TOP_K)], axis=0)


def _rows(top_e, rank, base_col):
    n_tok = top_e.shape[1]
    tok_spec = pl.BlockSpec((TOP_K, ROUTE_TM), lambda i: (0, i))
    return pl.pallas_call(
        _rows_kernel,
        grid=(n_tok // ROUTE_TM,),
        in_specs=[tok_spec, tok_spec, _const_spec(base_col.shape)],
        out_specs=tok_spec,
        out_shape=jax.ShapeDtypeStruct((TOP_K, n_tok), jnp.int32),
        compiler_params=pltpu.CompilerParams(dimension_semantics=("arbitrary",)),
        name="rows",
    )(top_e, rank, base_col)


SC_WINDOW = 128


def _sc_mesh():
    return plsc.VectorSubcoreMesh(core_axis_name="c", subcore_axis_name="s")


def _sc_workers():
    info = plsc.get_sparse_core_info()
    return info.num_cores, info.num_cores * info.num_subcores


def _sc_scatter_rows(src, pos, n_rows):
    n_tok, width = src.shape
    n_k = pos.shape[0]
    n_cores, n_workers = _sc_workers()
    n_win = n_tok // (n_workers * SC_WINDOW)

    @functools.partial(
        pl.kernel, out_type=jax.ShapeDtypeStruct((n_rows, width), src.dtype), mesh=_sc_mesh(),
        scratch_types=[pltpu.VMEM((n_k * n_win, SC_WINDOW), jnp.int32),
                       pltpu.VMEM((SC_WINDOW, width), src.dtype),
                       pltpu.SemaphoreType.DMA])
    def scatter(src_hbm, pos_hbm, out_hbm, idx_v, rows_v, sem):
        wid = lax.axis_index("s") * n_cores + lax.axis_index("c")
        pltpu.sync_copy(pos_hbm.at[wid], idx_v)

        @pl.loop(0, n_win)
        def _(j):
            pltpu.sync_copy(src_hbm.at[pl.ds((wid * n_win + j) * SC_WINDOW, SC_WINDOW)], rows_v)
            copies = [pltpu.async_copy(rows_v, out_hbm.at[idx_v.at[kk * n_win + j]], sem)
                      for kk in range(n_k)]
            for cp in copies:
                cp.wait()

    pos4 = pos.reshape(n_k, n_workers, n_win, SC_WINDOW).transpose(1, 0, 2, 3)
    return scatter(src, pos4.reshape(n_workers, n_k * n_win, SC_WINDOW))


def _sc_gather_rows(src, idx):
    n_idx = idx.shape[0]
    width = src.shape[1]
    n_cores, n_workers = _sc_workers()
    n_win = n_idx // (n_workers * SC_WINDOW)

    @functools.partial(
        pl.kernel, out_type=jax.ShapeDtypeStruct((n_idx, width), src.dtype), mesh=_sc_mesh(),
        scratch_types=[pltpu.VMEM((n_win, SC_WINDOW), jnp.int32),
                       pltpu.VMEM((SC_WINDOW, width), src.dtype),
                       pltpu.SemaphoreType.DMA])
    def gather(src_hbm, idx_hbm, out_hbm, idx_v, rows_v, sem):
        wid = lax.axis_index("s") * n_cores + lax.axis_index("c")
        pltpu.sync_copy(idx_hbm.at[wid], idx_v)

        @pl.loop(0, n_win)
        def _(j):
            pltpu.async_copy(src_hbm.at[idx_v.at[j]], rows_v, sem).wait()
            pltpu.sync_copy(rows_v, out_hbm.at[pl.ds((wid * n_win + j) * SC_WINDOW, SC_WINDOW)])

    return gather(src, idx.reshape(n_workers, n_win, SC_WINDOW))


def _unpack_pair(words):
    lo = pltpu.bitcast(words << 16, F32).astype(BF)
    hi = pltpu.bitcast(words & jnp.uint32(0xFFFF0000), F32).astype(BF)
    return lo, hi


def _pack_pair(val):
    bits = pltpu.bitcast(val.astype(BF).astype(F32), jnp.uint32)
    half = val.shape[1] // 2
    return (bits[:, :half] >> 16) | (bits[:, half:] & jnp.uint32(0xFFFF0000))


def _ffn_kernel(be_ref, nv_ref, x_ref, wgu_ref, wd_ref, o_ref, wgu16_ref, wd16_ref):
    i = pl.program_id(0)
    changed = jnp.logical_or(i == 0, be_ref[i] != be_ref[jnp.maximum(i - 1, 0)])

    @pl.when(changed)
    def _():
        wgu16_ref[...] = wgu_ref[0].astype(BF)
        wd16_ref[...] = wd_ref[0].astype(BF)

    n_valid = nv_ref[i]

    @pl.when(n_valid > 0)
    def _():
        half = D_MODEL // 2
        row = lax.broadcasted_iota(jnp.int32, (FFN_ROWS, 1), 0)
        words = jnp.where(row < n_valid, x_ref[...], jnp.uint32(0))
        lo, hi = _unpack_pair(words)
        gu = _dot(lo, wgu16_ref[0:half, :]) + _dot(hi, wgu16_ref[half:, :])
        act = (jax.nn.silu(gu[:, :EXPERT_FF]) * gu[:, EXPERT_FF:]).astype(BF)
        o_ref[...] = _pack_pair(_dot(act, wd16_ref[...]))

    @pl.when(n_valid <= 0)
    def _():
        o_ref[...] = jnp.zeros_like(o_ref)


def _ffn(block_expert, n_valid, xs, wgu, wd):
    n_rows = xs.shape[0]
    n_blk = n_rows // FFN_ROWS
    half = D_MODEL // 2
    grid_spec = pltpu.PrefetchScalarGridSpec(
        num_scalar_prefetch=2,
        grid=(n_blk,),
        in_specs=[pl.BlockSpec((FFN_ROWS, half), lambda i, be, nv: (i, 0)),
                  pl.BlockSpec((1, D_MODEL, 2 * EXPERT_FF), lambda i, be, nv: (be[i], 0, 0)),
                  pl.BlockSpec((1, EXPERT_FF, D_MODEL), lambda i, be, nv: (be[i], 0, 0))],
        out_specs=pl.BlockSpec((FFN_ROWS, half), lambda i, be, nv: (i, 0)),
        scratch_shapes=[pltpu.VMEM((D_MODEL, 2 * EXPERT_FF), BF), pltpu.VMEM((EXPERT_FF, D_MODEL), BF)],
    )
    return pl.pallas_call(
        _ffn_kernel,
        grid_spec=grid_spec,
        out_shape=jax.ShapeDtypeStruct((n_rows, half), jnp.uint32),
        compiler_params=pltpu.CompilerParams(dimension_semantics=("arbitrary",)),
        name="ffn",
    )(block_expert, n_valid, xs, wgu, wd)


def _final_kernel(rows_ref, gate_ref, h_ref, wsgu_ref, wsd_ref, ln2w_ref, ln2b_ref, o_ref):
    h = h_ref[...]
    hb = h.astype(BF)
    gu = _dot(hb, wsgu_ref[...])
    act = (jax.nn.silu(gu[:, :SHARED_FF]) * gu[:, SHARED_FF:]).astype(BF)
    shared = _dot(act, wsd_ref[...])

    half = D_MODEL // 2
    lo_acc = jnp.zeros((FINAL_TM, half), F32)
    hi_acc = jnp.zeros((FINAL_TM, half), F32)
    gates = gate_ref[...]
    for kk in range(TOP_K):
        words = rows_ref[kk]
        gk = gates[:, kk:kk + 1]
        lo_acc = lo_acc + gk * pltpu.bitcast(words << 16, F32)
        hi_acc = hi_acc + gk * pltpu.bitcast(words & jnp.uint32(0xFFFF0000), F32)
    routed = jnp.concatenate([lo_acc, hi_acc], axis=1)
    o_ref[...] = _layer_norm(ALPHA * h + routed + shared, ln2w_ref[...], ln2b_ref[...])


def _final(rows, gates_t, h, wsgu, wsd, ln2w, ln2b):
    n_tok = h.shape[0]
    n_tiles = n_tok // FINAL_TM
    half = D_MODEL // 2
    row = lambda i: (i, 0)
    consts = (wsgu, wsd, ln2w, ln2b)
    return pl.pallas_call(
        _final_kernel,
        grid=(n_tiles,),
        in_specs=[pl.BlockSpec((TOP_K, FINAL_TM, half), lambda i: (0, i, 0)),
                  pl.BlockSpec((FINAL_TM, TOP_K), row),
                  pl.BlockSpec((FINAL_TM, D_MODEL), row)] + [_const_spec(c.shape) for c in consts],
        out_specs=pl.BlockSpec((FINAL_TM, D_MODEL), row),
        out_shape=jax.ShapeDtypeStruct((n_tok, D_MODEL), F32),
        compiler_params=pltpu.CompilerParams(dimension_semantics=("arbitrary",)),
        name="final",
    )(rows, gates_t, h, *consts)


def _np_consts():
    idx = np.arange(QUAD)
    g256 = (idx[:, None] // HEAD_DIM == idx[None, :] // HEAD_DIM).astype(np.float32)
    t = np.arange(TM)
    same = t[:, None] // SCAN_CHUNK == t[None, :] // SCAN_CHUNK
    l128 = (same & (t[None, :] <= t[:, None])).astype(np.float32)
    e128 = same.astype(np.float32)
    return jnp.asarray(g256, BF), jnp.asarray(l128, BF), jnp.asarray(e128, BF)


def _block_plan(counts, n_assign):
    padded = (counts + FFN_ROWS - 1) // FFN_ROWS * FFN_ROWS
    pad_end = jnp.cumsum(padded)
    base = pad_end - padded
    n_blocks = (n_assign + N_EXPERTS * (FFN_ROWS - 1) + FFN_ROWS - 1) // FFN_ROWS
    blk_row = jnp.arange(n_blocks, dtype=jnp.int32) * FFN_ROWS
    block_expert = jnp.minimum(jnp.sum(blk_row[:, None] >= pad_end[None, :], axis=1),
                               N_EXPERTS - 1).astype(jnp.int32)
    experts = jnp.arange(N_EXPERTS, dtype=jnp.int32)
    last = jnp.sum(jnp.where(block_expert[:, None] == experts[None, :], (base + counts)[None, :], 0), axis=1)
    n_valid = jnp.clip(last - blk_row, 0, FFN_ROWS).astype(jnp.int32)
    return base.astype(jnp.int32), block_expert, n_valid, n_blocks * FFN_ROWS


def _layer(x2, seq, w_in, shift_mu, w0, decay_lora, a0, iclr_lora, gate_lora, k_k, k_a, r_k, gn_w, gn_b,
           lnw, lnb, spatial_w, spatial_b, w_o_rwkv, w_o_gmlp, w_out, ln1_w, ln1_b,
           w_router, router_bias, w_exp_gate_up, w_exp_down, w_sh_gate_up, w_sh_down, ln2_w, ln2_b):
    n_tok = x2.shape[0]
    c_rwkv = C_RKV + C_LORA
    pad = C_LORA_PAD - C_LORA
    win = jnp.concatenate([w_in[:, :c_rwkv], jnp.zeros((D_MODEL, pad), F32), w_in[:, c_rwkv:]], axis=1).astype(BF)
    mu = jnp.pad(shift_mu, (0, pad)).reshape(1, C_RWKV_PAD)
    row1 = lambda t: t.reshape(1, -1)
    w1 = jnp.zeros((LANES, 2 * D_MODEL), F32)
    w1 = w1.at[:DECAY_LORA, :D_MODEL].set(decay_lora).at[DECAY_LORA:, D_MODEL:].set(iclr_lora).astype(BF)
    w2 = jnp.zeros((C_LORA_PAD - LANES, D_MODEL), F32).at[:GATE_LORA].set(gate_lora).astype(BF)
    g256, l128, e128 = _np_consts()
    sb_map = jnp.repeat(spatial_b.T, D_MODEL // GMLP_GROUPS, axis=1)

    outs = _front(x2, win, mu, row1(w0), row1(a0), w1, w2, row1(k_k), row1(k_a), row1(r_k),
                  g256, l128, e128, row1(lnw), row1(lnb), spatial_w, sb_map, w_o_gmlp.astype(BF), seq)
    quads, wc, (g, bg, ga, zb) = outs[:7], outs[7], outs[8:]
    h1, h1p = _scan(quads, wc, g, bg, ga, zb, x2, row1(gn_w), row1(gn_b), g256,
                    w_o_rwkv.astype(BF), w_out.astype(BF), row1(ln1_w), row1(ln1_b), seq)

    wrt = w_router.T
    wrh = wrt.astype(BF)
    wrl = (wrt - wrh.astype(F32)).astype(BF)
    tt = np.arange(ROUTE_TM)
    ustrict = jnp.asarray((tt[:, None] < tt[None, :]).astype(np.float32), BF)
    top_e, gates, rank, counts = _route(h1, wrh, wrl, router_bias.reshape(N_EXPERTS, 1), ustrict)
    base, block_expert, n_valid, n_rows = _block_plan(counts[:, 0], n_tok * TOP_K)
    pos = _rows(top_e, rank, base.reshape(N_EXPERTS, 1))
    xs = _sc_scatter_rows(h1p, pos, n_rows)
    rows = _ffn(block_expert, n_valid, xs, w_exp_gate_up, w_exp_down)
    picked = _sc_gather_rows(rows, pos.reshape(-1)).reshape(TOP_K, n_tok, D_MODEL // 2)
    return _final(picked, gates.T, h1, w_sh_gate_up.astype(BF), w_sh_down.astype(BF),
                  row1(ln2_w), row1(ln2_b))


def kernel(x, w_in, shift_mu, rwkv_w0, rwkv_decay_lora, rwkv_a0, rwkv_iclr_lora, rwkv_gate_lora, rwkv_k_k, rwkv_k_a, rwkv_r_k, rwkv_gn_w, rwkv_gn_b, gmlp_ln_w, gmlp_ln_b, gmlp_spatial_w, gmlp_spatial_b, w_o_rwkv, w_o_gmlp, w_out, ln1_w, ln1_b, w_router, router_bias, w_exp_gate_up, w_exp_down, w_sh_gate_up, w_sh_down, ln2_w, ln2_b):
    bsz, seq, d = x.shape
    h = x.reshape(bsz * seq, d)
    for l in range(DEPTH):
        h = _layer(h, seq, w_in[l], shift_mu[l], rwkv_w0[l], rwkv_decay_lora[l], rwkv_a0[l],
                   rwkv_iclr_lora[l], rwkv_gate_lora[l], rwkv_k_k[l], rwkv_k_a[l], rwkv_r_k[l],
                   rwkv_gn_w[l], rwkv_gn_b[l], gmlp_ln_w[l], gmlp_ln_b[l], gmlp_spatial_w[l],
                   gmlp_spatial_b[l], w_o_rwkv[l], w_o_gmlp[l], w_out[l], ln1_w[l], ln1_b[l],
                   w_router[l], router_bias[l], w_exp_gate_up[l], w_exp_down[l], w_sh_gate_up[l],
                   w_sh_down[l], ln2_w[l], ln2_b[l])
    return h.reshape(bsz, seq, d)
```

```python
import functools

import numpy as np
import jax
import jax.numpy as jnp
from jax import lax
from jax.experimental import pallas as pl
from jax.experimental.pallas import tpu as pltpu
from jax.experimental.pallas import tpu_sc as plsc

F32 = jnp.float32
BF = jnp.bfloat16

D_MODEL = 1024
HEAD_DIM = 64
N_HEADS = 16
DECAY_LORA = 64
ICLR_LORA = 64
GATE_LORA = 160
GN_EPS = 64e-5
GMLP_GROUPS = 8
GMLP_CHUNK = 128
N_EXPERTS = 256
TOP_K = 8
N_EXPERT_GROUPS = 8
TOPK_GROUPS = 4
EXPERT_FF = 256
SHARED_FF = 256
ROUTED_SCALE = 2.5
LN_EPS = 1e-5
DEPTH = 1
ALPHA = (2 * DEPTH) ** 0.25

LANES = 128
V7X_VMEM_LIMIT_BYTES = 56 * 1024 * 1024

C_RKV = 3 * D_MODEL
C_LORA = DECAY_LORA + ICLR_LORA + GATE_LORA
C_LORA_PAD = 384
C_RWKV_PAD = C_RKV + C_LORA_PAD
C_GMLP = 2 * D_MODEL
C_GATES = 2 * D_MODEL
C_IN_PAD = C_RWKV_PAD + C_GMLP + C_GATES

SCAN_CHUNK = 64
QUAD = 4 * HEAD_DIM
N_QUADS = D_MODEL // QUAD
TM = 128
SCAN_TM = 256
ROUTE_TM = 256
FFN_ROWS = 256
FINAL_TM = 128


def _dot(a, b):
    return jnp.dot(a, b, preferred_element_type=F32)


def _dot_nt(a, b):
    return lax.dot_general(a, b, (((1,), (1,)), ((), ())), preferred_element_type=F32)


def _dot_tn(a, b):
    return lax.dot_general(a, b, (((0,), (0,)), ((), ())), preferred_element_type=F32)


def _split2(z):
    hi = z.astype(BF)
    lo = (z - hi.astype(F32)).astype(BF)
    return hi, lo


def _split3(z):
    hi = z.astype(BF)
    r1 = z - hi.astype(F32)
    mid = r1.astype(BF)
    lo = (r1 - mid.astype(F32)).astype(BF)
    return hi, mid, lo


def _headsum(z, g256):
    outs = []
    for j in range(N_QUADS):
        hi, lo = _split2(z[:, QUAD * j:QUAD * (j + 1)])
        outs.append(_dot(hi, g256) + _dot(lo, g256))
    return jnp.concatenate(outs, axis=1)


def _layer_norm(v, w, b):
    mu = jnp.mean(v, axis=-1, keepdims=True)
    d = v - mu
    var = jnp.mean(d * d, axis=-1, keepdims=True)
    return d * lax.rsqrt(var + LN_EPS) * w + b


def _const_spec(shape):
    nd = len(shape)
    return pl.BlockSpec(shape, lambda *_: (0,) * nd, pipeline_mode=pl.Buffered(1))


def _front_kernel(x_ref, win_ref, mu_ref, w0_ref, a0_ref, w1_ref, w2_ref, kk_ref, ka_ref, rk_ref,
                  g256_ref, l128_ref, e128_ref, lnw_ref, lnb_ref, ws_ref, sb_ref, wog_ref,
                  rt_ref, at_ref, bt_ref, kt_ref, vv_ref, bh_ref, kh_ref, wc_ref,
                  g_ref, bg_ref, ga_ref, zb_ref, prev_ref, *, tiles_per_seq):
    i = pl.program_id(0)
    first = (i % tiles_per_seq) == 0
    xb = x_ref[...].astype(BF)

    p = _dot(xb, win_ref[:, 0:C_RWKV_PAD])
    row = lax.broadcasted_iota(jnp.int32, (TM, 1), 0)
    prev = jnp.where(first, 0.0, prev_ref[...])
    sh = jnp.where(row == 0, prev, pltpu.roll(p, 1, 0))
    prev_ref[...] = p[TM - 1:TM, :]
    pm = p + mu_ref[...] * (sh - p)
    r = pm[:, 0:D_MODEL]
    k = pm[:, D_MODEL:2 * D_MODEL]
    v = pm[:, 2 * D_MODEL:3 * D_MODEL]

    l1 = pm[:, C_RKV:C_RKV + LANES]
    lane = lax.broadcasted_iota(jnp.int32, (1, LANES), 1)
    act1 = jnp.where(lane < DECAY_LORA, jnp.tanh(l1), l1).astype(BF)
    o1 = _dot(act1, w1_ref[...])
    w = -jax.nn.softplus(-(w0_ref[...] + o1[:, 0:D_MODEL])) - 0.5
    logd = -jnp.exp(w)
    a = jax.nn.sigmoid(a0_ref[...] + o1[:, D_MODEL:2 * D_MODEL])
    act2 = jax.nn.sigmoid(pm[:, C_RKV + LANES:C_RWKV_PAD]).astype(BF)
    g = _dot(act2, w2_ref[...])

    g256 = g256_ref[...]
    kkr = k * kk_ref[...]
    nrm = jnp.sqrt(_headsum(kkr * kkr, g256))
    kk = kkr / jnp.maximum(nrm, 1e-12)
    kmod = k * (1.0 + (a - 1.0) * ka_ref[...])
    bonus = _headsum(r * kmod * rk_ref[...], g256) * v
    kka = kk * a

    h3 = _split3(logd)
    cum = sum(_dot(l128_ref[...], part) for part in h3)
    cend = sum(_dot(e128_ref[...], part) for part in h3)
    e_in = jnp.exp(cum)
    e_ex = jnp.exp(cum - logd)
    e_inv = jnp.exp(-cum)
    e_end = jnp.exp(cend - cum)

    outs = (
        (rt_ref, r * e_in),
        (at_ref, -kk * e_ex),
        (bt_ref, kka * e_inv),
        (kt_ref, kmod * e_inv),
        (vv_ref, v),
        (bh_ref, kka * e_end),
        (kh_ref, kmod * e_end),
    )
    for ref, val in outs:
        vb = val.astype(BF)
        for q in range(N_QUADS):
            ref[q] = vb[:, QUAD * q:QUAD * (q + 1)]
    ewc = jnp.exp(cend)
    wc_ref[0] = jnp.concatenate(
        [ewc[SCAN_CHUNK * c:SCAN_CHUNK * c + 1, QUAD * q:QUAD * (q + 1)]
         for c in range(TM // SCAN_CHUNK) for q in range(N_QUADS)], axis=0)
    g_ref[...] = g.astype(BF)
    bg_ref[...] = (bonus * g).astype(BF)

    pg = _dot(xb, win_ref[:, C_RWKV_PAD:C_RWKV_PAD + C_GMLP])
    z = 0.5 * pg * (1.0 + lax.erf(pg * np.float32(np.sqrt(0.5))))
    u = z[:, 0:D_MODEL]
    vn = _layer_norm(z[:, D_MODEL:], lnw_ref[...], lnb_ref[...]).astype(BF)
    trow = lax.broadcasted_iota(jnp.int32, (GMLP_CHUNK, GMLP_CHUNK), 0)
    tcol = lax.broadcasted_iota(jnp.int32, (GMLP_CHUNK, GMLP_CHUNK), 1)
    causal = tcol <= trow
    svs = []
    for gi in range(GMLP_GROUPS):
        wsg = jnp.where(causal, ws_ref[gi], 0.0).astype(BF)
        svs.append(_dot(wsg, vn[:, LANES * gi:LANES * (gi + 1)]))
    sv = jnp.concatenate(svs, axis=1) + sb_ref[...]
    yb = _dot((u * sv).astype(BF), wog_ref[...])

    pgate = _dot(xb, win_ref[:, C_RWKV_PAD + C_GMLP:C_IN_PAD])
    ga_ref[...] = jax.nn.sigmoid(pgate[:, 0:D_MODEL]).astype(BF)
    zb_ref[...] = (jax.nn.sigmoid(pgate[:, D_MODEL:]) * yb).astype(BF)


def _front(x2, win, mu, w0, a0, w1, w2, k_k, k_a, r_k, g256, l128, e128, lnw, lnb, ws, sb, wog, seq):
    n_tok = x2.shape[0]
    n_tiles = n_tok // TM
    row = lambda i: (i, 0)
    quad_spec = pl.BlockSpec((N_QUADS, TM, QUAD), lambda i: (0, i, 0))
    quad_shape = jax.ShapeDtypeStruct((N_QUADS, n_tok, QUAD), BF)
    full_spec = pl.BlockSpec((TM, D_MODEL), row)
    full_shape = jax.ShapeDtypeStruct((n_tok, D_MODEL), BF)
    n_wc = (TM // SCAN_CHUNK) * N_QUADS
    consts = (win, mu, w0, a0, w1, w2, k_k, k_a, r_k, g256, l128, e128, lnw, lnb, ws, sb, wog)
    return pl.pallas_call(
        functools.partial(_front_kernel, tiles_per_seq=seq // TM),
        grid=(n_tiles,),
        in_specs=[pl.BlockSpec((TM, D_MODEL), row)] + [_const_spec(c.shape) for c in consts],
        out_specs=[quad_spec] * 7 + [pl.BlockSpec((1, n_wc, QUAD), lambda i: (i, 0, 0))] + [full_spec] * 4,
        out_shape=[quad_shape] * 7 + [jax.ShapeDtypeStruct((n_tiles, n_wc, QUAD), F32)] + [full_shape] * 4,
        scratch_shapes=[pltpu.VMEM((1, C_RWKV_PAD), F32)],
        compiler_params=pltpu.CompilerParams(
            dimension_semantics=("arbitrary",), vmem_limit_bytes=V7X_VMEM_LIMIT_BYTES),
        name="front",
    )(x2, *consts)


def _scan_kernel(rt_ref, at_ref, bt_ref, kt_ref, vv_ref, bh_ref, kh_ref, wc_ref,
                 g_ref, bg_ref, ga_ref, zb_ref, x_ref, gnw_ref, gnb_ref, g256_ref,
                 wor_ref, wout_ref, ln1w_ref, ln1b_ref,
                 h_ref, hp_ref, s_ref, *, tiles_per_seq):
    i = pl.program_id(0)

    @pl.when((i % tiles_per_seq) == 0)
    def _():
        s_ref[...] = jnp.zeros_like(s_ref)

    lane = lax.broadcasted_iota(jnp.int32, (1, QUAD), 1)
    lane_head = lane // HEAD_DIM
    lane_pos = lane % HEAD_DIM
    trow = lax.broadcasted_iota(jnp.int32, (SCAN_CHUNK, 1), 0)
    strict = lane_pos < trow
    incl = lane_pos <= trow
    row_head = lax.broadcasted_iota(jnp.int32, (QUAD, 1), 0) // HEAD_DIM
    diag_blocks = row_head == lane_head

    def bd4(m):
        zero = jnp.zeros_like(m)
        return jnp.concatenate([jnp.where(lane_head == j, m, zero) for j in range(4)], axis=0)

    n_chunks = SCAN_TM // SCAN_CHUNK
    chunks_per_front_tile = TM // SCAN_CHUNK
    items = [(c, q) for c in range(n_chunks) for q in range(N_QUADS)]
    rows_of = lambda c: slice(SCAN_CHUNK * c, SCAN_CHUNK * (c + 1))
    C = SCAN_CHUNK

    at, rt, bdv, a_ab, a_akrk, a_rb = {}, {}, {}, {}, {}, {}
    for it in items:
        c, q = it
        at[it] = at_ref[q, rows_of(c), :]
        rt[it] = rt_ref[q, rows_of(c), :]
        ra = jnp.concatenate([at[it], rt[it]], axis=0)
        gb = _dot_nt(ra, bd4(bt_ref[q, rows_of(c), :]))
        gk = _dot_nt(ra, bd4(kt_ref[q, rows_of(c), :]))
        a_ab[it] = jnp.where(strict, gb[:C], 0.0)
        a_rb[it] = jnp.where(incl, gb[C:], 0.0).astype(BF)
        a_akrk[it] = jnp.concatenate(
            [jnp.where(strict, gk[:C], 0.0), jnp.where(incl, gk[C:], 0.0)], axis=0).astype(BF)
        bdv[it] = bd4(vv_ref[q, rows_of(c), :])

    tm, qpow = {}, {}
    for it in items:
        qb = a_ab[it].astype(BF)
        tm[it] = a_ab[it]
        qpow[it] = _dot(qb, bd4(qb))
    for level in range(1, 6):
        for it in items:
            qb = qpow[it].astype(BF)
            bdq = bd4(qb)
            if level < 5:
                both = _dot(jnp.concatenate([tm[it].astype(BF), qb], axis=0), bdq)
                tm[it] = tm[it] + qpow[it] + both[:C]
                qpow[it] = both[C:]
            else:
                tm[it] = tm[it] + qpow[it] + _dot(tm[it].astype(BF), bdq)

    awr, u0, rkv = {}, {}, {}
    for it in items:
        tb = tm[it].astype(BF)
        aw = at[it].astype(F32) + _dot(tb, bd4(at[it]))
        awr[it] = jnp.concatenate([aw.astype(BF), rt[it]], axis=0)
        kv = _dot(a_akrk[it], bdv[it])
        u0[it] = kv[:C] + _dot(tb, bd4(kv[:C].astype(BF)))
        rkv[it] = kv[C:]

    state = [s_ref[q] for q in range(N_QUADS)]
    ys = {}
    for c in range(n_chunks):
        for q in range(N_QUADS):
            it = (c, q)
            us = _dot_nt(awr[it], state[q].astype(BF))
            ub = (u0[it] + us[:C]).astype(BF)
            ys[it] = us[C:] + rkv[it] + _dot(a_rb[it], bd4(ub))
            uv = jnp.concatenate([ub, vv_ref[q, rows_of(c), :]], axis=0)
            bk = jnp.concatenate([bh_ref[q, rows_of(c), :], kh_ref[q, rows_of(c), :]], axis=0)
            wrow = (c % chunks_per_front_tile) * N_QUADS + q
            wc = wc_ref[c // chunks_per_front_tile, wrow:wrow + 1, :]
            state[q] = state[q] * wc + jnp.where(diag_blocks, _dot_tn(uv, bk), 0.0)
    for q in range(N_QUADS):
        s_ref[q] = state[q]

    g256 = g256_ref[...]
    y = jnp.concatenate(
        [jnp.concatenate([ys[(c, q)] for c in range(n_chunks)], axis=0) for q in range(N_QUADS)], axis=1)
    inv_n = 1.0 / HEAD_DIM
    d = y - _headsum(y, g256) * inv_n
    var = _headsum(d * d, g256) * inv_n
    yn = d * lax.rsqrt(var + GN_EPS) * gnw_ref[...] + gnb_ref[...]
    ya = _dot((yn * g_ref[...].astype(F32) + bg_ref[...].astype(F32)).astype(BF), wor_ref[...])
    mix_in = ga_ref[...].astype(F32) * ya + zb_ref[...].astype(F32)
    mix = _dot(mix_in.astype(BF), wout_ref[...])
    h = _layer_norm(ALPHA * x_ref[...] + mix, ln1w_ref[...], ln1b_ref[...])
    h_ref[...] = h
    hb = pltpu.bitcast(h.astype(BF).astype(F32), jnp.uint32)
    half = D_MODEL // 2
    hp_ref[...] = (hb[:, :half] >> 16) | (hb[:, half:] & jnp.uint32(0xFFFF0000))


def _scan(quads, wc, g, bg, ga, zb, x2, gnw, gnb, g256, wor, wout, ln1w, ln1b, seq):
    n_tok = x2.shape[0]
    n_tiles = n_tok // SCAN_TM
    row = lambda i: (i, 0)
    quad_spec = pl.BlockSpec((N_QUADS, SCAN_TM, QUAD), lambda i: (0, i, 0))
    full_spec = pl.BlockSpec((SCAN_TM, D_MODEL), row)
    n_wc = wc.shape[1]
    consts = (gnw, gnb, g256, wor, wout, ln1w, ln1b)
    return pl.pallas_call(
        functools.partial(_scan_kernel, tiles_per_seq=seq // SCAN_TM),
        grid=(n_tiles,),
        in_specs=[quad_spec] * 7 + [pl.BlockSpec((SCAN_TM // TM, n_wc, QUAD), lambda i: (i, 0, 0))]
        + [full_spec] * 5 + [_const_spec(c.shape) for c in consts],
        out_specs=[full_spec, pl.BlockSpec((SCAN_TM, D_MODEL // 2), row)],
        out_shape=[jax.ShapeDtypeStruct((n_tok, D_MODEL), F32),
                   jax.ShapeDtypeStruct((n_tok, D_MODEL // 2), jnp.uint32)],
        scratch_shapes=[pltpu.VMEM((N_QUADS, QUAD, QUAD), F32)],
        compiler_params=pltpu.CompilerParams(
            dimension_semantics=("arbitrary",), vmem_limit_bytes=V7X_VMEM_LIMIT_BYTES),
        name="scan",
    )(*quads, wc, g, bg, ga, zb, x2, *consts)


def _first_max(x, idx, n):
    m = jnp.max(x, axis=0, keepdims=True)
    first = jnp.min(jnp.where(x == m, idx, n), axis=0, keepdims=True)
    return m, first


def _route_kernel(h_ref, wrh_ref, wrl_ref, bias_ref, ustrict_ref, e_ref, gate_ref, rank_ref, cnt_ref,
                  carry_ref):
    @pl.when(pl.program_id(0) == 0)
    def _():
        carry_ref[...] = jnp.zeros_like(carry_ref)

    hh, hl = _split2(h_ref[...])
    wrh = wrh_ref[...]
    logits = _dot_nt(wrh, hh) + _dot_nt(wrh, hl) + _dot_nt(wrl_ref[...], hh)
    scores = jax.nn.sigmoid(logits)
    choice = scores + bias_ref[...]
    n_tok = choice.shape[1]
    per = N_EXPERTS // N_EXPERT_GROUPS
    eidx = lax.broadcasted_iota(jnp.int32, (N_EXPERTS, 1), 0)
    gidx = lax.broadcasted_iota(jnp.int32, (N_EXPERT_GROUPS, 1), 0)
    lidx = lax.broadcasted_iota(jnp.int32, (per, 1), 0)
    neg = -jnp.inf

    gsc = []
    for gi in range(N_EXPERT_GROUPS):
        blk = choice[per * gi:per * (gi + 1), :]
        m1, i1 = _first_max(blk, lidx, per)
        m2 = jnp.max(jnp.where(lidx == i1, neg, blk), axis=0, keepdims=True)
        gsc.append(m1 + m2)
    gscore = jnp.concatenate(gsc, axis=0)
    gsel = jnp.zeros(gscore.shape, jnp.bool_)
    for _ in range(TOPK_GROUPS):
        _, gi1 = _first_max(gscore, gidx, N_EXPERT_GROUPS)
        hit = gidx == gi1
        gsel = gsel | hit
        gscore = jnp.where(hit, neg, gscore)
    emask = jnp.concatenate(
        [jnp.broadcast_to(gsel[gi:gi + 1, :], (per, n_tok)) for gi in range(N_EXPERT_GROUPS)], axis=0)
    cand = jnp.where(emask, choice, neg)
    es, gs, hits = [], [], []
    for _ in range(TOP_K):
        _, e1 = _first_max(cand, eidx, N_EXPERTS)
        hit = eidx == e1
        es.append(e1)
        hits.append(hit)
        gs.append(jnp.sum(jnp.where(hit, scores, 0.0), axis=0, keepdims=True))
        cand = jnp.where(hit, neg, cand)
    gates = jnp.concatenate(gs, axis=0)
    gates = gates / jnp.sum(gates, axis=0, keepdims=True) * ROUTED_SCALE
    e_ref[...] = jnp.concatenate(es, axis=0)
    gate_ref[...] = gates

    chosen = functools.reduce(jnp.logical_or, hits)
    sel = jnp.where(chosen, 1.0, 0.0)
    before = _dot(sel.astype(BF), ustrict_ref[...]) + carry_ref[:, 0:1]
    rank_ref[...] = jnp.concatenate(
        [jnp.sum(jnp.where(hit, before, 0.0), axis=0, keepdims=True) for hit in hits],
        axis=0).astype(jnp.int32)
    total = before[:, n_tok - 1:n_tok] + sel[:, n_tok - 1:n_tok]
    carry_ref[...] = jnp.broadcast_to(total, carry_ref.shape)
    cnt_ref[...] = jnp.broadcast_to(total, cnt_ref.shape).astype(jnp.int32)


def _route(h, wrh, wrl, bias_col, ustrict):
    n_tok = h.shape[0]
    col = lambda i: (0, i)
    tok_spec = pl.BlockSpec((TOP_K, ROUTE_TM), col)
    return pl.pallas_call(
        _route_kernel,
        grid=(n_tok // ROUTE_TM,),
        in_specs=[pl.BlockSpec((ROUTE_TM, D_MODEL), lambda i: (i, 0)),
                  _const_spec(wrh.shape), _const_spec(wrl.shape), _const_spec(bias_col.shape),
                  _const_spec(ustrict.shape)],
        out_specs=[tok_spec, tok_spec, tok_spec, pl.BlockSpec((N_EXPERTS, LANES), lambda i: (0, 0))],
        out_shape=[jax.ShapeDtypeStruct((TOP_K, n_tok), jnp.int32),
                   jax.ShapeDtypeStruct((TOP_K, n_tok), F32),
                   jax.ShapeDtypeStruct((TOP_K, n_tok), jnp.int32),
                   jax.ShapeDtypeStruct((N_EXPERTS, LANES), jnp.int32)],
        scratch_shapes=[pltpu.VMEM((N_EXPERTS, LANES), F32)],
        compiler_params=pltpu.CompilerParams(dimension_semantics=("arbitrary",)),
        name="route",
    )(h, wrh, wrl, bias_col, ustrict)


def _rows_kernel(e_ref, rank_ref, base_ref, pos_ref):
    eidx = lax.broadcasted_iota(jnp.int32, (N_EXPERTS, 1), 0)
    base = base_ref[...]
    pos_ref[...] = rank_ref[...] + jnp.concatenate(
        [jnp.sum(jnp.where(eidx == e_ref[kk:kk + 1, :], base, 0), axis=0, keepdims=True)
         for kk in range(TOP_K)], axis=0)


def _rows(top_e, rank, base_col):
    n_tok = top_e.shape[1]
    tok_spec = pl.BlockSpec((TOP_K, ROUTE_TM), lambda i: (0, i))
    return pl.pallas_call(
        _rows_kernel,
        grid=(n_tok // ROUTE_TM,),
        in_specs=[tok_spec, tok_spec, _const_spec(base_col.shape)],
        out_specs=tok_spec,
        out_shape=jax.ShapeDtypeStruct((TOP_K, n_tok), jnp.int32),
        compiler_params=pltpu.CompilerParams(dimension_semantics=("arbitrary",)),
        name="rows",
    )(top_e, rank, base_col)


SC_WINDOW = 128


def _sc_mesh():
    return plsc.VectorSubcoreMesh(core_axis_name="c", subcore_axis_name="s")


def _sc_workers():
    info = plsc.get_sparse_core_info()
    return info.num_cores, info.num_cores * info.num_subcores


def _sc_scatter_rows(src, pos, n_rows):
    n_tok, width = src.shape
    n_k = pos.shape[0]
    n_cores, n_workers = _sc_workers()
    n_win = n_tok // (n_workers * SC_WINDOW)

    @functools.partial(
        pl.kernel, out_type=jax.ShapeDtypeStruct((n_rows, width), src.dtype), mesh=_sc_mesh(),
        scratch_types=[pltpu.VMEM((n_k * n_win, SC_WINDOW), jnp.int32),
                       pltpu.VMEM((SC_WINDOW, width), src.dtype),
                       pltpu.SemaphoreType.DMA])
    def scatter(src_hbm, pos_hbm, out_hbm, idx_v, rows_v, sem):
        wid = lax.axis_index("s") * n_cores + lax.axis_index("c")
        pltpu.sync_copy(pos_hbm.at[wid], idx_v)

        @pl.loop(0, n_win)
        def _(j):
            pltpu.sync_copy(src_hbm.at[pl.ds((wid * n_win + j) * SC_WINDOW, SC_WINDOW)], rows_v)
            copies = [pltpu.async_copy(rows_v, out_hbm.at[idx_v.at[kk * n_win + j]], sem)
                      for kk in range(n_k)]
            for cp in copies:
                cp.wait()

    pos4 = pos.reshape(n_k, n_workers, n_win, SC_WINDOW).transpose(1, 0, 2, 3)
    return scatter(src, pos4.reshape(n_workers, n_k * n_win, SC_WINDOW))


def _sc_gather_rows(src, idx):
    n_idx = idx.shape[0]
    width = src.shape[1]
    n_cores, n_workers = _sc_workers()
    n_win = n_idx // (n_workers * SC_WINDOW)

    @functools.partial(
        pl.kernel, out_type=jax.ShapeDtypeStruct((n_idx, width), src.dtype), mesh=_sc_mesh(),
        scratch_types=[pltpu.VMEM((n_win, SC_WINDOW), jnp.int32),
                       pltpu.VMEM((SC_WINDOW, width), src.dtype),
                       pltpu.SemaphoreType.DMA])
    def gather(src_hbm, idx_hbm, out_hbm, idx_v, rows_v, sem):
        wid = lax.axis_index("s") * n_cores + lax.axis_index("c")
        pltpu.sync_copy(idx_hbm.at[wid], idx_v)

        @pl.loop(0, n_win)
        def _(j):
            pltpu.async_copy(src_hbm.at[idx_v.at[j]], rows_v, sem).wait()
            pltpu.sync_copy(rows_v, out_hbm.at[pl.ds((wid * n_win + j) * SC_WINDOW, SC_WINDOW)])

    return gather(src, idx.reshape(n_workers, n_win, SC_WINDOW))


def _unpack_pair(words):
    lo = pltpu.bitcast(words << 16, F32).astype(BF)
    hi = pltpu.bitcast(words & jnp.uint32(0xFFFF0000), F32).astype(BF)
    return lo, hi


def _pack_pair(val):
    bits = pltpu.bitcast(val.astype(BF).astype(F32), jnp.uint32)
    half = val.shape[1] // 2
    return (bits[:, :half] >> 16) | (bits[:, half:] & jnp.uint32(0xFFFF0000))


def _ffn_kernel(be_ref, nv_ref, x_ref, wgu_ref, wd_ref, o_ref, wgu16_ref, wd16_ref):
    i = pl.program_id(0)
    changed = jnp.logical_or(i == 0, be_ref[i] != be_ref[jnp.maximum(i - 1, 0)])

    @pl.when(changed)
    def _():
        wgu16_ref[...] = wgu_ref[0].astype(BF)
        wd16_ref[...] = wd_ref[0].astype(BF)

    n_valid = nv_ref[i]

    @pl.when(n_valid > 0)
    def _():
        half = D_MODEL // 2
        row = lax.broadcasted_iota(jnp.int32, (FFN_ROWS, 1), 0)
        words = jnp.where(row < n_valid, x_ref[...], jnp.uint32(0))
        lo, hi = _unpack_pair(words)
        gu = _dot(lo, wgu16_ref[0:half, :]) + _dot(hi, wgu16_ref[half:, :])
        act = (jax.nn.silu(gu[:, :EXPERT_FF]) * gu[:, EXPERT_FF:]).astype(BF)
        o_ref[...] = _pack_pair(_dot(act, wd16_ref[...]))

    @pl.when(n_valid <= 0)
    def _():
        o_ref[...] = jnp.zeros_like(o_ref)


def _ffn(block_expert, n_valid, xs, wgu, wd):
    n_rows = xs.shape[0]
    n_blk = n_rows // FFN_ROWS
    half = D_MODEL // 2
    grid_spec = pltpu.PrefetchScalarGridSpec(
        num_scalar_prefetch=2,
        grid=(n_blk,),
        in_specs=[pl.BlockSpec((FFN_ROWS, half), lambda i, be, nv: (i, 0)),
                  pl.BlockSpec((1, D_MODEL, 2 * EXPERT_FF), lambda i, be, nv: (be[i], 0, 0)),
                  pl.BlockSpec((1, EXPERT_FF, D_MODEL), lambda i, be, nv: (be[i], 0, 0))],
        out_specs=pl.BlockSpec((FFN_ROWS, half), lambda i, be, nv: (i, 0)),
        scratch_shapes=[pltpu.VMEM((D_MODEL, 2 * EXPERT_FF), BF), pltpu.VMEM((EXPERT_FF, D_MODEL), BF)],
    )
    return pl.pallas_call(
        _ffn_kernel,
        grid_spec=grid_spec,
        out_shape=jax.ShapeDtypeStruct((n_rows, half), jnp.uint32),
        compiler_params=pltpu.CompilerParams(dimension_semantics=("arbitrary",)),
        name="ffn",
    )(block_expert, n_valid, xs, wgu, wd)


def _final_kernel(rows_ref, gate_ref, h_ref, wsgu_ref, wsd_ref, ln2w_ref, ln2b_ref, o_ref):
    h = h_ref[...]
    hb = h.astype(BF)
    gu = _dot(hb, wsgu_ref[...])
    act = (jax.nn.silu(gu[:, :SHARED_FF]) * gu[:, SHARED_FF:]).astype(BF)
    shared = _dot(act, wsd_ref[...])

    half = D_MODEL // 2
    lo_acc = jnp.zeros((FINAL_TM, half), F32)
    hi_acc = jnp.zeros((FINAL_TM, half), F32)
    gates = gate_ref[...]
    for kk in range(TOP_K):
        words = rows_ref[kk]
        gk = gates[:, kk:kk + 1]
        lo_acc = lo_acc + gk * pltpu.bitcast(words << 16, F32)
        hi_acc = hi_acc + gk * pltpu.bitcast(words & jnp.uint32(0xFFFF0000), F32)
    routed = jnp.concatenate([lo_acc, hi_acc], axis=1)
    o_ref[...] = _layer_norm(ALPHA * h + routed + shared, ln2w_ref[...], ln2b_ref[...])


def _final(rows, gates_t, h, wsgu, wsd, ln2w, ln2b):
    n_tok = h.shape[0]
    n_tiles = n_tok // FINAL_TM
    half = D_MODEL // 2
    row = lambda i: (i, 0)
    consts = (wsgu, wsd, ln2w, ln2b)
    return pl.pallas_call(
        _final_kernel,
        grid=(n_tiles,),
        in_specs=[pl.BlockSpec((TOP_K, FINAL_TM, half), lambda i: (0, i, 0)),
                  pl.BlockSpec((FINAL_TM, TOP_K), row),
                  pl.BlockSpec((FINAL_TM, D_MODEL), row)] + [_const_spec(c.shape) for c in consts],
        out_specs=pl.BlockSpec((FINAL_TM, D_MODEL), row),
        out_shape=jax.ShapeDtypeStruct((n_tok, D_MODEL), F32),
        compiler_params=pltpu.CompilerParams(dimension_semantics=("arbitrary",)),
        name="final",
    )(rows, gates_t, h, *consts)


def _np_consts():
    idx = np.arange(QUAD)
    g256 = (idx[:, None] // HEAD_DIM == idx[None, :] // HEAD_DIM).astype(np.float32)
    t = np.arange(TM)
    same = t[:, None] // SCAN_CHUNK == t[None, :] // SCAN_CHUNK
    l128 = (same & (t[None, :] <= t[:, None])).astype(np.float32)
    e128 = same.astype(np.float32)
    return jnp.asarray(g256, BF), jnp.asarray(l128, BF), jnp.asarray(e128, BF)


def _block_plan(counts, n_assign):
    padded = (counts + FFN_ROWS - 1) // FFN_ROWS * FFN_ROWS
    pad_end = jnp.cumsum(padded)
    base = pad_end - padded
    n_blocks = (n_assign + N_EXPERTS * (FFN_ROWS - 1) + FFN_ROWS - 1) // FFN_ROWS
    blk_row = jnp.arange(n_blocks, dtype=jnp.int32) * FFN_ROWS
    block_expert = jnp.minimum(jnp.sum(blk_row[:, None] >= pad_end[None, :], axis=1),
                               N_EXPERTS - 1).astype(jnp.int32)
    experts = jnp.arange(N_EXPERTS, dtype=jnp.int32)
    last = jnp.sum(jnp.where(block_expert[:, None] == experts[None, :], (base + counts)[None, :], 0), axis=1)
    n_valid = jnp.clip(last - blk_row, 0, FFN_ROWS).astype(jnp.int32)
    return base.astype(jnp.int32), block_expert, n_valid, n_blocks * FFN_ROWS


def _layer(x2, seq, w_in, shift_mu, w0, decay_lora, a0, iclr_lora, gate_lora, k_k, k_a, r_k, gn_w, gn_b,
           lnw, lnb, spatial_w, spatial_b, w_o_rwkv, w_o_gmlp, w_out, ln1_w, ln1_b,
           w_router, router_bias, w_exp_gate_up, w_exp_down, w_sh_gate_up, w_sh_down, ln2_w, ln2_b):
    n_tok = x2.shape[0]
    c_rwkv = C_RKV + C_LORA
    pad = C_LORA_PAD - C_LORA
    win = jnp.concatenate([w_in[:, :c_rwkv], jnp.zeros((D_MODEL, pad), F32), w_in[:, c_rwkv:]], axis=1).astype(BF)
    mu = jnp.pad(shift_mu, (0, pad)).reshape(1, C_RWKV_PAD)
    row1 = lambda t: t.reshape(1, -1)
    w1 = jnp.zeros((LANES, 2 * D_MODEL), F32)
    w1 = w1.at[:DECAY_LORA, :D_MODEL].set(decay_lora).at[DECAY_LORA:, D_MODEL:].set(iclr_lora).astype(BF)
    w2 = jnp.zeros((C_LORA_PAD - LANES, D_MODEL), F32).at[:GATE_LORA].set(gate_lora).astype(BF)
    g256, l128, e128 = _np_consts()
    sb_map = jnp.repeat(spatial_b.T, D_MODEL // GMLP_GROUPS, axis=1)

    outs = _front(x2, win, mu, row1(w0), row1(a0), w1, w2, row1(k_k), row1(k_a), row1(r_k),
                  g256, l128, e128, row1(lnw), row1(lnb), spatial_w, sb_map, w_o_gmlp.astype(BF), seq)
    quads, wc, (g, bg, ga, zb) = outs[:7], outs[7], outs[8:]
    h1, h1p = _scan(quads, wc, g, bg, ga, zb, x2, row1(gn_w), row1(gn_b), g256,
                    w_o_rwkv.astype(BF), w_out.astype(BF), row1(ln1_w), row1(ln1_b), seq)

    wrt = w_router.T
    wrh = wrt.astype(BF)
    wrl = (wrt - wrh.astype(F32)).astype(BF)
    tt = np.arange(ROUTE_TM)
    ustrict = jnp.asarray((tt[:, None] < tt[None, :]).astype(np.float32), BF)
    top_e, gates, rank, counts = _route(h1, wrh, wrl, router_bias.reshape(N_EXPERTS, 1), ustrict)
    base, block_expert, n_valid, n_rows = _block_plan(counts[:, 0], n_tok * TOP_K)
    pos = _rows(top_e, rank, base.reshape(N_EXPERTS, 1))
    xs = _sc_scatter_rows(h1p, pos, n_rows)
    rows = _ffn(block_expert, n_valid, xs, w_exp_gate_up, w_exp_down)
    picked = _sc_gather_rows(rows, pos.reshape(-1)).reshape(TOP_K, n_tok, D_MODEL // 2)
    return _final(picked, gates.T, h1, w_sh_gate_up.astype(BF), w_sh_down.astype(BF),
                  row1(ln2_w), row1(ln2_b))


def kernel(x, w_in, shift_mu, rwkv_w0, rwkv_decay_lora, rwkv_a0, rwkv_iclr_lora, rwkv_gate_lora, rwkv_k_k, rwkv_k_a, rwkv_r_k, rwkv_gn_w, rwkv_gn_b, gmlp_ln_w, gmlp_ln_b, gmlp_spatial_w, gmlp_spatial_b, w_o_rwkv, w_o_gmlp, w_out, ln1_w, ln1_b, w_router, router_bias, w_exp_gate_up, w_exp_down, w_sh_gate_up, w_sh_down, ln2_w, ln2_b):
    bsz, seq, d = x.shape
    h = x.reshape(bsz * seq, d)
    for l in range(DEPTH):
        h = _layer(h, seq, w_in[l], shift_mu[l], rwkv_w0[l], rwkv_decay_lora[l], rwkv_a0[l],
                   rwkv_iclr_lora[l], rwkv_gate_lora[l], rwkv_k_k[l], rwkv_k_a[l], rwkv_r_k[l],
                   rwkv_gn_w[l], rwkv_gn_b[l], gmlp_ln_w[l], gmlp_ln_b[l], gmlp_spatial_w[l],
                   gmlp_spatial_b[l], w_o_rwkv[l], w_o_gmlp[l], w_out[l], ln1_w[l], ln1_b[l],
                   w_router[l], router_bias[l], w_exp_gate_up[l], w_exp_down[l], w_sh_gate_up[l],
                   w_sh_down[l], ln2_w[l], ln2_b[l])
    return h.reshape(bsz, seq, d)
```

```python
import functools

import numpy as np
import jax
import jax.numpy as jnp
from jax import lax
from jax.experimental import pallas as pl
from jax.experimental.pallas import tpu as pltpu
from jax.experimental.pallas import tpu_sc as plsc

F32 = jnp.float32
BF = jnp.bfloat16

D_MODEL = 1024
HEAD_DIM = 64
N_HEADS = 16
DECAY_LORA = 64
ICLR_LORA = 64
GATE_LORA = 160
GN_EPS = 64e-5
GMLP_GROUPS = 8
GMLP_CHUNK = 128
N_EXPERTS = 256
TOP_K = 8
N_EXPERT_GROUPS = 8
TOPK_GROUPS = 4
EXPERT_FF = 256
SHARED_FF = 256
ROUTED_SCALE = 2.5
LN_EPS = 1e-5
DEPTH = 1
ALPHA = (2 * DEPTH) ** 0.25

LANES = 128
V7X_VMEM_LIMIT_BYTES = 56 * 1024 * 1024

C_RKV = 3 * D_MODEL
C_LORA = DECAY_LORA + ICLR_LORA + GATE_LORA
C_LORA_PAD = 384
C_RWKV_PAD = C_RKV + C_LORA_PAD
C_GMLP = 2 * D_MODEL
C_GATES = 2 * D_MODEL
C_IN_PAD = C_RWKV_PAD + C_GMLP + C_GATES

SCAN_CHUNK = 64
QUAD = 4 * HEAD_DIM
N_QUADS = D_MODEL // QUAD
TM = 256
SCAN_TM = 256
ROUTE_TM = 256
FFN_ROWS = 512
FINAL_TM = 128


def _dot(a, b):
    return jnp.dot(a, b, preferred_element_type=F32)


def _dot_nt(a, b):
    return lax.dot_general(a, b, (((1,), (1,)), ((), ())), preferred_element_type=F32)


def _dot_tn(a, b):
    return lax.dot_general(a, b, (((0,), (0,)), ((), ())), preferred_element_type=F32)


def _split2(z):
    hi = z.astype(BF)
    lo = (z - hi.astype(F32)).astype(BF)
    return hi, lo


def _split3(z):
    hi = z.astype(BF)
    r1 = z - hi.astype(F32)
    mid = r1.astype(BF)
    lo = (r1 - mid.astype(F32)).astype(BF)
    return hi, mid, lo


def _headsum(z, g256):
    outs = []
    for j in range(N_QUADS):
        hi, lo = _split2(z[:, QUAD * j:QUAD * (j + 1)])
        outs.append(_dot(hi, g256) + _dot(lo, g256))
    return jnp.concatenate(outs, axis=1)


def _layer_norm(v, w, b):
    mu = jnp.mean(v, axis=-1, keepdims=True)
    d = v - mu
    var = jnp.mean(d * d, axis=-1, keepdims=True)
    return d * lax.rsqrt(var + LN_EPS) * w + b


def _const_spec(shape):
    nd = len(shape)
    return pl.BlockSpec(shape, lambda *_: (0,) * nd, pipeline_mode=pl.Buffered(1))


def _front_kernel(x_ref, win_ref, mu_ref, w0_ref, a0_ref, w1_ref, w2_ref, kk_ref, ka_ref, rk_ref,
                  g256_ref, ltri_ref, lall_ref, lnw_ref, lnb_ref, ws_ref, sb_ref, wog_ref,
                  rt_ref, at_ref, bt_ref, kt_ref, vv_ref, bh_ref, kh_ref, wc_ref,
                  g_ref, bg_ref, ga_ref, zb_ref, prev_ref, *, tiles_per_seq):
    i = pl.program_id(0)
    first = (i % tiles_per_seq) == 0
    xb = x_ref[...].astype(BF)

    p = _dot(xb, win_ref[:, 0:C_RWKV_PAD])
    row = lax.broadcasted_iota(jnp.int32, (TM, 1), 0)
    prev = jnp.where(first, 0.0, prev_ref[...])
    sh = jnp.where(row == 0, prev, pltpu.roll(p, 1, 0))
    prev_ref[...] = p[TM - 1:TM, :]
    pm = p + mu_ref[...] * (sh - p)
    r = pm[:, 0:D_MODEL]
    k = pm[:, D_MODEL:2 * D_MODEL]
    v = pm[:, 2 * D_MODEL:3 * D_MODEL]

    l1 = pm[:, C_RKV:C_RKV + LANES]
    lane = lax.broadcasted_iota(jnp.int32, (1, LANES), 1)
    act1 = jnp.where(lane < DECAY_LORA, jnp.tanh(l1), l1).astype(BF)
    o1 = _dot(act1, w1_ref[...])
    w = -jax.nn.softplus(-(w0_ref[...] + o1[:, 0:D_MODEL])) - 0.5
    logd = -jnp.exp(w)
    a = jax.nn.sigmoid(a0_ref[...] + o1[:, D_MODEL:2 * D_MODEL])
    act2 = jax.nn.sigmoid(pm[:, C_RKV + LANES:C_RWKV_PAD]).astype(BF)
    g = _dot(act2, w2_ref[...])

    g256 = g256_ref[...]
    kkr = k * kk_ref[...]
    nrm = jnp.sqrt(_headsum(kkr * kkr, g256))
    kk = kkr / jnp.maximum(nrm, 1e-12)
    kmod = k * (1.0 + (a - 1.0) * ka_ref[...])
    bonus = _headsum(r * kmod * rk_ref[...], g256) * v
    kka = kk * a

    h3 = _split3(logd)
    cum = sum(_dot(ltri_ref[...], part) for part in h3)
    cend = sum(_dot(lall_ref[...], part) for part in h3)
    e_in = jnp.exp(cum)
    e_ex = jnp.exp(cum - logd)
    e_inv = jnp.exp(-cum)
    e_end = jnp.exp(cend - cum)

    outs = (
        (rt_ref, r * e_in),
        (at_ref, -kk * e_ex),
        (bt_ref, kka * e_inv),
        (kt_ref, kmod * e_inv),
        (vv_ref, v),
        (bh_ref, kka * e_end),
        (kh_ref, kmod * e_end),
    )
    for ref, val in outs:
        vb = val.astype(BF)
        for q in range(N_QUADS):
            ref[q] = vb[:, QUAD * q:QUAD * (q + 1)]
    ewc = jnp.exp(cend)
    wc_ref[0] = jnp.concatenate(
        [ewc[SCAN_CHUNK * c:SCAN_CHUNK * c + 1, QUAD * q:QUAD * (q + 1)]
         for c in range(TM // SCAN_CHUNK) for q in range(N_QUADS)], axis=0)
    g_ref[...] = g.astype(BF)
    bg_ref[...] = (bonus * g).astype(BF)

    pg = _dot(xb, win_ref[:, C_RWKV_PAD:C_RWKV_PAD + C_GMLP])
    z = 0.5 * pg * (1.0 + lax.erf(pg * np.float32(np.sqrt(0.5))))
    u = z[:, 0:D_MODEL]
    vn = _layer_norm(z[:, D_MODEL:], lnw_ref[...], lnb_ref[...]).astype(BF)
    trow = lax.broadcasted_iota(jnp.int32, (GMLP_CHUNK, GMLP_CHUNK), 0)
    tcol = lax.broadcasted_iota(jnp.int32, (GMLP_CHUNK, GMLP_CHUNK), 1)
    causal = tcol <= trow
    svs = []
    for gi in range(GMLP_GROUPS):
        wsg = jnp.where(causal, ws_ref[gi], 0.0).astype(BF)
        svs.append(jnp.concatenate(
            [_dot(wsg, vn[GMLP_CHUNK * cc:GMLP_CHUNK * (cc + 1), LANES * gi:LANES * (gi + 1)])
             for cc in range(TM // GMLP_CHUNK)], axis=0))
    sb = sb_ref[...]
    sv = jnp.concatenate(svs, axis=1) + jnp.concatenate([sb] * (TM // GMLP_CHUNK), axis=0)
    yb = _dot((u * sv).astype(BF), wog_ref[...])

    pgate = _dot(xb, win_ref[:, C_RWKV_PAD + C_GMLP:C_IN_PAD])
    ga_ref[...] = jax.nn.sigmoid(pgate[:, 0:D_MODEL]).astype(BF)
    zb_ref[...] = (jax.nn.sigmoid(pgate[:, D_MODEL:]) * yb).astype(BF)


def _front(x2, win, mu, w0, a0, w1, w2, k_k, k_a, r_k, g256, ltri, lall, lnw, lnb, ws, sb, wog, seq):
    n_tok = x2.shape[0]
    n_tiles = n_tok // TM
    row = lambda i: (i, 0)
    quad_spec = pl.BlockSpec((N_QUADS, TM, QUAD), lambda i: (0, i, 0))
    quad_shape = jax.ShapeDtypeStruct((N_QUADS, n_tok, QUAD), BF)
    full_spec = pl.BlockSpec((TM, D_MODEL), row)
    full_shape = jax.ShapeDtypeStruct((n_tok, D_MODEL), BF)
    n_wc = (TM // SCAN_CHUNK) * N_QUADS
    consts = (win, mu, w0, a0, w1, w2, k_k, k_a, r_k, g256, ltri, lall, lnw, lnb, ws, sb, wog)
    return pl.pallas_call(
        functools.partial(_front_kernel, tiles_per_seq=seq // TM),
        grid=(n_tiles,),
        in_specs=[pl.BlockSpec((TM, D_MODEL), row)] + [_const_spec(c.shape) for c in consts],
        out_specs=[quad_spec] * 7 + [pl.BlockSpec((1, n_wc, QUAD), lambda i: (i, 0, 0))] + [full_spec] * 4,
        out_shape=[quad_shape] * 7 + [jax.ShapeDtypeStruct((n_tiles, n_wc, QUAD), F32)] + [full_shape] * 4,
        scratch_shapes=[pltpu.VMEM((1, C_RWKV_PAD), F32)],
        compiler_params=pltpu.CompilerParams(
            dimension_semantics=("arbitrary",), vmem_limit_bytes=V7X_VMEM_LIMIT_BYTES),
        name="front",
    )(x2, *consts)


def _scan_kernel(rt_ref, at_ref, bt_ref, kt_ref, vv_ref, bh_ref, kh_ref, wc_ref,
                 g_ref, bg_ref, ga_ref, zb_ref, x_ref, gnw_ref, gnb_ref, g256_ref,
                 wor_ref, wout_ref, ln1w_ref, ln1b_ref,
                 h_ref, hp_ref, s_ref, *, tiles_per_seq):
    i = pl.program_id(0)

    @pl.when((i % tiles_per_seq) == 0)
    def _():
        s_ref[...] = jnp.zeros_like(s_ref)

    lane = lax.broadcasted_iota(jnp.int32, (1, QUAD), 1)
    lane_head = lane // HEAD_DIM
    lane_pos = lane % HEAD_DIM
    trow = lax.broadcasted_iota(jnp.int32, (SCAN_CHUNK, 1), 0)
    strict = lane_pos < trow
    incl = lane_pos <= trow
    row_head = lax.broadcasted_iota(jnp.int32, (QUAD, 1), 0) // HEAD_DIM
    diag_blocks = row_head == lane_head

    def bd4(m):
        zero = jnp.zeros_like(m)
        return jnp.concatenate([jnp.where(lane_head == j, m, zero) for j in range(4)], axis=0)

    n_chunks = SCAN_TM // SCAN_CHUNK
    chunks_per_front_tile = TM // SCAN_CHUNK
    items = [(c, q) for c in range(n_chunks) for q in range(N_QUADS)]
    rows_of = lambda c: slice(SCAN_CHUNK * c, SCAN_CHUNK * (c + 1))
    C = SCAN_CHUNK

    at, rt, bdv, a_ab, a_akrk, a_rb = {}, {}, {}, {}, {}, {}
    for it in items:
        c, q = it
        at[it] = at_ref[q, rows_of(c), :]
        rt[it] = rt_ref[q, rows_of(c), :]
        ra = jnp.concatenate([at[it], rt[it]], axis=0)
        gb = _dot_nt(ra, bd4(bt_ref[q, rows_of(c), :]))
        gk = _dot_nt(ra, bd4(kt_ref[q, rows_of(c), :]))
        a_ab[it] = jnp.where(strict, gb[:C], 0.0)
        a_rb[it] = jnp.where(incl, gb[C:], 0.0).astype(BF)
        a_akrk[it] = jnp.concatenate(
            [jnp.where(strict, gk[:C], 0.0), jnp.where(incl, gk[C:], 0.0)], axis=0).astype(BF)
        bdv[it] = bd4(vv_ref[q, rows_of(c), :])

    tm, qpow = {}, {}
    for it in items:
        qb = a_ab[it].astype(BF)
        tm[it] = a_ab[it]
        qpow[it] = _dot(qb, bd4(qb))
    for level in range(1, 6):
        for it in items:
            qb = qpow[it].astype(BF)
            bdq = bd4(qb)
            if level < 5:
                both = _dot(jnp.concatenate([tm[it].astype(BF), qb], axis=0), bdq)
                tm[it] = tm[it] + qpow[it] + both[:C]
                qpow[it] = both[C:]
            else:
                tm[it] = tm[it] + qpow[it] + _dot(tm[it].astype(BF), bdq)

    awr, u0, rkv = {}, {}, {}
    for it in items:
        tb = tm[it].astype(BF)
        aw = at[it].astype(F32) + _dot(tb, bd4(at[it]))
        awr[it] = jnp.concatenate([aw.astype(BF), rt[it]], axis=0)
        kv = _dot(a_akrk[it], bdv[it])
        u0[it] = kv[:C] + _dot(tb, bd4(kv[:C].astype(BF)))
        rkv[it] = kv[C:]

    state = [s_ref[q] for q in range(N_QUADS)]
    ys = {}
    for c in range(n_chunks):
        for q in range(N_QUADS):
            it = (c, q)
            us = _dot_nt(awr[it], state[q].astype(BF))
            ub = (u0[it] + us[:C]).astype(BF)
            ys[it] = us[C:] + rkv[it] + _dot(a_rb[it], bd4(ub))
            uv = jnp.concatenate([ub, vv_ref[q, rows_of(c), :]], axis=0)
            bk = jnp.concatenate([bh_ref[q, rows_of(c), :], kh_ref[q, rows_of(c), :]], axis=0)
            wrow = (c % chunks_per_front_tile) * N_QUADS + q
            wc = wc_ref[c // chunks_per_front_tile, wrow:wrow + 1, :]
            state[q] = state[q] * wc + jnp.where(diag_blocks, _dot_tn(uv, bk), 0.0)
    for q in range(N_QUADS):
        s_ref[q] = state[q]

    g256 = g256_ref[...]
    y = jnp.concatenate(
        [jnp.concatenate([ys[(c, q)] for c in range(n_chunks)], axis=0) for q in range(N_QUADS)], axis=1)
    inv_n = 1.0 / HEAD_DIM
    d = y - _headsum(y, g256) * inv_n
    var = _headsum(d * d, g256) * inv_n
    yn = d * lax.rsqrt(var + GN_EPS) * gnw_ref[...] + gnb_ref[...]
    ya = _dot((yn * g_ref[...].astype(F32) + bg_ref[...].astype(F32)).astype(BF), wor_ref[...])
    mix_in = ga_ref[...].astype(F32) * ya + zb_ref[...].astype(F32)
    mix = _dot(mix_in.astype(BF), wout_ref[...])
    h = _layer_norm(ALPHA * x_ref[...] + mix, ln1w_ref[...], ln1b_ref[...])
    h_ref[...] = h
    hb = pltpu.bitcast(h.astype(BF).astype(F32), jnp.uint32)
    half = D_MODEL // 2
    hp_ref[...] = (hb[:, :half] >> 16) | (hb[:, half:] & jnp.uint32(0xFFFF0000))


def _scan(quads, wc, g, bg, ga, zb, x2, gnw, gnb, g256, wor, wout, ln1w, ln1b, seq):
    n_tok = x2.shape[0]
    n_tiles = n_tok // SCAN_TM
    row = lambda i: (i, 0)
    quad_spec = pl.BlockSpec((N_QUADS, SCAN_TM, QUAD), lambda i: (0, i, 0))
    full_spec = pl.BlockSpec((SCAN_TM, D_MODEL), row)
    n_wc = wc.shape[1]
    consts = (gnw, gnb, g256, wor, wout, ln1w, ln1b)
    return pl.pallas_call(
        functools.partial(_scan_kernel, tiles_per_seq=seq // SCAN_TM),
        grid=(n_tiles,),
        in_specs=[quad_spec] * 7 + [pl.BlockSpec((SCAN_TM // TM, n_wc, QUAD), lambda i: (i, 0, 0))]
        + [full_spec] * 5 + [_const_spec(c.shape) for c in consts],
        out_specs=[full_spec, pl.BlockSpec((SCAN_TM, D_MODEL // 2), row)],
        out_shape=[jax.ShapeDtypeStruct((n_tok, D_MODEL), F32),
                   jax.ShapeDtypeStruct((n_tok, D_MODEL // 2), jnp.uint32)],
        scratch_shapes=[pltpu.VMEM((N_QUADS, QUAD, QUAD), F32)],
        compiler_params=pltpu.CompilerParams(
            dimension_semantics=("arbitrary",), vmem_limit_bytes=V7X_VMEM_LIMIT_BYTES),
        name="scan",
    )(*quads, wc, g, bg, ga, zb, x2, *consts)


def _first_max(x, idx, n):
    m = jnp.max(x, axis=0, keepdims=True)
    first = jnp.min(jnp.where(x == m, idx, n), axis=0, keepdims=True)
    return m, first


def _route_kernel(h_ref, wrh_ref, wrl_ref, bias_ref, ustrict_ref, e_ref, gate_ref, rank_ref, cnt_ref,
                  carry_ref):
    @pl.when(pl.program_id(0) == 0)
    def _():
        carry_ref[...] = jnp.zeros_like(carry_ref)

    hh, hl = _split2(h_ref[...])
    wrh = wrh_ref[...]
    logits = _dot_nt(wrh, hh) + _dot_nt(wrh, hl) + _dot_nt(wrl_ref[...], hh)
    scores = jax.nn.sigmoid(logits)
    choice = scores + bias_ref[...]
    n_tok = choice.shape[1]
    per = N_EXPERTS // N_EXPERT_GROUPS
    eidx = lax.broadcasted_iota(jnp.int32, (N_EXPERTS, 1), 0)
    gidx = lax.broadcasted_iota(jnp.int32, (N_EXPERT_GROUPS, 1), 0)
    lidx = lax.broadcasted_iota(jnp.int32, (per, 1), 0)
    neg = -jnp.inf

    gsc = []
    for gi in range(N_EXPERT_GROUPS):
        blk = choice[per * gi:per * (gi + 1), :]
        m1, i1 = _first_max(blk, lidx, per)
        m2 = jnp.max(jnp.where(lidx == i1, neg, blk), axis=0, keepdims=True)
        gsc.append(m1 + m2)
    gscore = jnp.concatenate(gsc, axis=0)
    gsel = jnp.zeros(gscore.shape, jnp.bool_)
    for _ in range(TOPK_GROUPS):
        _, gi1 = _first_max(gscore, gidx, N_EXPERT_GROUPS)
        hit = gidx == gi1
        gsel = gsel | hit
        gscore = jnp.where(hit, neg, gscore)
    emask = jnp.concatenate(
        [jnp.broadcast_to(gsel[gi:gi + 1, :], (per, n_tok)) for gi in range(N_EXPERT_GROUPS)], axis=0)
    cand = jnp.where(emask, choice, neg)
    es, gs, hits = [], [], []
    for _ in range(TOP_K):
        _, e1 = _first_max(cand, eidx, N_EXPERTS)
        hit = eidx == e1
        es.append(e1)
        hits.append(hit)
        gs.append(jnp.sum(jnp.where(hit, scores, 0.0), axis=0, keepdims=True))
        cand = jnp.where(hit, neg, cand)
    gates = jnp.concatenate(gs, axis=0)
    gates = gates / jnp.sum(gates, axis=0, keepdims=True) * ROUTED_SCALE
    e_ref[...] = jnp.concatenate(es, axis=0)
    gate_ref[...] = gates

    chosen = functools.reduce(jnp.logical_or, hits)
    sel = jnp.where(chosen, 1.0, 0.0)
    before = _dot(sel.astype(BF), ustrict_ref[...]) + carry_ref[:, 0:1]
    rank_ref[...] = jnp.concatenate(
        [jnp.sum(jnp.where(hit, before, 0.0), axis=0, keepdims=True) for hit in hits],
        axis=0).astype(jnp.int32)
    total = before[:, n_tok - 1:n_tok] + sel[:, n_tok - 1:n_tok]
    carry_ref[...] = jnp.broadcast_to(total, carry_ref.shape)
    cnt_ref[...] = jnp.broadcast_to(total, cnt_ref.shape).astype(jnp.int32)


def _route(h, wrh, wrl, bias_col, ustrict):
    n_tok = h.shape[0]
    col = lambda i: (0, i)
    tok_spec = pl.BlockSpec((TOP_K, ROUTE_TM), col)
    return pl.pallas_call(
        _route_kernel,
        grid=(n_tok // ROUTE_TM,),
        in_specs=[pl.BlockSpec((ROUTE_TM, D_MODEL), lambda i: (i, 0)),
                  _const_spec(wrh.shape), _const_spec(wrl.shape), _const_spec(bias_col.shape),
                  _const_spec(ustrict.shape)],
        out_specs=[tok_spec, tok_spec, tok_spec, pl.BlockSpec((N_EXPERTS, LANES), lambda i: (0, 0))],
        out_shape=[jax.ShapeDtypeStruct((TOP_K, n_tok), jnp.int32),
                   jax.ShapeDtypeStruct((TOP_K, n_tok), F32),
                   jax.ShapeDtypeStruct((TOP_K, n_tok), jnp.int32),
                   jax.ShapeDtypeStruct((N_EXPERTS, LANES), jnp.int32)],
        scratch_shapes=[pltpu.VMEM((N_EXPERTS, LANES), F32)],
        compiler_params=pltpu.CompilerParams(dimension_semantics=("arbitrary",)),
        name="route",
    )(h, wrh, wrl, bias_col, ustrict)


def _rows_kernel(e_ref, rank_ref, base_ref, pos_ref):
    eidx = lax.broadcasted_iota(jnp.int32, (N_EXPERTS, 1), 0)
    base = base_ref[...]
    pos_ref[...] = rank_ref[...] + jnp.concatenate(
        [jnp.sum(jnp.where(eidx == e_ref[kk:kk + 1, :], base, 0), axis=0, keepdims=True)
         for kk in range(TOP_K)], axis=0)


def _rows(top_e, rank, base_col):
    n_tok = top_e.shape[1]
    tok_spec = pl.BlockSpec((TOP_K, ROUTE_TM), lambda i: (0, i))
    return pl.pallas_call(
        _rows_kernel,
        grid=(n_tok // ROUTE_TM,),
        in_specs=[tok_spec, tok_spec, _const_spec(base_col.shape)],
        out_specs=tok_spec,
        out_shape=jax.ShapeDtypeStruct((TOP_K, n_tok), jnp.int32),
        compiler_params=pltpu.CompilerParams(dimension_semantics=("arbitrary",)),
        name="rows",
    )(top_e, rank, base_col)


SC_WINDOW = 128


def _sc_mesh():
    return plsc.VectorSubcoreMesh(core_axis_name="c", subcore_axis_name="s")


def _sc_workers():
    info = plsc.get_sparse_core_info()
    return info.num_cores, info.num_cores * info.num_subcores


def _sc_scatter_rows(src, pos, n_rows):
    n_tok, width = src.shape
    n_k = pos.shape[0]
    n_cores, n_workers = _sc_workers()
    n_win = n_tok // (n_workers * SC_WINDOW)

    @functools.partial(
        pl.kernel, out_type=jax.ShapeDtypeStruct((n_rows, width), src.dtype), mesh=_sc_mesh(),
        scratch_types=[pltpu.VMEM((n_k * n_win, SC_WINDOW), jnp.int32),
                       pltpu.VMEM((SC_WINDOW, width), src.dtype),
                       pltpu.SemaphoreType.DMA])
    def scatter(src_hbm, pos_hbm, out_hbm, idx_v, rows_v, sem):
        wid = lax.axis_index("s") * n_cores + lax.axis_index("c")
        pltpu.sync_copy(pos_hbm.at[wid], idx_v)

        @pl.loop(0, n_win)
        def _(j):
            pltpu.sync_copy(src_hbm.at[pl.ds((wid * n_win + j) * SC_WINDOW, SC_WINDOW)], rows_v)
            copies = [pltpu.async_copy(rows_v, out_hbm.at[idx_v.at[kk * n_win + j]], sem)
                      for kk in range(n_k)]
            for cp in copies:
                cp.wait()

    pos4 = pos.reshape(n_k, n_workers, n_win, SC_WINDOW).transpose(1, 0, 2, 3)
    return scatter(src, pos4.reshape(n_workers, n_k * n_win, SC_WINDOW))


def _sc_gather_rows(src, idx):
    n_idx = idx.shape[0]
    width = src.shape[1]
    n_cores, n_workers = _sc_workers()
    n_win = n_idx // (n_workers * SC_WINDOW)

    @functools.partial(
        pl.kernel, out_type=jax.ShapeDtypeStruct((n_idx, width), src.dtype), mesh=_sc_mesh(),
        scratch_types=[pltpu.VMEM((n_win, SC_WINDOW), jnp.int32),
                       pltpu.VMEM((SC_WINDOW, width), src.dtype),
                       pltpu.SemaphoreType.DMA])
    def gather(src_hbm, idx_hbm, out_hbm, idx_v, rows_v, sem):
        wid = lax.axis_index("s") * n_cores + lax.axis_index("c")
        pltpu.sync_copy(idx_hbm.at[wid], idx_v)

        @pl.loop(0, n_win)
        def _(j):
            pltpu.async_copy(src_hbm.at[idx_v.at[j]], rows_v, sem).wait()
            pltpu.sync_copy(rows_v, out_hbm.at[pl.ds((wid * n_win + j) * SC_WINDOW, SC_WINDOW)])

    return gather(src, idx.reshape(n_workers, n_win, SC_WINDOW))


def _unpack_pair(words):
    lo = pltpu.bitcast(words << 16, F32).astype(BF)
    hi = pltpu.bitcast(words & jnp.uint32(0xFFFF0000), F32).astype(BF)
    return lo, hi


def _pack_pair(val):
    bits = pltpu.bitcast(val.astype(BF).astype(F32), jnp.uint32)
    half = val.shape[1] // 2
    return (bits[:, :half] >> 16) | (bits[:, half:] & jnp.uint32(0xFFFF0000))


def _ffn_kernel(be_ref, nv_ref, x_ref, wgu_ref, wd_ref, o_ref, wgu16_ref, wd16_ref):
    i = pl.program_id(0)
    changed = jnp.logical_or(i == 0, be_ref[i] != be_ref[jnp.maximum(i - 1, 0)])

    @pl.when(changed)
    def _():
        wgu16_ref[...] = wgu_ref[0].astype(BF)
        wd16_ref[...] = wd_ref[0].astype(BF)

    n_valid = nv_ref[i]

    @pl.when(n_valid > 0)
    def _():
        half = D_MODEL // 2
        row = lax.broadcasted_iota(jnp.int32, (FFN_ROWS, 1), 0)
        words = jnp.where(row < n_valid, x_ref[...], jnp.uint32(0))
        lo, hi = _unpack_pair(words)
        gu = _dot(lo, wgu16_ref[0:half, :]) + _dot(hi, wgu16_ref[half:, :])
        act = (jax.nn.silu(gu[:, :EXPERT_FF]) * gu[:, EXPERT_FF:]).astype(BF)
        o_ref[...] = _pack_pair(_dot(act, wd16_ref[...]))

    @pl.when(n_valid <= 0)
    def _():
        o_ref[...] = jnp.zeros_like(o_ref)


def _ffn(block_expert, n_valid, xs, wgu, wd):
    n_rows = xs.shape[0]
    n_blk = n_rows // FFN_ROWS
    half = D_MODEL // 2
    grid_spec = pltpu.PrefetchScalarGridSpec(
        num_scalar_prefetch=2,
        grid=(n_blk,),
        in_specs=[pl.BlockSpec((FFN_ROWS, half), lambda i, be, nv: (i, 0)),
                  pl.BlockSpec((1, D_MODEL, 2 * EXPERT_FF), lambda i, be, nv: (be[i], 0, 0)),
                  pl.BlockSpec((1, EXPERT_FF, D_MODEL), lambda i, be, nv: (be[i], 0, 0))],
        out_specs=pl.BlockSpec((FFN_ROWS, half), lambda i, be, nv: (i, 0)),
        scratch_shapes=[pltpu.VMEM((D_MODEL, 2 * EXPERT_FF), BF), pltpu.VMEM((EXPERT_FF, D_MODEL), BF)],
    )
    return pl.pallas_call(
        _ffn_kernel,
        grid_spec=grid_spec,
        out_shape=jax.ShapeDtypeStruct((n_rows, half), jnp.uint32),
        compiler_params=pltpu.CompilerParams(dimension_semantics=("arbitrary",)),
        name="ffn",
    )(block_expert, n_valid, xs, wgu, wd)


def _final_kernel(rows_ref, gate_ref, h_ref, wsgu_ref, wsd_ref, ln2w_ref, ln2b_ref, o_ref):
    h = h_ref[...]
    hb = h.astype(BF)
    gu = _dot(hb, wsgu_ref[...])
    act = (jax.nn.silu(gu[:, :SHARED_FF]) * gu[:, SHARED_FF:]).astype(BF)
    shared = _dot(act, wsd_ref[...])

    half = D_MODEL // 2
    lo_acc = jnp.zeros((FINAL_TM, half), F32)
    hi_acc = jnp.zeros((FINAL_TM, half), F32)
    gates = gate_ref[...]
    for kk in range(TOP_K):
        words = rows_ref[kk]
        gk = gates[:, kk:kk + 1]
        lo_acc = lo_acc + gk * pltpu.bitcast(words << 16, F32)
        hi_acc = hi_acc + gk * pltpu.bitcast(words & jnp.uint32(0xFFFF0000), F32)
    routed = jnp.concatenate([lo_acc, hi_acc], axis=1)
    o_ref[...] = _layer_norm(ALPHA * h + routed + shared, ln2w_ref[...], ln2b_ref[...])


def _final(rows, gates_t, h, wsgu, wsd, ln2w, ln2b):
    n_tok = h.shape[0]
    n_tiles = n_tok // FINAL_TM
    half = D_MODEL // 2
    row = lambda i: (i, 0)
    consts = (wsgu, wsd, ln2w, ln2b)
    return pl.pallas_call(
        _final_kernel,
        grid=(n_tiles,),
        in_specs=[pl.BlockSpec((TOP_K, FINAL_TM, half), lambda i: (0, i, 0)),
                  pl.BlockSpec((FINAL_TM, TOP_K), row),
                  pl.BlockSpec((FINAL_TM, D_MODEL), row)] + [_const_spec(c.shape) for c in consts],
        out_specs=pl.BlockSpec((FINAL_TM, D_MODEL), row),
        out_shape=jax.ShapeDtypeStruct((n_tok, D_MODEL), F32),
        compiler_params=pltpu.CompilerParams(dimension_semantics=("arbitrary",)),
        name="final",
    )(rows, gates_t, h, *consts)


def _np_consts():
    idx = np.arange(QUAD)
    g256 = (idx[:, None] // HEAD_DIM == idx[None, :] // HEAD_DIM).astype(np.float32)
    t = np.arange(TM)
    same = t[:, None] // SCAN_CHUNK == t[None, :] // SCAN_CHUNK
    ltri = (same & (t[None, :] <= t[:, None])).astype(np.float32)
    lall = same.astype(np.float32)
    return jnp.asarray(g256, BF), jnp.asarray(ltri, BF), jnp.asarray(lall, BF)


def _block_plan(counts, n_assign):
    padded = (counts + FFN_ROWS - 1) // FFN_ROWS * FFN_ROWS
    pad_end = jnp.cumsum(padded)
    base = pad_end - padded
    n_blocks = (n_assign + N_EXPERTS * (FFN_ROWS - 1) + FFN_ROWS - 1) // FFN_ROWS
    blk_row = jnp.arange(n_blocks, dtype=jnp.int32) * FFN_ROWS
    block_expert = jnp.minimum(jnp.sum(blk_row[:, None] >= pad_end[None, :], axis=1),
                               N_EXPERTS - 1).astype(jnp.int32)
    experts = jnp.arange(N_EXPERTS, dtype=jnp.int32)
    last = jnp.sum(jnp.where(block_expert[:, None] == experts[None, :], (base + counts)[None, :], 0), axis=1)
    n_valid = jnp.clip(last - blk_row, 0, FFN_ROWS).astype(jnp.int32)
    return base.astype(jnp.int32), block_expert, n_valid, n_blocks * FFN_ROWS


def _layer(x2, seq, w_in, shift_mu, w0, decay_lora, a0, iclr_lora, gate_lora, k_k, k_a, r_k, gn_w, gn_b,
           lnw, lnb, spatial_w, spatial_b, w_o_rwkv, w_o_gmlp, w_out, ln1_w, ln1_b,
           w_router, router_bias, w_exp_gate_up, w_exp_down, w_sh_gate_up, w_sh_down, ln2_w, ln2_b):
    n_tok = x2.shape[0]
    c_rwkv = C_RKV + C_LORA
    pad = C_LORA_PAD - C_LORA
    win = jnp.concatenate([w_in[:, :c_rwkv], jnp.zeros((D_MODEL, pad), F32), w_in[:, c_rwkv:]], axis=1).astype(BF)
    mu = jnp.pad(shift_mu, (0, pad)).reshape(1, C_RWKV_PAD)
    row1 = lambda t: t.reshape(1, -1)
    w1 = jnp.zeros((LANES, 2 * D_MODEL), F32)
    w1 = w1.at[:DECAY_LORA, :D_MODEL].set(decay_lora).at[DECAY_LORA:, D_MODEL:].set(iclr_lora).astype(BF)
    w2 = jnp.zeros((C_LORA_PAD - LANES, D_MODEL), F32).at[:GATE_LORA].set(gate_lora).astype(BF)
    g256, ltri, lall = _np_consts()
    sb_map = jnp.repeat(spatial_b.T, D_MODEL // GMLP_GROUPS, axis=1)

    outs = _front(x2, win, mu, row1(w0), row1(a0), w1, w2, row1(k_k), row1(k_a), row1(r_k),
                  g256, ltri, lall, row1(lnw), row1(lnb), spatial_w, sb_map, w_o_gmlp.astype(BF), seq)
    quads, wc, (g, bg, ga, zb) = outs[:7], outs[7], outs[8:]
    h1, h1p = _scan(quads, wc, g, bg, ga, zb, x2, row1(gn_w), row1(gn_b), g256,
                    w_o_rwkv.astype(BF), w_out.astype(BF), row1(ln1_w), row1(ln1_b), seq)

    wrt = w_router.T
    wrh = wrt.astype(BF)
    wrl = (wrt - wrh.astype(F32)).astype(BF)
    tt = np.arange(ROUTE_TM)
    ustrict = jnp.asarray((tt[:, None] < tt[None, :]).astype(np.float32), BF)
    top_e, gates, rank, counts = _route(h1, wrh, wrl, router_bias.reshape(N_EXPERTS, 1), ustrict)
    base, block_expert, n_valid, n_rows = _block_plan(counts[:, 0], n_tok * TOP_K)
    pos = _rows(top_e, rank, base.reshape(N_EXPERTS, 1))
    xs = _sc_scatter_rows(h1p, pos, n_rows)
    rows = _ffn(block_expert, n_valid, xs, w_exp_gate_up, w_exp_down)
    picked = _sc_gather_rows(rows, pos.reshape(-1)).reshape(TOP_K, n_tok, D_MODEL // 2)
    return _final(picked, gates.T, h1, w_sh_gate_up.astype(BF), w_sh_down.astype(BF),
                  row1(ln2_w), row1(ln2_b))


def kernel(x, w_in, shift_mu, rwkv_w0, rwkv_decay_lora, rwkv_a0, rwkv_iclr_lora, rwkv_gate_lora, rwkv_k_k, rwkv_k_a, rwkv_r_k, rwkv_gn_w, rwkv_gn_b, gmlp_ln_w, gmlp_ln_b, gmlp_spatial_w, gmlp_spatial_b, w_o_rwkv, w_o_gmlp, w_out, ln1_w, ln1_b, w_router, router_bias, w_exp_gate_up, w_exp_down, w_sh_gate_up, w_sh_down, ln2_w, ln2_b):
    bsz, seq, d = x.shape
    h = x.reshape(bsz * seq, d)
    for l in range(DEPTH):
        h = _layer(h, seq, w_in[l], shift_mu[l], rwkv_w0[l], rwkv_decay_lora[l], rwkv_a0[l],
                   rwkv_iclr_lora[l], rwkv_gate_lora[l], rwkv_k_k[l], rwkv_k_a[l], rwkv_r_k[l],
                   rwkv_gn_w[l], rwkv_gn_b[l], gmlp_ln_w[l], gmlp_ln_b[l], gmlp_spatial_w[l],
                   gmlp_spatial_b[l], w_o_rwkv[l], w_o_gmlp[l], w_out[l], ln1_w[l], ln1_b[l],
                   w_router[l], router_bias[l], w_exp_gate_up[l], w_exp_down[l], w_sh_gate_up[l],
                   w_sh_down[l], ln2_w[l], ln2_b[l])
    return h.reshape(bsz, seq, d)
```

```python
import functools

import numpy as np
import jax
import jax.numpy as jnp
from jax import lax
from jax.experimental import pallas as pl
from jax.experimental.pallas import tpu as pltpu
from jax.experimental.pallas import tpu_sc as plsc

F32 = jnp.float32
BF = jnp.bfloat16

D_MODEL = 1024
HEAD_DIM = 64
N_HEADS = 16
DECAY_LORA = 64
ICLR_LORA = 64
GATE_LORA = 160
GN_EPS = 64e-5
GMLP_GROUPS = 8
GMLP_CHUNK = 128
N_EXPERTS = 256
TOP_K = 8
N_EXPERT_GROUPS = 8
TOPK_GROUPS = 4
EXPERT_FF = 256
SHARED_FF = 256
ROUTED_SCALE = 2.5
LN_EPS = 1e-5
DEPTH = 1
ALPHA = (2 * DEPTH) ** 0.25

LANES = 128
V7X_VMEM_LIMIT_BYTES = 56 * 1024 * 1024

C_RKV = 3 * D_MODEL
C_LORA = DECAY_LORA + ICLR_LORA + GATE_LORA
C_LORA_PAD = 384
C_RWKV_PAD = C_RKV + C_LORA_PAD
C_GMLP = 2 * D_MODEL
C_GATES = 2 * D_MODEL
C_IN_PAD = C_RWKV_PAD + C_GMLP + C_GATES

SCAN_CHUNK = 64
QUAD = 4 * HEAD_DIM
N_QUADS = D_MODEL // QUAD
TM = 256
SCAN_TM = 256
ROUTE_TM = 256
FFN_ROWS = 512
FINAL_TM = 256


def _dot(a, b):
    return jnp.dot(a, b, preferred_element_type=F32)


def _dot_nt(a, b):
    return lax.dot_general(a, b, (((1,), (1,)), ((), ())), preferred_element_type=F32)


def _dot_tn(a, b):
    return lax.dot_general(a, b, (((0,), (0,)), ((), ())), preferred_element_type=F32)


def _split2(z):
    hi = z.astype(BF)
    lo = (z - hi.astype(F32)).astype(BF)
    return hi, lo


def _headsum(z, g256, passes=2):
    outs = []
    for j in range(N_QUADS):
        zj = z[:, QUAD * j:QUAD * (j + 1)]
        if passes == 1:
            outs.append(_dot(zj.astype(BF), g256))
        else:
            hi, lo = _split2(zj)
            outs.append(_dot(hi, g256) + _dot(lo, g256))
    return jnp.concatenate(outs, axis=1)


def _layer_norm(v, w, b):
    mu = jnp.mean(v, axis=-1, keepdims=True)
    d = v - mu
    var = jnp.mean(d * d, axis=-1, keepdims=True)
    return d * lax.rsqrt(var + LN_EPS) * w + b


def _const_spec(shape):
    nd = len(shape)
    return pl.BlockSpec(shape, lambda *_: (0,) * nd, pipeline_mode=pl.Buffered(1))


def _front_kernel(x_ref, win_ref, mu_ref, w0_ref, a0_ref, w1_ref, w2_ref, kk_ref, ka_ref, rk_ref,
                  g256_ref, ltri_ref, lnw_ref, lnb_ref, ws_ref, sb_ref, wog_ref,
                  rt_ref, at_ref, bt_ref, kt_ref, vv_ref, bh_ref, kh_ref, wc_ref,
                  g_ref, bg_ref, ga_ref, zb_ref, prev_ref, xb_ref, *, tiles_per_seq):
    i = pl.program_id(0)
    first = (i % tiles_per_seq) == 0
    xb_ref[...] = x_ref[...].astype(BF)
    is_row0 = lax.broadcasted_iota(jnp.int32, (TM, 1), 0) == 0
    W = QUAD

    def proj(lo, width=W):
        return _dot(xb_ref[...], win_ref[:, lo:lo + width])

    def shift_mix(lo, width=W):
        p = proj(lo, width)
        prev = jnp.where(first, 0.0, prev_ref[:, lo:lo + width])
        sh = jnp.where(is_row0, prev, pltpu.roll(p, 1, 0))
        prev_ref[:, lo:lo + width] = p[TM - 1:TM, :]
        return p + mu_ref[:, lo:lo + width] * (sh - p)

    lora = shift_mix(C_RKV, C_LORA_PAD)
    lane = lax.broadcasted_iota(jnp.int32, (1, LANES), 1)
    act1 = jnp.where(lane < DECAY_LORA, jnp.tanh(lora[:, :LANES]), lora[:, :LANES]).astype(BF)
    act2 = jax.nn.sigmoid(lora[:, LANES:]).astype(BF)
    g256 = g256_ref[...]
    for q in range(N_QUADS):
        c0 = W * q
        cols = slice(c0, c0 + W)
        r = shift_mix(c0)
        k = shift_mix(D_MODEL + c0)
        v = shift_mix(2 * D_MODEL + c0)
        w = -jax.nn.softplus(-(w0_ref[:, cols] + _dot(act1, w1_ref[:, cols]))) - 0.5
        logd = -jnp.exp(w)
        a = jax.nn.sigmoid(a0_ref[:, cols] + _dot(act1, w1_ref[:, D_MODEL + c0:D_MODEL + c0 + W]))
        g = _dot(act2, w2_ref[:, cols])

        kkr = k * kk_ref[:, cols]
        kk = kkr * lax.rsqrt(jnp.maximum(_dot((kkr * kkr).astype(BF), g256), 1e-24))
        kmod = k * (1.0 + (a - 1.0) * ka_ref[:, cols])
        bonus = _dot((r * kmod * rk_ref[:, cols]).astype(BF), g256) * v
        kka = kk * a

        cum = sum(_dot(ltri_ref[...], part) for part in _split2(logd))
        e_inv = jnp.exp(-cum)
        bt = kka * e_inv
        kt = kmod * e_inv
        ends = []
        for c in range(TM // SCAN_CHUNK):
            last = SCAN_CHUNK * (c + 1) - 1
            ewc_row = jnp.exp(cum[last:last + 1, :])
            wc_ref[0, c * N_QUADS + q:c * N_QUADS + q + 1, :] = ewc_row
            ends.append(jnp.broadcast_to(ewc_row, (SCAN_CHUNK, W)))
        ewc = jnp.concatenate(ends, axis=0)
        rt_ref[q] = (r * jnp.exp(cum)).astype(BF)
        at_ref[q] = (-kk * jnp.exp(cum - logd)).astype(BF)
        bt_ref[q] = bt.astype(BF)
        kt_ref[q] = kt.astype(BF)
        vv_ref[q] = v.astype(BF)
        bh_ref[q] = (bt * ewc).astype(BF)
        kh_ref[q] = (kt * ewc).astype(BF)
        g_ref[:, cols] = g.astype(BF)
        bg_ref[:, cols] = (bonus * g).astype(BF)

    gelu = lambda t: 0.5 * t * (1.0 + lax.erf(t * np.float32(np.sqrt(0.5))))
    u0 = C_RWKV_PAD
    zv = [gelu(proj(u0 + D_MODEL + W * j)) for j in range(N_QUADS)]
    mean = sum(jnp.sum(z, axis=-1, keepdims=True) for z in zv) * (1.0 / D_MODEL)
    dv = [z - mean for z in zv]
    var = sum(jnp.sum(d * d, axis=-1, keepdims=True) for d in dv) * (1.0 / D_MODEL)
    rstd = lax.rsqrt(var + LN_EPS)
    trow = lax.broadcasted_iota(jnp.int32, (GMLP_CHUNK, GMLP_CHUNK), 0)
    tcol = lax.broadcasted_iota(jnp.int32, (GMLP_CHUNK, GMLP_CHUNK), 1)
    causal = tcol <= trow
    n_time = TM // GMLP_CHUNK
    yb = jnp.zeros((TM, D_MODEL), F32)
    for j in range(N_QUADS):
        cols = slice(W * j, W * (j + 1))
        vn = (dv[j] * rstd * lnw_ref[:, cols] + lnb_ref[:, cols]).astype(BF)
        svs = []
        for gl in range(W // LANES):
            wsg = jnp.where(causal, ws_ref[(W // LANES) * j + gl], 0.0).astype(BF)
            svs.append(jnp.concatenate(
                [_dot(wsg, vn[GMLP_CHUNK * cc:GMLP_CHUNK * (cc + 1), LANES * gl:LANES * (gl + 1)])
                 for cc in range(n_time)], axis=0))
        sb = sb_ref[:, cols]
        sv = jnp.concatenate(svs, axis=1) + jnp.concatenate([sb] * n_time, axis=0)
        u = gelu(proj(u0 + W * j))
        yb = yb + _dot((u * sv).astype(BF), wog_ref[W * j:W * (j + 1), :])

    g0 = C_RWKV_PAD + C_GMLP
    for j in range(N_QUADS):
        cols = slice(W * j, W * (j + 1))
        ga_ref[:, cols] = jax.nn.sigmoid(proj(g0 + W * j)).astype(BF)
        zb_ref[:, cols] = (jax.nn.sigmoid(proj(g0 + D_MODEL + W * j)) * yb[:, cols]).astype(BF)


def _front(x2, win, mu, w0, a0, w1, w2, k_k, k_a, r_k, g256, ltri, lnw, lnb, ws, sb, wog, seq):
    n_tok = x2.shape[0]
    n_tiles = n_tok // TM
    row = lambda i: (i, 0)
    quad_spec = pl.BlockSpec((N_QUADS, TM, QUAD), lambda i: (0, i, 0))
    quad_shape = jax.ShapeDtypeStruct((N_QUADS, n_tok, QUAD), BF)
    full_spec = pl.BlockSpec((TM, D_MODEL), row)
    full_shape = jax.ShapeDtypeStruct((n_tok, D_MODEL), BF)
    n_wc = (TM // SCAN_CHUNK) * N_QUADS
    consts = (win, mu, w0, a0, w1, w2, k_k, k_a, r_k, g256, ltri, lnw, lnb, ws, sb, wog)
    return pl.pallas_call(
        functools.partial(_front_kernel, tiles_per_seq=seq // TM),
        grid=(n_tiles,),
        in_specs=[pl.BlockSpec((TM, D_MODEL), row)] + [_const_spec(c.shape) for c in consts],
        out_specs=[quad_spec] * 7 + [pl.BlockSpec((1, n_wc, QUAD), lambda i: (i, 0, 0))] + [full_spec] * 4,
        out_shape=[quad_shape] * 7 + [jax.ShapeDtypeStruct((n_tiles, n_wc, QUAD), F32)] + [full_shape] * 4,
        scratch_shapes=[pltpu.VMEM((1, C_RWKV_PAD), F32), pltpu.VMEM((TM, D_MODEL), BF)],
        compiler_params=pltpu.CompilerParams(
            dimension_semantics=("arbitrary",), vmem_limit_bytes=V7X_VMEM_LIMIT_BYTES),
        name="front",
    )(x2, *consts)


def _scan_kernel(rt_ref, at_ref, bt_ref, kt_ref, vv_ref, bh_ref, kh_ref, wc_ref,
                 g_ref, bg_ref, ga_ref, zb_ref, x_ref, gnw_ref, gnb_ref, g256_ref,
                 wor_ref, wout_ref, ln1w_ref, ln1b_ref,
                 h_ref, hp_ref, s_ref, *, tiles_per_seq):
    i = pl.program_id(0)

    @pl.when((i % tiles_per_seq) == 0)
    def _():
        s_ref[...] = jnp.zeros_like(s_ref)

    lane = lax.broadcasted_iota(jnp.int32, (1, QUAD), 1)
    lane_head = lane // HEAD_DIM
    lane_pos = lane % HEAD_DIM
    trow = lax.broadcasted_iota(jnp.int32, (SCAN_CHUNK, 1), 0)
    strict = lane_pos < trow
    incl = lane_pos <= trow
    row_head = lax.broadcasted_iota(jnp.int32, (QUAD, 1), 0) // HEAD_DIM
    diag_blocks = row_head == lane_head

    def bd4(m):
        zero = jnp.zeros_like(m)
        return jnp.concatenate([jnp.where(lane_head == j, m, zero) for j in range(4)], axis=0)

    n_chunks = SCAN_TM // SCAN_CHUNK
    chunks_per_front_tile = TM // SCAN_CHUNK
    items = [(c, q) for c in range(n_chunks) for q in range(N_QUADS)]
    rows_of = lambda c: slice(SCAN_CHUNK * c, SCAN_CHUNK * (c + 1))
    C = SCAN_CHUNK

    at, rt, bdv, a_ab, a_akrk, a_rb = {}, {}, {}, {}, {}, {}
    for it in items:
        c, q = it
        at[it] = at_ref[q, rows_of(c), :]
        rt[it] = rt_ref[q, rows_of(c), :]
        ra = jnp.concatenate([at[it], rt[it]], axis=0)
        gb = _dot_nt(ra, bd4(bt_ref[q, rows_of(c), :]))
        gk = _dot_nt(ra, bd4(kt_ref[q, rows_of(c), :]))
        a_ab[it] = jnp.where(strict, gb[:C], 0.0)
        a_rb[it] = jnp.where(incl, gb[C:], 0.0).astype(BF)
        a_akrk[it] = jnp.concatenate(
            [jnp.where(strict, gk[:C], 0.0), jnp.where(incl, gk[C:], 0.0)], axis=0).astype(BF)
        bdv[it] = bd4(vv_ref[q, rows_of(c), :])

    tm, qpow = {}, {}
    for it in items:
        qb = a_ab[it].astype(BF)
        tm[it] = a_ab[it]
        qpow[it] = _dot(qb, bd4(qb))
    for level in range(1, 6):
        for it in items:
            qb = qpow[it].astype(BF)
            bdq = bd4(qb)
            if level < 5:
                both = _dot(jnp.concatenate([tm[it].astype(BF), qb], axis=0), bdq)
                tm[it] = tm[it] + qpow[it] + both[:C]
                qpow[it] = both[C:]
            else:
                tm[it] = tm[it] + qpow[it] + _dot(tm[it].astype(BF), bdq)

    awr, u0, rkv = {}, {}, {}
    for it in items:
        tb = tm[it].astype(BF)
        aw = at[it].astype(F32) + _dot(tb, bd4(at[it]))
        awr[it] = jnp.concatenate([aw.astype(BF), rt[it]], axis=0)
        kv = _dot(a_akrk[it], bdv[it])
        u0[it] = kv[:C] + _dot(tb, bd4(kv[:C].astype(BF)))
        rkv[it] = kv[C:]

    state = [s_ref[q] for q in range(N_QUADS)]
    ys = {}
    for c in range(n_chunks):
        for q in range(N_QUADS):
            it = (c, q)
            us = _dot_nt(awr[it], state[q].astype(BF))
            ub = (u0[it] + us[:C]).astype(BF)
            ys[it] = us[C:] + rkv[it] + _dot(a_rb[it], bd4(ub))
            uv = jnp.concatenate([ub, vv_ref[q, rows_of(c), :]], axis=0)
            bk = jnp.concatenate([bh_ref[q, rows_of(c), :], kh_ref[q, rows_of(c), :]], axis=0)
            wrow = (c % chunks_per_front_tile) * N_QUADS + q
            wc = wc_ref[c // chunks_per_front_tile, wrow:wrow + 1, :]
            state[q] = state[q] * wc + jnp.where(diag_blocks, _dot_tn(uv, bk), 0.0)
    for q in range(N_QUADS):
        s_ref[q] = state[q]

    g256 = g256_ref[...]
    y = jnp.concatenate(
        [jnp.concatenate([ys[(c, q)] for c in range(n_chunks)], axis=0) for q in range(N_QUADS)], axis=1)
    inv_n = 1.0 / HEAD_DIM
    d = y - _headsum(y, g256) * inv_n
    var = _headsum(d * d, g256) * inv_n
    yn = d * lax.rsqrt(var + GN_EPS) * gnw_ref[...] + gnb_ref[...]
    ya = _dot((yn * g_ref[...].astype(F32) + bg_ref[...].astype(F32)).astype(BF), wor_ref[...])
    mix_in = ga_ref[...].astype(F32) * ya + zb_ref[...].astype(F32)
    mix = _dot(mix_in.astype(BF), wout_ref[...])
    h = _layer_norm(ALPHA * x_ref[...] + mix, ln1w_ref[...], ln1b_ref[...])
    h_ref[...] = h
    hb = pltpu.bitcast(h.astype(BF).astype(F32), jnp.uint32)
    half = D_MODEL // 2
    hp_ref[...] = (hb[:, :half] >> 16) | (hb[:, half:] & jnp.uint32(0xFFFF0000))


def _scan(quads, wc, g, bg, ga, zb, x2, gnw, gnb, g256, wor, wout, ln1w, ln1b, seq):
    n_tok = x2.shape[0]
    n_tiles = n_tok // SCAN_TM
    row = lambda i: (i, 0)
    quad_spec = pl.BlockSpec((N_QUADS, SCAN_TM, QUAD), lambda i: (0, i, 0))
    full_spec = pl.BlockSpec((SCAN_TM, D_MODEL), row)
    n_wc = wc.shape[1]
    consts = (gnw, gnb, g256, wor, wout, ln1w, ln1b)
    return pl.pallas_call(
        functools.partial(_scan_kernel, tiles_per_seq=seq // SCAN_TM),
        grid=(n_tiles,),
        in_specs=[quad_spec] * 7 + [pl.BlockSpec((SCAN_TM // TM, n_wc, QUAD), lambda i: (i, 0, 0))]
        + [full_spec] * 5 + [_const_spec(c.shape) for c in consts],
        out_specs=[full_spec, pl.BlockSpec((SCAN_TM, D_MODEL // 2), row)],
        out_shape=[jax.ShapeDtypeStruct((n_tok, D_MODEL), F32),
                   jax.ShapeDtypeStruct((n_tok, D_MODEL // 2), jnp.uint32)],
        scratch_shapes=[pltpu.VMEM((N_QUADS, QUAD, QUAD), F32)],
        compiler_params=pltpu.CompilerParams(
            dimension_semantics=("arbitrary",), vmem_limit_bytes=V7X_VMEM_LIMIT_BYTES),
        name="scan",
    )(*quads, wc, g, bg, ga, zb, x2, *consts)


def _first_max(x, idx, n):
    m = jnp.max(x, axis=0, keepdims=True)
    first = jnp.min(jnp.where(x == m, idx, n), axis=0, keepdims=True)
    return m, first


def _route_kernel(h_ref, wrh_ref, wrl_ref, bias_ref, ustrict_ref, e_ref, gate_ref, rank_ref, cnt_ref,
                  carry_ref):
    @pl.when(pl.program_id(0) == 0)
    def _():
        carry_ref[...] = jnp.zeros_like(carry_ref)

    hh, hl = _split2(h_ref[...])
    wrh = wrh_ref[...]
    logits = _dot_nt(wrh, hh) + _dot_nt(wrh, hl) + _dot_nt(wrl_ref[...], hh)
    scores = jax.nn.sigmoid(logits)
    choice = scores + bias_ref[...]
    n_tok = choice.shape[1]
    per = N_EXPERTS // N_EXPERT_GROUPS
    eidx = lax.broadcasted_iota(jnp.int32, (N_EXPERTS, 1), 0)
    gidx = lax.broadcasted_iota(jnp.int32, (N_EXPERT_GROUPS, 1), 0)
    lidx = lax.broadcasted_iota(jnp.int32, (per, 1), 0)
    neg = -jnp.inf

    gsc = []
    for gi in range(N_EXPERT_GROUPS):
        blk = choice[per * gi:per * (gi + 1), :]
        m1, i1 = _first_max(blk, lidx, per)
        m2 = jnp.max(jnp.where(lidx == i1, neg, blk), axis=0, keepdims=True)
        gsc.append(m1 + m2)
    gscore = jnp.concatenate(gsc, axis=0)
    gsel = jnp.zeros(gscore.shape, jnp.bool_)
    for _ in range(TOPK_GROUPS):
        _, gi1 = _first_max(gscore, gidx, N_EXPERT_GROUPS)
        hit = gidx == gi1
        gsel = gsel | hit
        gscore = jnp.where(hit, neg, gscore)
    emask = jnp.concatenate(
        [jnp.broadcast_to(gsel[gi:gi + 1, :], (per, n_tok)) for gi in range(N_EXPERT_GROUPS)], axis=0)
    cand = jnp.where(emask, choice, neg)
    es, gs, hits = [], [], []
    for _ in range(TOP_K):
        _, e1 = _first_max(cand, eidx, N_EXPERTS)
        hit = eidx == e1
        es.append(e1)
        hits.append(hit)
        gs.append(jnp.sum(jnp.where(hit, scores, 0.0), axis=0, keepdims=True))
        cand = jnp.where(hit, neg, cand)
    gates = jnp.concatenate(gs, axis=0)
    gates = gates / jnp.sum(gates, axis=0, keepdims=True) * ROUTED_SCALE
    e_ref[...] = jnp.concatenate(es, axis=0)
    gate_ref[...] = gates

    chosen = functools.reduce(jnp.logical_or, hits)
    sel = jnp.where(chosen, 1.0, 0.0)
    before = _dot(sel.astype(BF), ustrict_ref[...]) + carry_ref[:, 0:1]
    rank_ref[...] = jnp.concatenate(
        [jnp.sum(jnp.where(hit, before, 0.0), axis=0, keepdims=True) for hit in hits],
        axis=0).astype(jnp.int32)
    total = before[:, n_tok - 1:n_tok] + sel[:, n_tok - 1:n_tok]
    carry_ref[...] = jnp.broadcast_to(total, carry_ref.shape)
    cnt_ref[...] = jnp.broadcast_to(total, cnt_ref.shape).astype(jnp.int32)


def _route(h, wrh, wrl, bias_col, ustrict):
    n_tok = h.shape[0]
    col = lambda i: (0, i)
    tok_spec = pl.BlockSpec((TOP_K, ROUTE_TM), col)
    return pl.pallas_call(
        _route_kernel,
        grid=(n_tok // ROUTE_TM,),
        in_specs=[pl.BlockSpec((ROUTE_TM, D_MODEL), lambda i: (i, 0)),
                  _const_spec(wrh.shape), _const_spec(wrl.shape), _const_spec(bias_col.shape),
                  _const_spec(ustrict.shape)],
        out_specs=[tok_spec, tok_spec, tok_spec, pl.BlockSpec((N_EXPERTS, LANES), lambda i: (0, 0))],
        out_shape=[jax.ShapeDtypeStruct((TOP_K, n_tok), jnp.int32),
                   jax.ShapeDtypeStruct((TOP_K, n_tok), F32),
                   jax.ShapeDtypeStruct((TOP_K, n_tok), jnp.int32),
                   jax.ShapeDtypeStruct((N_EXPERTS, LANES), jnp.int32)],
        scratch_shapes=[pltpu.VMEM((N_EXPERTS, LANES), F32)],
        compiler_params=pltpu.CompilerParams(dimension_semantics=("arbitrary",)),
        name="route",
    )(h, wrh, wrl, bias_col, ustrict)


def _rows_kernel(e_ref, rank_ref, base_ref, pos_ref):
    eidx = lax.broadcasted_iota(jnp.int32, (N_EXPERTS, 1), 0)
    base = base_ref[...]
    pos_ref[...] = rank_ref[...] + jnp.concatenate(
        [jnp.sum(jnp.where(eidx == e_ref[kk:kk + 1, :], base, 0), axis=0, keepdims=True)
         for kk in range(TOP_K)], axis=0)


def _rows(top_e, rank, base_col):
    n_tok = top_e.shape[1]
    tok_spec = pl.BlockSpec((TOP_K, ROUTE_TM), lambda i: (0, i))
    return pl.pallas_call(
        _rows_kernel,
        grid=(n_tok // ROUTE_TM,),
        in_specs=[tok_spec, tok_spec, _const_spec(base_col.shape)],
        out_specs=tok_spec,
        out_shape=jax.ShapeDtypeStruct((TOP_K, n_tok), jnp.int32),
        compiler_params=pltpu.CompilerParams(dimension_semantics=("arbitrary",)),
        name="rows",
    )(top_e, rank, base_col)


SC_WINDOW = 128


def _sc_mesh():
    return plsc.VectorSubcoreMesh(core_axis_name="c", subcore_axis_name="s")


def _sc_workers():
    info = plsc.get_sparse_core_info()
    return info.num_cores, info.num_cores * info.num_subcores


def _sc_scatter_rows(src, pos, n_rows):
    n_tok, width = src.shape
    n_k = pos.shape[0]
    n_cores, n_workers = _sc_workers()
    n_win = n_tok // (n_workers * SC_WINDOW)

    @functools.partial(
        pl.kernel, out_type=jax.ShapeDtypeStruct((n_rows, width), src.dtype), mesh=_sc_mesh(),
        scratch_types=[pltpu.VMEM((n_k * n_win, SC_WINDOW), jnp.int32),
                       pltpu.VMEM((SC_WINDOW, width), src.dtype),
                       pltpu.SemaphoreType.DMA])
    def scatter(src_hbm, pos_hbm, out_hbm, idx_v, rows_v, sem):
        wid = lax.axis_index("s") * n_cores + lax.axis_index("c")
        pltpu.sync_copy(pos_hbm.at[wid], idx_v)

        @pl.loop(0, n_win)
        def _(j):
            pltpu.sync_copy(src_hbm.at[pl.ds((wid * n_win + j) * SC_WINDOW, SC_WINDOW)], rows_v)
            copies = [pltpu.async_copy(rows_v, out_hbm.at[idx_v.at[kk * n_win + j]], sem)
                      for kk in range(n_k)]
            for cp in copies:
                cp.wait()

    pos4 = pos.reshape(n_k, n_workers, n_win, SC_WINDOW).transpose(1, 0, 2, 3)
    return scatter(src, pos4.reshape(n_workers, n_k * n_win, SC_WINDOW))


def _sc_gather_rows(src, idx):
    n_idx = idx.shape[0]
    width = src.shape[1]
    n_cores, n_workers = _sc_workers()
    n_win = n_idx // (n_workers * SC_WINDOW)

    @functools.partial(
        pl.kernel, out_type=jax.ShapeDtypeStruct((n_idx, width), src.dtype), mesh=_sc_mesh(),
        scratch_types=[pltpu.VMEM((n_win, SC_WINDOW), jnp.int32),
                       pltpu.VMEM((SC_WINDOW, width), src.dtype),
                       pltpu.SemaphoreType.DMA])
    def gather(src_hbm, idx_hbm, out_hbm, idx_v, rows_v, sem):
        wid = lax.axis_index("s") * n_cores + lax.axis_index("c")
        pltpu.sync_copy(idx_hbm.at[wid], idx_v)

        @pl.loop(0, n_win)
        def _(j):
            pltpu.async_copy(src_hbm.at[idx_v.at[j]], rows_v, sem).wait()
            pltpu.sync_copy(rows_v, out_hbm.at[pl.ds((wid * n_win + j) * SC_WINDOW, SC_WINDOW)])

    return gather(src, idx.reshape(n_workers, n_win, SC_WINDOW))


def _unpack_pair(words):
    lo = pltpu.bitcast(words << 16, F32).astype(BF)
    hi = pltpu.bitcast(words & jnp.uint32(0xFFFF0000), F32).astype(BF)
    return lo, hi


def _pack_pair(val):
    bits = pltpu.bitcast(val.astype(BF).astype(F32), jnp.uint32)
    half = val.shape[1] // 2
    return (bits[:, :half] >> 16) | (bits[:, half:] & jnp.uint32(0xFFFF0000))


def _ffn_kernel(be_ref, nv_ref, x_ref, wgu_ref, wd_ref, o_ref, wgu16_ref, wd16_ref):
    i = pl.program_id(0)
    changed = jnp.logical_or(i == 0, be_ref[i] != be_ref[jnp.maximum(i - 1, 0)])

    @pl.when(changed)
    def _():
        wgu16_ref[...] = wgu_ref[0].astype(BF)
        wd16_ref[...] = wd_ref[0].astype(BF)

    n_valid = nv_ref[i]

    @pl.when(n_valid > 0)
    def _():
        half = D_MODEL // 2
        row = lax.broadcasted_iota(jnp.int32, (FFN_ROWS, 1), 0)
        words = jnp.where(row < n_valid, x_ref[...], jnp.uint32(0))
        lo, hi = _unpack_pair(words)
        gu = _dot(lo, wgu16_ref[0:half, :]) + _dot(hi, wgu16_ref[half:, :])
        act = (jax.nn.silu(gu[:, :EXPERT_FF]) * gu[:, EXPERT_FF:]).astype(BF)
        o_ref[...] = _pack_pair(_dot(act, wd16_ref[...]))

    @pl.when(n_valid <= 0)
    def _():
        o_ref[...] = jnp.zeros_like(o_ref)


def _ffn(block_expert, n_valid, xs, wgu, wd):
    n_rows = xs.shape[0]
    n_blk = n_rows // FFN_ROWS
    half = D_MODEL // 2
    grid_spec = pltpu.PrefetchScalarGridSpec(
        num_scalar_prefetch=2,
        grid=(n_blk,),
        in_specs=[pl.BlockSpec((FFN_ROWS, half), lambda i, be, nv: (jnp.where(nv[i] > 0, i, 0), 0)),
                  pl.BlockSpec((1, D_MODEL, 2 * EXPERT_FF), lambda i, be, nv: (be[i], 0, 0)),
                  pl.BlockSpec((1, EXPERT_FF, D_MODEL), lambda i, be, nv: (be[i], 0, 0))],
        out_specs=pl.BlockSpec((FFN_ROWS, half), lambda i, be, nv: (i, 0)),
        scratch_shapes=[pltpu.VMEM((D_MODEL, 2 * EXPERT_FF), BF), pltpu.VMEM((EXPERT_FF, D_MODEL), BF)],
    )
    return pl.pallas_call(
        _ffn_kernel,
        grid_spec=grid_spec,
        out_shape=jax.ShapeDtypeStruct((n_rows, half), jnp.uint32),
        compiler_params=pltpu.CompilerParams(dimension_semantics=("arbitrary",)),
        name="ffn",
    )(block_expert, n_valid, xs, wgu, wd)


def _final_kernel(rows_ref, gate_ref, h_ref, wsgu_ref, wsd_ref, ln2w_ref, ln2b_ref, o_ref):
    h = h_ref[...]
    hb = h.astype(BF)
    gu = _dot(hb, wsgu_ref[...])
    act = (jax.nn.silu(gu[:, :SHARED_FF]) * gu[:, SHARED_FF:]).astype(BF)
    shared = _dot(act, wsd_ref[...])

    half = D_MODEL // 2
    lo_acc = jnp.zeros((FINAL_TM, half), F32)
    hi_acc = jnp.zeros((FINAL_TM, half), F32)
    gates = gate_ref[...]
    for kk in range(TOP_K):
        words = rows_ref[kk]
        gk = gates[:, kk:kk + 1]
        lo_acc = lo_acc + gk * pltpu.bitcast(words << 16, F32)
        hi_acc = hi_acc + gk * pltpu.bitcast(words & jnp.uint32(0xFFFF0000), F32)
    routed = jnp.concatenate([lo_acc, hi_acc], axis=1)
    o_ref[...] = _layer_norm(ALPHA * h + routed + shared, ln2w_ref[...], ln2b_ref[...])


def _final(rows, gates_t, h, wsgu, wsd, ln2w, ln2b):
    n_tok = h.shape[0]
    n_tiles = n_tok // FINAL_TM
    half = D_MODEL // 2
    row = lambda i: (i, 0)
    consts = (wsgu, wsd, ln2w, ln2b)
    return pl.pallas_call(
        _final_kernel,
        grid=(n_tiles,),
        in_specs=[pl.BlockSpec((TOP_K, FINAL_TM, half), lambda i: (0, i, 0)),
                  pl.BlockSpec((FINAL_TM, TOP_K), row),
                  pl.BlockSpec((FINAL_TM, D_MODEL), row)] + [_const_spec(c.shape) for c in consts],
        out_specs=pl.BlockSpec((FINAL_TM, D_MODEL), row),
        out_shape=jax.ShapeDtypeStruct((n_tok, D_MODEL), F32),
        compiler_params=pltpu.CompilerParams(dimension_semantics=("arbitrary",)),
        name="final",
    )(rows, gates_t, h, *consts)


def _np_consts():
    idx = np.arange(QUAD)
    g256 = (idx[:, None] // HEAD_DIM == idx[None, :] // HEAD_DIM).astype(np.float32)
    t = np.arange(TM)
    same = t[:, None] // SCAN_CHUNK == t[None, :] // SCAN_CHUNK
    ltri = (same & (t[None, :] <= t[:, None])).astype(np.float32)
    return jnp.asarray(g256, BF), jnp.asarray(ltri, BF)


def _block_plan(counts, n_assign):
    padded = (counts + FFN_ROWS - 1) // FFN_ROWS * FFN_ROWS
    pad_end = jnp.cumsum(padded)
    base = pad_end - padded
    n_blocks = (n_assign + N_EXPERTS * (FFN_ROWS - 1) + FFN_ROWS - 1) // FFN_ROWS
    blk_row = jnp.arange(n_blocks, dtype=jnp.int32) * FFN_ROWS
    block_expert = jnp.minimum(jnp.sum(blk_row[:, None] >= pad_end[None, :], axis=1),
                               N_EXPERTS - 1).astype(jnp.int32)
    experts = jnp.arange(N_EXPERTS, dtype=jnp.int32)
    last = jnp.sum(jnp.where(block_expert[:, None] == experts[None, :], (base + counts)[None, :], 0), axis=1)
    n_valid = jnp.clip(last - blk_row, 0, FFN_ROWS).astype(jnp.int32)
    return base.astype(jnp.int32), block_expert, n_valid, n_blocks * FFN_ROWS


def _layer(x2, seq, w_in, shift_mu, w0, decay_lora, a0, iclr_lora, gate_lora, k_k, k_a, r_k, gn_w, gn_b,
           lnw, lnb, spatial_w, spatial_b, w_o_rwkv, w_o_gmlp, w_out, ln1_w, ln1_b,
           w_router, router_bias, w_exp_gate_up, w_exp_down, w_sh_gate_up, w_sh_down, ln2_w, ln2_b):
    n_tok = x2.shape[0]
    c_rwkv = C_RKV + C_LORA
    pad = C_LORA_PAD - C_LORA
    win = jnp.concatenate([w_in[:, :c_rwkv], jnp.zeros((D_MODEL, pad), F32), w_in[:, c_rwkv:]], axis=1).astype(BF)
    mu = jnp.pad(shift_mu, (0, pad)).reshape(1, C_RWKV_PAD)
    row1 = lambda t: t.reshape(1, -1)
    w1 = jnp.zeros((LANES, 2 * D_MODEL), F32)
    w1 = w1.at[:DECAY_LORA, :D_MODEL].set(decay_lora).at[DECAY_LORA:, D_MODEL:].set(iclr_lora).astype(BF)
    w2 = jnp.zeros((C_LORA_PAD - LANES, D_MODEL), F32).at[:GATE_LORA].set(gate_lora).astype(BF)
    g256, ltri = _np_consts()
    sb_map = jnp.repeat(spatial_b.T, D_MODEL // GMLP_GROUPS, axis=1)

    outs = _front(x2, win, mu, row1(w0), row1(a0), w1, w2, row1(k_k), row1(k_a), row1(r_k),
                  g256, ltri, row1(lnw), row1(lnb), spatial_w, sb_map, w_o_gmlp.astype(BF), seq)
    quads, wc, (g, bg, ga, zb) = outs[:7], outs[7], outs[8:]
    h1, h1p = _scan(quads, wc, g, bg, ga, zb, x2, row1(gn_w), row1(gn_b), g256,
                    w_o_rwkv.astype(BF), w_out.astype(BF), row1(ln1_w), row1(ln1_b), seq)

    wrt = w_router.T
    wrh = wrt.astype(BF)
    wrl = (wrt - wrh.astype(F32)).astype(BF)
    tt = np.arange(ROUTE_TM)
    ustrict = jnp.asarray((tt[:, None] < tt[None, :]).astype(np.float32), BF)
    top_e, gates, rank, counts = _route(h1, wrh, wrl, router_bias.reshape(N_EXPERTS, 1), ustrict)
    base, block_expert, n_valid, n_rows = _block_plan(counts[:, 0], n_tok * TOP_K)
    pos = _rows(top_e, rank, base.reshape(N_EXPERTS, 1))
    xs = _sc_scatter_rows(h1p, pos, n_rows)
    rows = _ffn(block_expert, n_valid, xs, w_exp_gate_up, w_exp_down)
    picked = _sc_gather_rows(rows, pos.reshape(-1)).reshape(TOP_K, n_tok, D_MODEL // 2)
    return _final(picked, gates.T, h1, w_sh_gate_up.astype(BF), w_sh_down.astype(BF),
                  row1(ln2_w), row1(ln2_b))


def kernel(x, w_in, shift_mu, rwkv_w0, rwkv_decay_lora, rwkv_a0, rwkv_iclr_lora, rwkv_gate_lora, rwkv_k_k, rwkv_k_a, rwkv_r_k, rwkv_gn_w, rwkv_gn_b, gmlp_ln_w, gmlp_ln_b, gmlp_spatial_w, gmlp_spatial_b, w_o_rwkv, w_o_gmlp, w_out, ln1_w, ln1_b, w_router, router_bias, w_exp_gate_up, w_exp_down, w_sh_gate_up, w_sh_down, ln2_w, ln2_b):
    bsz, seq, d = x.shape
    h = x.reshape(bsz * seq, d)
    for l in range(DEPTH):
        h = _layer(h, seq, w_in[l], shift_mu[l], rwkv_w0[l], rwkv_decay_lora[l], rwkv_a0[l],
                   rwkv_iclr_lora[l], rwkv_gate_lora[l], rwkv_k_k[l], rwkv_k_a[l], rwkv_r_k[l],
                   rwkv_gn_w[l], rwkv_gn_b[l], gmlp_ln_w[l], gmlp_ln_b[l], gmlp_spatial_w[l],
                   gmlp_spatial_b[l], w_o_rwkv[l], w_o_gmlp[l], w_out[l], ln1_w[l], ln1_b[l],
                   w_router[l], router_bias[l], w_exp_gate_up[l], w_exp_down[l], w_sh_gate_up[l],
                   w_sh_down[l], ln2_w[l], ln2_b[l])
    return h.reshape(bsz, seq, d)
```

```python
import functools

import numpy as np
import jax
import jax.numpy as jnp
from jax import lax
from jax.experimental import pallas as pl
from jax.experimental.pallas import tpu as pltpu
from jax.experimental.pallas import tpu_sc as plsc

F32 = jnp.float32
BF = jnp.bfloat16

D_MODEL = 1024
HEAD_DIM = 64
N_HEADS = 16
DECAY_LORA = 64
ICLR_LORA = 64
GATE_LORA = 160
GN_EPS = 64e-5
GMLP_GROUPS = 8
GMLP_CHUNK = 128
N_EXPERTS = 256
TOP_K = 8
N_EXPERT_GROUPS = 8
TOPK_GROUPS = 4
EXPERT_FF = 256
SHARED_FF = 256
ROUTED_SCALE = 2.5
LN_EPS = 1e-5
DEPTH = 1
ALPHA = (2 * DEPTH) ** 0.25

LANES = 128
V7X_VMEM_LIMIT_BYTES = 56 * 1024 * 1024

C_RKV = 3 * D_MODEL
C_LORA = DECAY_LORA + ICLR_LORA + GATE_LORA
C_LORA_PAD = 384
C_RWKV_PAD = C_RKV + C_LORA_PAD
C_GMLP = 2 * D_MODEL
C_GATES = 2 * D_MODEL
C_IN_PAD = C_RWKV_PAD + C_GMLP + C_GATES

SCAN_CHUNK = 64
QUAD = 4 * HEAD_DIM
N_QUADS = D_MODEL // QUAD
TM = 256
SCAN_TM = 256
ROUTE_TM = 256
FFN_ROWS = 256
FINAL_TM = 256


def _dot(a, b):
    return jnp.dot(a, b, preferred_element_type=F32)


def _dot_nt(a, b):
    return lax.dot_general(a, b, (((1,), (1,)), ((), ())), preferred_element_type=F32)


def _dot_tn(a, b):
    return lax.dot_general(a, b, (((0,), (0,)), ((), ())), preferred_element_type=F32)


def _split2(z):
    hi = z.astype(BF)
    lo = (z - hi.astype(F32)).astype(BF)
    return hi, lo


def _headsum(z, g256, passes=2):
    outs = []
    for j in range(N_QUADS):
        zj = z[:, QUAD * j:QUAD * (j + 1)]
        if passes == 1:
            outs.append(_dot(zj.astype(BF), g256))
        else:
            hi, lo = _split2(zj)
            outs.append(_dot(hi, g256) + _dot(lo, g256))
    return jnp.concatenate(outs, axis=1)


def _layer_norm(v, w, b):
    mu = jnp.mean(v, axis=-1, keepdims=True)
    d = v - mu
    var = jnp.mean(d * d, axis=-1, keepdims=True)
    return d * lax.rsqrt(var + LN_EPS) * w + b


def _const_spec(shape):
    nd = len(shape)
    return pl.BlockSpec(shape, lambda *_: (0,) * nd, pipeline_mode=pl.Buffered(1))


def _front_kernel(x_ref, win_ref, mu_ref, w0_ref, a0_ref, w1_ref, w2_ref, kk_ref, ka_ref, rk_ref,
                  g256_ref, ltri_ref, lnw_ref, lnb_ref, ws_ref, sb_ref, wog_ref,
                  rt_ref, at_ref, bt_ref, kt_ref, vv_ref, bh_ref, kh_ref, wc_ref,
                  g_ref, bg_ref, ga_ref, zb_ref, prev_ref, xb_ref, *, tiles_per_seq):
    i = pl.program_id(0)
    first = (i % tiles_per_seq) == 0
    xb_ref[...] = x_ref[...].astype(BF)
    is_row0 = lax.broadcasted_iota(jnp.int32, (TM, 1), 0) == 0
    W = QUAD

    def proj(lo, width=W):
        return _dot(xb_ref[...], win_ref[:, lo:lo + width])

    def shift_mix(lo, width=W):
        p = proj(lo, width)
        prev = jnp.where(first, 0.0, prev_ref[:, lo:lo + width])
        sh = jnp.where(is_row0, prev, pltpu.roll(p, 1, 0))
        prev_ref[:, lo:lo + width] = p[TM - 1:TM, :]
        return p + mu_ref[:, lo:lo + width] * (sh - p)

    lora = shift_mix(C_RKV, C_LORA_PAD)
    lane = lax.broadcasted_iota(jnp.int32, (1, LANES), 1)
    act1 = jnp.where(lane < DECAY_LORA, jnp.tanh(lora[:, :LANES]), lora[:, :LANES]).astype(BF)
    act2 = jax.nn.sigmoid(lora[:, LANES:]).astype(BF)
    g256 = g256_ref[...]
    for q in range(N_QUADS):
        c0 = W * q
        cols = slice(c0, c0 + W)
        r = shift_mix(c0)
        k = shift_mix(D_MODEL + c0)
        v = shift_mix(2 * D_MODEL + c0)
        w = -jax.nn.softplus(-(w0_ref[:, cols] + _dot(act1, w1_ref[:, cols]))) - 0.5
        logd = -jnp.exp(w)
        a = jax.nn.sigmoid(a0_ref[:, cols] + _dot(act1, w1_ref[:, D_MODEL + c0:D_MODEL + c0 + W]))
        g = _dot(act2, w2_ref[:, cols])

        kkr = k * kk_ref[:, cols]
        kk = kkr * lax.rsqrt(jnp.maximum(_dot((kkr * kkr).astype(BF), g256), 1e-24))
        kmod = k * (1.0 + (a - 1.0) * ka_ref[:, cols])
        bonus = _dot((r * kmod * rk_ref[:, cols]).astype(BF), g256) * v
        kka = kk * a

        cum = sum(_dot(ltri_ref[...], part) for part in _split2(logd))
        e_inv = jnp.exp(-cum)
        bt = kka * e_inv
        kt = kmod * e_inv
        ends = []
        for c in range(TM // SCAN_CHUNK):
            last = SCAN_CHUNK * (c + 1) - 1
            ewc_row = jnp.exp(cum[last:last + 1, :])
            wc_ref[0, c * N_QUADS + q:c * N_QUADS + q + 1, :] = ewc_row
            ends.append(jnp.broadcast_to(ewc_row, (SCAN_CHUNK, W)))
        ewc = jnp.concatenate(ends, axis=0)
        rt_ref[q] = (r * jnp.exp(cum)).astype(BF)
        at_ref[q] = (-kk * jnp.exp(cum - logd)).astype(BF)
        bt_ref[q] = bt.astype(BF)
        kt_ref[q] = kt.astype(BF)
        vv_ref[q] = v.astype(BF)
        bh_ref[q] = (bt * ewc).astype(BF)
        kh_ref[q] = (kt * ewc).astype(BF)
        g_ref[:, cols] = g.astype(BF)
        bg_ref[:, cols] = (bonus * g).astype(BF)

    gelu = lambda t: 0.5 * t * (1.0 + lax.erf(t * np.float32(np.sqrt(0.5))))
    u0 = C_RWKV_PAD
    zv = [gelu(proj(u0 + D_MODEL + W * j)) for j in range(N_QUADS)]
    mean = sum(jnp.sum(z, axis=-1, keepdims=True) for z in zv) * (1.0 / D_MODEL)
    dv = [z - mean for z in zv]
    var = sum(jnp.sum(d * d, axis=-1, keepdims=True) for d in dv) * (1.0 / D_MODEL)
    rstd = lax.rsqrt(var + LN_EPS)
    trow = lax.broadcasted_iota(jnp.int32, (GMLP_CHUNK, GMLP_CHUNK), 0)
    tcol = lax.broadcasted_iota(jnp.int32, (GMLP_CHUNK, GMLP_CHUNK), 1)
    causal = tcol <= trow
    n_time = TM // GMLP_CHUNK
    yb = jnp.zeros((TM, D_MODEL), F32)
    for j in range(N_QUADS):
        cols = slice(W * j, W * (j + 1))
        vn = (dv[j] * rstd * lnw_ref[:, cols] + lnb_ref[:, cols]).astype(BF)
        svs = []
        for gl in range(W // LANES):
            wsg = jnp.where(causal, ws_ref[(W // LANES) * j + gl], 0.0).astype(BF)
            svs.append(jnp.concatenate(
                [_dot(wsg, vn[GMLP_CHUNK * cc:GMLP_CHUNK * (cc + 1), LANES * gl:LANES * (gl + 1)])
                 for cc in range(n_time)], axis=0))
        sb = sb_ref[:, cols]
        sv = jnp.concatenate(svs, axis=1) + jnp.concatenate([sb] * n_time, axis=0)
        u = gelu(proj(u0 + W * j))
        yb = yb + _dot((u * sv).astype(BF), wog_ref[W * j:W * (j + 1), :])

    g0 = C_RWKV_PAD + C_GMLP
    for j in range(N_QUADS):
        cols = slice(W * j, W * (j + 1))
        ga_ref[:, cols] = jax.nn.sigmoid(proj(g0 + W * j)).astype(BF)
        zb_ref[:, cols] = (jax.nn.sigmoid(proj(g0 + D_MODEL + W * j)) * yb[:, cols]).astype(BF)


def _front(x2, win, mu, w0, a0, w1, w2, k_k, k_a, r_k, g256, ltri, lnw, lnb, ws, sb, wog, seq):
    n_tok = x2.shape[0]
    n_tiles = n_tok // TM
    row = lambda i: (i, 0)
    quad_spec = pl.BlockSpec((N_QUADS, TM, QUAD), lambda i: (0, i, 0))
    quad_shape = jax.ShapeDtypeStruct((N_QUADS, n_tok, QUAD), BF)
    full_spec = pl.BlockSpec((TM, D_MODEL), row)
    full_shape = jax.ShapeDtypeStruct((n_tok, D_MODEL), BF)
    n_wc = (TM // SCAN_CHUNK) * N_QUADS
    consts = (win, mu, w0, a0, w1, w2, k_k, k_a, r_k, g256, ltri, lnw, lnb, ws, sb, wog)
    return pl.pallas_call(
        functools.partial(_front_kernel, tiles_per_seq=seq // TM),
        grid=(n_tiles,),
        in_specs=[pl.BlockSpec((TM, D_MODEL), row)] + [_const_spec(c.shape) for c in consts],
        out_specs=[quad_spec] * 7 + [pl.BlockSpec((1, n_wc, QUAD), lambda i: (i, 0, 0))] + [full_spec] * 4,
        out_shape=[quad_shape] * 7 + [jax.ShapeDtypeStruct((n_tiles, n_wc, QUAD), F32)] + [full_shape] * 4,
        scratch_shapes=[pltpu.VMEM((1, C_RWKV_PAD), F32), pltpu.VMEM((TM, D_MODEL), BF)],
        compiler_params=pltpu.CompilerParams(
            dimension_semantics=("arbitrary",), vmem_limit_bytes=V7X_VMEM_LIMIT_BYTES),
        name="front",
    )(x2, *consts)


def _scan_kernel(rt_ref, at_ref, bt_ref, kt_ref, vv_ref, bh_ref, kh_ref, wc_ref,
                 g_ref, bg_ref, ga_ref, zb_ref, x_ref, gnw_ref, gnb_ref, g256_ref,
                 wor_ref, wout_ref, ln1w_ref, ln1b_ref,
                 h_ref, hp_ref, s_ref, *, tiles_per_seq):
    i = pl.program_id(0)

    @pl.when((i % tiles_per_seq) == 0)
    def _():
        s_ref[...] = jnp.zeros_like(s_ref)

    lane = lax.broadcasted_iota(jnp.int32, (1, QUAD), 1)
    lane_head = lane // HEAD_DIM
    lane_pos = lane % HEAD_DIM
    trow = lax.broadcasted_iota(jnp.int32, (SCAN_CHUNK, 1), 0)
    strict = lane_pos < trow
    incl = lane_pos <= trow
    row_head = lax.broadcasted_iota(jnp.int32, (QUAD, 1), 0) // HEAD_DIM
    diag_blocks = row_head == lane_head

    def bd4(m):
        zero = jnp.zeros_like(m)
        return jnp.concatenate([jnp.where(lane_head == j, m, zero) for j in range(4)], axis=0)

    n_chunks = SCAN_TM // SCAN_CHUNK
    chunks_per_front_tile = TM // SCAN_CHUNK
    items = [(c, q) for c in range(n_chunks) for q in range(N_QUADS)]
    rows_of = lambda c: slice(SCAN_CHUNK * c, SCAN_CHUNK * (c + 1))
    C = SCAN_CHUNK

    at, rt, bdv, a_ab, a_akrk, a_rb = {}, {}, {}, {}, {}, {}
    for it in items:
        c, q = it
        at[it] = at_ref[q, rows_of(c), :]
        rt[it] = rt_ref[q, rows_of(c), :]
        ra = jnp.concatenate([at[it], rt[it]], axis=0)
        gb = _dot_nt(ra, bd4(bt_ref[q, rows_of(c), :]))
        gk = _dot_nt(ra, bd4(kt_ref[q, rows_of(c), :]))
        a_ab[it] = jnp.where(strict, gb[:C], 0.0)
        a_rb[it] = jnp.where(incl, gb[C:], 0.0).astype(BF)
        a_akrk[it] = jnp.concatenate(
            [jnp.where(strict, gk[:C], 0.0), jnp.where(incl, gk[C:], 0.0)], axis=0).astype(BF)
        bdv[it] = bd4(vv_ref[q, rows_of(c), :])

    tm, qpow = {}, {}
    for it in items:
        qb = a_ab[it].astype(BF)
        tm[it] = a_ab[it]
        qpow[it] = _dot(qb, bd4(qb))
    for level in range(1, 6):
        for it in items:
            qb = qpow[it].astype(BF)
            bdq = bd4(qb)
            if level < 5:
                both = _dot(jnp.concatenate([tm[it].astype(BF), qb], axis=0), bdq)
                tm[it] = tm[it] + qpow[it] + both[:C]
                qpow[it] = both[C:]
            else:
                tm[it] = tm[it] + qpow[it] + _dot(tm[it].astype(BF), bdq)

    awr, u0, rkv = {}, {}, {}
    for it in items:
        tb = tm[it].astype(BF)
        aw = at[it].astype(F32) + _dot(tb, bd4(at[it]))
        awr[it] = jnp.concatenate([aw.astype(BF), rt[it]], axis=0)
        kv = _dot(a_akrk[it], bdv[it])
        u0[it] = kv[:C] + _dot(tb, bd4(kv[:C].astype(BF)))
        rkv[it] = kv[C:]

    state = [s_ref[q] for q in range(N_QUADS)]
    ys = {}
    for c in range(n_chunks):
        for q in range(N_QUADS):
            it = (c, q)
            us = _dot_nt(awr[it], state[q].astype(BF))
            ub = (u0[it] + us[:C]).astype(BF)
            ys[it] = us[C:] + rkv[it] + _dot(a_rb[it], bd4(ub))
            uv = jnp.concatenate([ub, vv_ref[q, rows_of(c), :]], axis=0)
            bk = jnp.concatenate([bh_ref[q, rows_of(c), :], kh_ref[q, rows_of(c), :]], axis=0)
            wrow = (c % chunks_per_front_tile) * N_QUADS + q
            wc = wc_ref[c // chunks_per_front_tile, wrow:wrow + 1, :]
            state[q] = state[q] * wc + jnp.where(diag_blocks, _dot_tn(uv, bk), 0.0)
    for q in range(N_QUADS):
        s_ref[q] = state[q]

    g256 = g256_ref[...]
    y = jnp.concatenate(
        [jnp.concatenate([ys[(c, q)] for c in range(n_chunks)], axis=0) for q in range(N_QUADS)], axis=1)
    inv_n = 1.0 / HEAD_DIM
    d = y - _headsum(y, g256) * inv_n
    var = _headsum(d * d, g256) * inv_n
    yn = d * lax.rsqrt(var + GN_EPS) * gnw_ref[...] + gnb_ref[...]
    ya = _dot((yn * g_ref[...].astype(F32) + bg_ref[...].astype(F32)).astype(BF), wor_ref[...])
    mix_in = ga_ref[...].astype(F32) * ya + zb_ref[...].astype(F32)
    mix = _dot(mix_in.astype(BF), wout_ref[...])
    h = _layer_norm(ALPHA * x_ref[...] + mix, ln1w_ref[...], ln1b_ref[...])
    h_ref[...] = h
    hb = pltpu.bitcast(h.astype(BF).astype(F32), jnp.uint32)
    half = D_MODEL // 2
    hp_ref[...] = (hb[:, :half] >> 16) | (hb[:, half:] & jnp.uint32(0xFFFF0000))


def _scan(quads, wc, g, bg, ga, zb, x2, gnw, gnb, g256, wor, wout, ln1w, ln1b, seq):
    n_tok = x2.shape[0]
    n_tiles = n_tok // SCAN_TM
    row = lambda i: (i, 0)
    quad_spec = pl.BlockSpec((N_QUADS, SCAN_TM, QUAD), lambda i: (0, i, 0))
    full_spec = pl.BlockSpec((SCAN_TM, D_MODEL), row)
    n_wc = wc.shape[1]
    consts = (gnw, gnb, g256, wor, wout, ln1w, ln1b)
    return pl.pallas_call(
        functools.partial(_scan_kernel, tiles_per_seq=seq // SCAN_TM),
        grid=(n_tiles,),
        in_specs=[quad_spec] * 7 + [pl.BlockSpec((SCAN_TM // TM, n_wc, QUAD), lambda i: (i, 0, 0))]
        + [full_spec] * 5 + [_const_spec(c.shape) for c in consts],
        out_specs=[full_spec, pl.BlockSpec((SCAN_TM, D_MODEL // 2), row)],
        out_shape=[jax.ShapeDtypeStruct((n_tok, D_MODEL), F32),
                   jax.ShapeDtypeStruct((n_tok, D_MODEL // 2), jnp.uint32)],
        scratch_shapes=[pltpu.VMEM((N_QUADS, QUAD, QUAD), F32)],
        compiler_params=pltpu.CompilerParams(
            dimension_semantics=("arbitrary",), vmem_limit_bytes=V7X_VMEM_LIMIT_BYTES),
        name="scan",
    )(*quads, wc, g, bg, ga, zb, x2, *consts)


def _first_max(x, idx, n):
    m = jnp.max(x, axis=0, keepdims=True)
    first = jnp.min(jnp.where(x == m, idx, n), axis=0, keepdims=True)
    return m, first


def _route_kernel(h_ref, wrh_ref, wrl_ref, bias_ref, ustrict_ref, e_ref, gate_ref, rank_ref, cnt_ref,
                  carry_ref):
    @pl.when(pl.program_id(0) == 0)
    def _():
        carry_ref[...] = jnp.zeros_like(carry_ref)

    hh, hl = _split2(h_ref[...])
    wrh = wrh_ref[...]
    logits = _dot_nt(wrh, hh) + _dot_nt(wrh, hl) + _dot_nt(wrl_ref[...], hh)
    scores = jax.nn.sigmoid(logits)
    choice = scores + bias_ref[...]
    n_tok = choice.shape[1]
    per = N_EXPERTS // N_EXPERT_GROUPS
    eidx = lax.broadcasted_iota(jnp.int32, (N_EXPERTS, 1), 0)
    gidx = lax.broadcasted_iota(jnp.int32, (N_EXPERT_GROUPS, 1), 0)
    lidx = lax.broadcasted_iota(jnp.int32, (per, 1), 0)
    neg = -jnp.inf

    gsc = []
    for gi in range(N_EXPERT_GROUPS):
        blk = choice[per * gi:per * (gi + 1), :]
        m1, i1 = _first_max(blk, lidx, per)
        m2 = jnp.max(jnp.where(lidx == i1, neg, blk), axis=0, keepdims=True)
        gsc.append(m1 + m2)
    gscore = jnp.concatenate(gsc, axis=0)
    gsel = jnp.zeros(gscore.shape, jnp.bool_)
    for _ in range(TOPK_GROUPS):
        _, gi1 = _first_max(gscore, gidx, N_EXPERT_GROUPS)
        hit = gidx == gi1
        gsel = gsel | hit
        gscore = jnp.where(hit, neg, gscore)
    emask = jnp.concatenate(
        [jnp.broadcast_to(gsel[gi:gi + 1, :], (per, n_tok)) for gi in range(N_EXPERT_GROUPS)], axis=0)
    cand = jnp.where(emask, choice, neg)
    es, gs, hits = [], [], []
    for _ in range(TOP_K):
        _, e1 = _first_max(cand, eidx, N_EXPERTS)
        hit = eidx == e1
        es.append(e1)
        hits.append(hit)
        gs.append(jnp.sum(jnp.where(hit, scores, 0.0), axis=0, keepdims=True))
        cand = jnp.where(hit, neg, cand)
    gates = jnp.concatenate(gs, axis=0)
    gates = gates / jnp.sum(gates, axis=0, keepdims=True) * ROUTED_SCALE
    e_ref[...] = jnp.concatenate(es, axis=0)
    gate_ref[...] = gates

    chosen = functools.reduce(jnp.logical_or, hits)
    sel = jnp.where(chosen, 1.0, 0.0)
    before = _dot(sel.astype(BF), ustrict_ref[...]) + carry_ref[:, 0:1]
    rank_ref[...] = jnp.concatenate(
        [jnp.sum(jnp.where(hit, before, 0.0), axis=0, keepdims=True) for hit in hits],
        axis=0).astype(jnp.int32)
    total = before[:, n_tok - 1:n_tok] + sel[:, n_tok - 1:n_tok]
    carry_ref[...] = jnp.broadcast_to(total, carry_ref.shape)
    cnt_ref[...] = jnp.broadcast_to(total, cnt_ref.shape).astype(jnp.int32)


def _route(h, wrh, wrl, bias_col, ustrict):
    n_tok = h.shape[0]
    col = lambda i: (0, i)
    tok_spec = pl.BlockSpec((TOP_K, ROUTE_TM), col)
    return pl.pallas_call(
        _route_kernel,
        grid=(n_tok // ROUTE_TM,),
        in_specs=[pl.BlockSpec((ROUTE_TM, D_MODEL), lambda i: (i, 0)),
                  _const_spec(wrh.shape), _const_spec(wrl.shape), _const_spec(bias_col.shape),
                  _const_spec(ustrict.shape)],
        out_specs=[tok_spec, tok_spec, tok_spec, pl.BlockSpec((N_EXPERTS, LANES), lambda i: (0, 0))],
        out_shape=[jax.ShapeDtypeStruct((TOP_K, n_tok), jnp.int32),
                   jax.ShapeDtypeStruct((TOP_K, n_tok), F32),
                   jax.ShapeDtypeStruct((TOP_K, n_tok), jnp.int32),
                   jax.ShapeDtypeStruct((N_EXPERTS, LANES), jnp.int32)],
        scratch_shapes=[pltpu.VMEM((N_EXPERTS, LANES), F32)],
        compiler_params=pltpu.CompilerParams(dimension_semantics=("arbitrary",)),
        name="route",
    )(h, wrh, wrl, bias_col, ustrict)


def _rows_kernel(e_ref, rank_ref, base_ref, pos_ref):
    eidx = lax.broadcasted_iota(jnp.int32, (N_EXPERTS, 1), 0)
    base = base_ref[...]
    pos_ref[...] = rank_ref[...] + jnp.concatenate(
        [jnp.sum(jnp.where(eidx == e_ref[kk:kk + 1, :], base, 0), axis=0, keepdims=True)
         for kk in range(TOP_K)], axis=0)


def _rows(top_e, rank, base_col):
    n_tok = top_e.shape[1]
    tok_spec = pl.BlockSpec((TOP_K, ROUTE_TM), lambda i: (0, i))
    return pl.pallas_call(
        _rows_kernel,
        grid=(n_tok // ROUTE_TM,),
        in_specs=[tok_spec, tok_spec, _const_spec(base_col.shape)],
        out_specs=tok_spec,
        out_shape=jax.ShapeDtypeStruct((TOP_K, n_tok), jnp.int32),
        compiler_params=pltpu.CompilerParams(dimension_semantics=("arbitrary",)),
        name="rows",
    )(top_e, rank, base_col)


SC_WINDOW = 128


def _sc_mesh():
    return plsc.VectorSubcoreMesh(core_axis_name="c", subcore_axis_name="s")


def _sc_workers():
    info = plsc.get_sparse_core_info()
    return info.num_cores, info.num_cores * info.num_subcores


def _sc_scatter_rows(src, pos, n_rows):
    n_tok, width = src.shape
    n_k = pos.shape[0]
    n_cores, n_workers = _sc_workers()
    n_win = n_tok // (n_workers * SC_WINDOW)

    @functools.partial(
        pl.kernel, out_type=jax.ShapeDtypeStruct((n_rows, width), src.dtype), mesh=_sc_mesh(),
        scratch_types=[pltpu.VMEM((n_k * n_win, SC_WINDOW), jnp.int32),
                       pltpu.VMEM((SC_WINDOW, width), src.dtype),
                       pltpu.SemaphoreType.DMA])
    def scatter(src_hbm, pos_hbm, out_hbm, idx_v, rows_v, sem):
        wid = lax.axis_index("s") * n_cores + lax.axis_index("c")
        pltpu.sync_copy(pos_hbm.at[wid], idx_v)

        @pl.loop(0, n_win)
        def _(j):
            pltpu.sync_copy(src_hbm.at[pl.ds((wid * n_win + j) * SC_WINDOW, SC_WINDOW)], rows_v)
            copies = [pltpu.async_copy(rows_v, out_hbm.at[idx_v.at[kk * n_win + j]], sem)
                      for kk in range(n_k)]
            for cp in copies:
                cp.wait()

    pos4 = pos.reshape(n_k, n_workers, n_win, SC_WINDOW).transpose(1, 0, 2, 3)
    return scatter(src, pos4.reshape(n_workers, n_k * n_win, SC_WINDOW))


def _sc_gather_rows(src, idx):
    n_idx = idx.shape[0]
    width = src.shape[1]
    n_cores, n_workers = _sc_workers()
    n_win = n_idx // (n_workers * SC_WINDOW)

    @functools.partial(
        pl.kernel, out_type=jax.ShapeDtypeStruct((n_idx, width), src.dtype), mesh=_sc_mesh(),
        scratch_types=[pltpu.VMEM((n_win, SC_WINDOW), jnp.int32),
                       pltpu.VMEM((SC_WINDOW, width), src.dtype),
                       pltpu.SemaphoreType.DMA])
    def gather(src_hbm, idx_hbm, out_hbm, idx_v, rows_v, sem):
        wid = lax.axis_index("s") * n_cores + lax.axis_index("c")
        pltpu.sync_copy(idx_hbm.at[wid], idx_v)

        @pl.loop(0, n_win)
        def _(j):
            pltpu.async_copy(src_hbm.at[idx_v.at[j]], rows_v, sem).wait()
            pltpu.sync_copy(rows_v, out_hbm.at[pl.ds((wid * n_win + j) * SC_WINDOW, SC_WINDOW)])

    return gather(src, idx.reshape(n_workers, n_win, SC_WINDOW))


def _unpack_pair(words):
    lo = pltpu.bitcast(words << 16, F32).astype(BF)
    hi = pltpu.bitcast(words & jnp.uint32(0xFFFF0000), F32).astype(BF)
    return lo, hi


def _pack_pair(val):
    bits = pltpu.bitcast(val.astype(BF).astype(F32), jnp.uint32)
    half = val.shape[1] // 2
    return (bits[:, :half] >> 16) | (bits[:, half:] & jnp.uint32(0xFFFF0000))


def _ffn_kernel(base_ref, cnt_ref, xs_ref, wgu_ref, wd_ref, out_ref,
                xbuf, obuf, wgu16_ref, wd16_ref, sem_in, sem_out, pend_ref, *, n_blocks):
    e = pl.program_id(0)
    n_rows_e = cnt_ref[e]
    n_blk = (n_rows_e + FFN_ROWS - 1) // FFN_ROWS
    first_row = base_ref[e]
    half = D_MODEL // 2

    def in_copy(row0, slot):
        return pltpu.make_async_copy(xs_ref.at[pl.ds(row0, FFN_ROWS), :], xbuf.at[slot], sem_in.at[slot])

    def out_copy(row0, slot):
        return pltpu.make_async_copy(obuf.at[slot], out_ref.at[pl.ds(row0, FFN_ROWS), :], sem_out.at[slot])

    @pl.when(e == 0)
    def _():
        pend_ref[0] = 0
        pend_ref[1] = 0

        @pl.when(n_blk > 0)
        def _():
            in_copy(pl.multiple_of(first_row, FFN_ROWS), 0).start()

    @pl.when(n_blk > 0)
    def _():
        wgu16_ref[...] = wgu_ref[0].astype(BF)
        wd16_ref[...] = wd_ref[0].astype(BF)

    row = lax.broadcasted_iota(jnp.int32, (FFN_ROWS, 1), 0)

    def block(j, carry):
        slot = j % 2
        row0 = pl.multiple_of(first_row + j * FFN_ROWS, FFN_ROWS)
        in_copy(row0, slot).wait()

        @pl.when(j + 1 < n_blk)
        def _():
            in_copy(row0 + FFN_ROWS, 1 - slot).start()

        words = jnp.where(row < n_rows_e - j * FFN_ROWS, xbuf[slot], jnp.uint32(0))
        lo, hi = _unpack_pair(words)
        gu = _dot(lo, wgu16_ref[0:half, :]) + _dot(hi, wgu16_ref[half:, :])
        act = (jax.nn.silu(gu[:, :EXPERT_FF]) * gu[:, EXPERT_FF:]).astype(BF)
        packed = _pack_pair(_dot(act, wd16_ref[...]))

        @pl.when(pend_ref[slot] == 1)
        def _():
            out_copy(row0, slot).wait()

        obuf[slot] = packed
        out_copy(row0, slot).start()
        pend_ref[slot] = 1
        return carry

    lax.fori_loop(0, n_blk, block, 0)

    @pl.when(e + 1 < N_EXPERTS)
    def _():
        @pl.when(cnt_ref[e + 1] > 0)
        def _():
            in_copy(pl.multiple_of(base_ref[e + 1], FFN_ROWS), 0).start()

    @pl.when(e == N_EXPERTS - 1)
    def _():
        for slot in range(2):
            @pl.when(pend_ref[slot] == 1)
            def _():
                out_copy(0, slot).wait()
        obuf[0] = jnp.zeros((FFN_ROWS, half), jnp.uint32)
        used = (first_row + n_blk * FFN_ROWS) // FFN_ROWS

        def fill(b, carry):
            cp = out_copy(pl.multiple_of(b * FFN_ROWS, FFN_ROWS), 0)
            cp.start()
            cp.wait()
            return carry

        lax.fori_loop(used, n_blocks, fill, 0)


def _ffn(base, counts, xs, wgu, wd):
    n_rows = xs.shape[0]
    half = D_MODEL // 2
    grid_spec = pltpu.PrefetchScalarGridSpec(
        num_scalar_prefetch=2,
        grid=(N_EXPERTS,),
        in_specs=[pl.BlockSpec(memory_space=pl.ANY),
                  pl.BlockSpec((1, D_MODEL, 2 * EXPERT_FF), lambda e, base, cnt: (e, 0, 0)),
                  pl.BlockSpec((1, EXPERT_FF, D_MODEL), lambda e, base, cnt: (e, 0, 0))],
        out_specs=pl.BlockSpec(memory_space=pl.ANY),
        scratch_shapes=[pltpu.VMEM((2, FFN_ROWS, half), jnp.uint32),
                        pltpu.VMEM((2, FFN_ROWS, half), jnp.uint32),
                        pltpu.VMEM((D_MODEL, 2 * EXPERT_FF), BF),
                        pltpu.VMEM((EXPERT_FF, D_MODEL), BF),
                        pltpu.SemaphoreType.DMA((2,)),
                        pltpu.SemaphoreType.DMA((2,)),
                        pltpu.SMEM((2,), jnp.int32)],
    )
    return pl.pallas_call(
        functools.partial(_ffn_kernel, n_blocks=n_rows // FFN_ROWS),
        grid_spec=grid_spec,
        out_shape=jax.ShapeDtypeStruct((n_rows, half), jnp.uint32),
        compiler_params=pltpu.CompilerParams(dimension_semantics=("arbitrary",)),
        name="ffn",
    )(base, counts, xs, wgu, wd)


def _final_kernel(rows_ref, gate_ref, h_ref, wsgu_ref, wsd_ref, ln2w_ref, ln2b_ref, o_ref):
    h = h_ref[...]
    hb = h.astype(BF)
    gu = _dot(hb, wsgu_ref[...])
    act = (jax.nn.silu(gu[:, :SHARED_FF]) * gu[:, SHARED_FF:]).astype(BF)
    shared = _dot(act, wsd_ref[...])

    half = D_MODEL // 2
    lo_acc = jnp.zeros((FINAL_TM, half), F32)
    hi_acc = jnp.zeros((FINAL_TM, half), F32)
    gates = gate_ref[...]
    for kk in range(TOP_K):
        words = rows_ref[kk]
        gk = gates[:, kk:kk + 1]
        lo_acc = lo_acc + gk * pltpu.bitcast(words << 16, F32)
        hi_acc = hi_acc + gk * pltpu.bitcast(words & jnp.uint32(0xFFFF0000), F32)
    routed = jnp.concatenate([lo_acc, hi_acc], axis=1)
    o_ref[...] = _layer_norm(ALPHA * h + routed + shared, ln2w_ref[...], ln2b_ref[...])


def _final(rows, gates_t, h, wsgu, wsd, ln2w, ln2b):
    n_tok = h.shape[0]
    n_tiles = n_tok // FINAL_TM
    half = D_MODEL // 2
    row = lambda i: (i, 0)
    consts = (wsgu, wsd, ln2w, ln2b)
    return pl.pallas_call(
        _final_kernel,
        grid=(n_tiles,),
        in_specs=[pl.BlockSpec((TOP_K, FINAL_TM, half), lambda i: (0, i, 0)),
                  pl.BlockSpec((FINAL_TM, TOP_K), row),
                  pl.BlockSpec((FINAL_TM, D_MODEL), row)] + [_const_spec(c.shape) for c in consts],
        out_specs=pl.BlockSpec((FINAL_TM, D_MODEL), row),
        out_shape=jax.ShapeDtypeStruct((n_tok, D_MODEL), F32),
        compiler_params=pltpu.CompilerParams(dimension_semantics=("arbitrary",)),
        name="final",
    )(rows, gates_t, h, *consts)


def _np_consts():
    idx = np.arange(QUAD)
    g256 = (idx[:, None] // HEAD_DIM == idx[None, :] // HEAD_DIM).astype(np.float32)
    t = np.arange(TM)
    same = t[:, None] // SCAN_CHUNK == t[None, :] // SCAN_CHUNK
    ltri = (same & (t[None, :] <= t[:, None])).astype(np.float32)
    return jnp.asarray(g256, BF), jnp.asarray(ltri, BF)


def _block_plan(counts, n_assign):
    padded = (counts + FFN_ROWS - 1) // FFN_ROWS * FFN_ROWS
    base = jnp.cumsum(padded) - padded
    n_blocks = (n_assign + N_EXPERTS * (FFN_ROWS - 1) + FFN_ROWS - 1) // FFN_ROWS
    return base.astype(jnp.int32), n_blocks * FFN_ROWS


def _layer(x2, seq, w_in, shift_mu, w0, decay_lora, a0, iclr_lora, gate_lora, k_k, k_a, r_k, gn_w, gn_b,
           lnw, lnb, spatial_w, spatial_b, w_o_rwkv, w_o_gmlp, w_out, ln1_w, ln1_b,
           w_router, router_bias, w_exp_gate_up, w_exp_down, w_sh_gate_up, w_sh_down, ln2_w, ln2_b):
    n_tok = x2.shape[0]
    c_rwkv = C_RKV + C_LORA
    pad = C_LORA_PAD - C_LORA
    win = jnp.concatenate([w_in[:, :c_rwkv], jnp.zeros((D_MODEL, pad), F32), w_in[:, c_rwkv:]], axis=1).astype(BF)
    mu = jnp.pad(shift_mu, (0, pad)).reshape(1, C_RWKV_PAD)
    row1 = lambda t: t.reshape(1, -1)
    w1 = jnp.zeros((LANES, 2 * D_MODEL), F32)
    w1 = w1.at[:DECAY_LORA, :D_MODEL].set(decay_lora).at[DECAY_LORA:, D_MODEL:].set(iclr_lora).astype(BF)
    w2 = jnp.zeros((C_LORA_PAD - LANES, D_MODEL), F32).at[:GATE_LORA].set(gate_lora).astype(BF)
    g256, ltri = _np_consts()
    sb_map = jnp.repeat(spatial_b.T, D_MODEL // GMLP_GROUPS, axis=1)

    outs = _front(x2, win, mu, row1(w0), row1(a0), w1, w2, row1(k_k), row1(k_a), row1(r_k),
                  g256, ltri, row1(lnw), row1(lnb), spatial_w, sb_map, w_o_gmlp.astype(BF), seq)
    quads, wc, (g, bg, ga, zb) = outs[:7], outs[7], outs[8:]
    h1, h1p = _scan(quads, wc, g, bg, ga, zb, x2, row1(gn_w), row1(gn_b), g256,
                    w_o_rwkv.astype(BF), w_out.astype(BF), row1(ln1_w), row1(ln1_b), seq)

    wrt = w_router.T
    wrh = wrt.astype(BF)
    wrl = (wrt - wrh.astype(F32)).astype(BF)
    tt = np.arange(ROUTE_TM)
    ustrict = jnp.asarray((tt[:, None] < tt[None, :]).astype(np.float32), BF)
    top_e, gates, rank, counts = _route(h1, wrh, wrl, router_bias.reshape(N_EXPERTS, 1), ustrict)
    counts = counts[:, 0]
    base, n_rows = _block_plan(counts, n_tok * TOP_K)
    pos = _rows(top_e, rank, base.reshape(N_EXPERTS, 1))
    xs = _sc_scatter_rows(h1p, pos, n_rows)
    rows = _ffn(base, counts, xs, w_exp_gate_up, w_exp_down)
    picked = _sc_gather_rows(rows, pos.reshape(-1)).reshape(TOP_K, n_tok, D_MODEL // 2)
    return _final(picked, gates.T, h1, w_sh_gate_up.astype(BF), w_sh_down.astype(BF),
                  row1(ln2_w), row1(ln2_b))


def kernel(x, w_in, shift_mu, rwkv_w0, rwkv_decay_lora, rwkv_a0, rwkv_iclr_lora, rwkv_gate_lora, rwkv_k_k, rwkv_k_a, rwkv_r_k, rwkv_gn_w, rwkv_gn_b, gmlp_ln_w, gmlp_ln_b, gmlp_spatial_w, gmlp_spatial_b, w_o_rwkv, w_o_gmlp, w_out, ln1_w, ln1_b, w_router, router_bias, w_exp_gate_up, w_exp_down, w_sh_gate_up, w_sh_down, ln2_w, ln2_b):
    bsz, seq, d = x.shape
    h = x.reshape(bsz * seq, d)
    for l in range(DEPTH):
        h = _layer(h, seq, w_in[l], shift_mu[l], rwkv_w0[l], rwkv_decay_lora[l], rwkv_a0[l],
                   rwkv_iclr_lora[l], rwkv_gate_lora[l], rwkv_k_k[l], rwkv_k_a[l], rwkv_r_k[l],
                   rwkv_gn_w[l], rwkv_gn_b[l], gmlp_ln_w[l], gmlp_ln_b[l], gmlp_spatial_w[l],
                   gmlp_spatial_b[l], w_o_rwkv[l], w_o_gmlp[l], w_out[l], ln1_w[l], ln1_b[l],
                   w_router[l], router_bias[l], w_exp_gate_up[l], w_exp_down[l], w_sh_gate_up[l],
                   w_sh_down[l], ln2_w[l], ln2_b[l])
    return h.reshape(bsz, seq, d)
```

```python
import functools

import numpy as np
import jax
import jax.numpy as jnp
from jax import lax
from jax.experimental import pallas as pl
from jax.experimental.pallas import tpu as pltpu
from jax.experimental.pallas import tpu_sc as plsc

F32 = jnp.float32
BF = jnp.bfloat16

D_MODEL = 1024
HEAD_DIM = 64
N_HEADS = 16
DECAY_LORA = 64
ICLR_LORA = 64
GATE_LORA = 160
GN_EPS = 64e-5
GMLP_GROUPS = 8
GMLP_CHUNK = 128
N_EXPERTS = 256
TOP_K = 8
N_EXPERT_GROUPS = 8
TOPK_GROUPS = 4
EXPERT_FF = 256
SHARED_FF = 256
ROUTED_SCALE = 2.5
LN_EPS = 1e-5
DEPTH = 1
ALPHA = (2 * DEPTH) ** 0.25

LANES = 128
V7X_VMEM_LIMIT_BYTES = 56 * 1024 * 1024

C_RKV = 3 * D_MODEL
C_LORA = DECAY_LORA + ICLR_LORA + GATE_LORA
C_LORA_PAD = 384
C_RWKV_PAD = C_RKV + C_LORA_PAD
C_GMLP = 2 * D_MODEL
C_GATES = 2 * D_MODEL
C_IN_PAD = C_RWKV_PAD + C_GMLP + C_GATES

SCAN_CHUNK = 64
QUAD = 4 * HEAD_DIM
N_QUADS = D_MODEL // QUAD
TM = 256
SCAN_TM = 256
ROUTE_TM = 256
FFN_ROWS = 256
FINAL_TM = 256


def _dot(a, b):
    return jnp.dot(a, b, preferred_element_type=F32)


def _dot_nt(a, b):
    return lax.dot_general(a, b, (((1,), (1,)), ((), ())), preferred_element_type=F32)


def _dot_tn(a, b):
    return lax.dot_general(a, b, (((0,), (0,)), ((), ())), preferred_element_type=F32)


def _split2(z):
    hi = z.astype(BF)
    lo = (z - hi.astype(F32)).astype(BF)
    return hi, lo


def _headsum(z, g256, passes=2):
    outs = []
    for j in range(N_QUADS):
        zj = z[:, QUAD * j:QUAD * (j + 1)]
        if passes == 1:
            outs.append(_dot(zj.astype(BF), g256))
        else:
            hi, lo = _split2(zj)
            outs.append(_dot(hi, g256) + _dot(lo, g256))
    return jnp.concatenate(outs, axis=1)


def _layer_norm(v, w, b):
    mu = jnp.mean(v, axis=-1, keepdims=True)
    d = v - mu
    var = jnp.mean(d * d, axis=-1, keepdims=True)
    return d * lax.rsqrt(var + LN_EPS) * w + b


def _const_spec(shape):
    nd = len(shape)
    return pl.BlockSpec(shape, lambda *_: (0,) * nd, pipeline_mode=pl.Buffered(1))


def _front_kernel(x_ref, win_ref, mu_ref, w0_ref, a0_ref, w1_ref, w2_ref, kk_ref, ka_ref, rk_ref,
                  g256_ref, ltri_ref, lnw_ref, lnb_ref, ws_ref, sb_ref, wog_ref,
                  rt_ref, at_ref, bt_ref, kt_ref, vv_ref, bh_ref, kh_ref, wc_ref,
                  g_ref, bg_ref, ga_ref, zb_ref, prev_ref, xb_ref, *, tiles_per_seq):
    i = pl.program_id(0)
    first = (i % tiles_per_seq) == 0
    xb_ref[...] = x_ref[...].astype(BF)
    is_row0 = lax.broadcasted_iota(jnp.int32, (TM, 1), 0) == 0
    W = QUAD

    def proj(lo, width=W):
        return _dot(xb_ref[...], win_ref[:, lo:lo + width])

    def shift_mix(lo, width=W):
        p = proj(lo, width)
        prev = jnp.where(first, 0.0, prev_ref[:, lo:lo + width])
        sh = jnp.where(is_row0, prev, pltpu.roll(p, 1, 0))
        prev_ref[:, lo:lo + width] = p[TM - 1:TM, :]
        return p + mu_ref[:, lo:lo + width] * (sh - p)

    lora = shift_mix(C_RKV, C_LORA_PAD)
    lane = lax.broadcasted_iota(jnp.int32, (1, LANES), 1)
    act1 = jnp.where(lane < DECAY_LORA, jnp.tanh(lora[:, :LANES]), lora[:, :LANES]).astype(BF)
    act2 = jax.nn.sigmoid(lora[:, LANES:]).astype(BF)
    g256 = g256_ref[...]
    for q in range(N_QUADS):
        c0 = W * q
        cols = slice(c0, c0 + W)
        r = shift_mix(c0)
        k = shift_mix(D_MODEL + c0)
        v = shift_mix(2 * D_MODEL + c0)
        w = -jax.nn.softplus(-(w0_ref[:, cols] + _dot(act1, w1_ref[:, cols]))) - 0.5
        logd = -jnp.exp(w)
        a = jax.nn.sigmoid(a0_ref[:, cols] + _dot(act1, w1_ref[:, D_MODEL + c0:D_MODEL + c0 + W]))
        g = _dot(act2, w2_ref[:, cols])

        kkr = k * kk_ref[:, cols]
        kk = kkr * lax.rsqrt(jnp.maximum(_dot((kkr * kkr).astype(BF), g256), 1e-24))
        kmod = k * (1.0 + (a - 1.0) * ka_ref[:, cols])
        bonus = _dot((r * kmod * rk_ref[:, cols]).astype(BF), g256) * v
        kka = kk * a

        cum = sum(_dot(ltri_ref[...], part) for part in _split2(logd))
        e_inv = jnp.exp(-cum)
        bt = kka * e_inv
        kt = kmod * e_inv
        ends = []
        for c in range(TM // SCAN_CHUNK):
            last = SCAN_CHUNK * (c + 1) - 1
            ewc_row = jnp.exp(cum[last:last + 1, :])
            wc_ref[0, c * N_QUADS + q:c * N_QUADS + q + 1, :] = ewc_row
            ends.append(jnp.broadcast_to(ewc_row, (SCAN_CHUNK, W)))
        ewc = jnp.concatenate(ends, axis=0)
        rt_ref[q] = (r * jnp.exp(cum)).astype(BF)
        at_ref[q] = (-kk * jnp.exp(cum - logd)).astype(BF)
        bt_ref[q] = bt.astype(BF)
        kt_ref[q] = kt.astype(BF)
        vv_ref[q] = v.astype(BF)
        bh_ref[q] = (bt * ewc).astype(BF)
        kh_ref[q] = (kt * ewc).astype(BF)
        g_ref[:, cols] = g.astype(BF)
        bg_ref[:, cols] = (bonus * g).astype(BF)

    gelu = lambda t: 0.5 * t * (1.0 + lax.erf(t * np.float32(np.sqrt(0.5))))
    u0 = C_RWKV_PAD
    zv = [gelu(proj(u0 + D_MODEL + W * j)) for j in range(N_QUADS)]
    mean = sum(jnp.sum(z, axis=-1, keepdims=True) for z in zv) * (1.0 / D_MODEL)
    dv = [z - mean for z in zv]
    var = sum(jnp.sum(d * d, axis=-1, keepdims=True) for d in dv) * (1.0 / D_MODEL)
    rstd = lax.rsqrt(var + LN_EPS)
    trow = lax.broadcasted_iota(jnp.int32, (GMLP_CHUNK, GMLP_CHUNK), 0)
    tcol = lax.broadcasted_iota(jnp.int32, (GMLP_CHUNK, GMLP_CHUNK), 1)
    causal = tcol <= trow
    n_time = TM // GMLP_CHUNK
    yb = jnp.zeros((TM, D_MODEL), F32)
    for j in range(N_QUADS):
        cols = slice(W * j, W * (j + 1))
        vn = (dv[j] * rstd * lnw_ref[:, cols] + lnb_ref[:, cols]).astype(BF)
        svs = []
        for gl in range(W // LANES):
            wsg = jnp.where(causal, ws_ref[(W // LANES) * j + gl], 0.0).astype(BF)
            svs.append(jnp.concatenate(
                [_dot(wsg, vn[GMLP_CHUNK * cc:GMLP_CHUNK * (cc + 1), LANES * gl:LANES * (gl + 1)])
                 for cc in range(n_time)], axis=0))
        sb = sb_ref[:, cols]
        sv = jnp.concatenate(svs, axis=1) + jnp.concatenate([sb] * n_time, axis=0)
        u = gelu(proj(u0 + W * j))
        yb = yb + _dot((u * sv).astype(BF), wog_ref[W * j:W * (j + 1), :])

    g0 = C_RWKV_PAD + C_GMLP
    for j in range(N_QUADS):
        cols = slice(W * j, W * (j + 1))
        ga_ref[:, cols] = jax.nn.sigmoid(proj(g0 + W * j)).astype(BF)
        zb_ref[:, cols] = (jax.nn.sigmoid(proj(g0 + D_MODEL + W * j)) * yb[:, cols]).astype(BF)


def _front(x2, win, mu, w0, a0, w1, w2, k_k, k_a, r_k, g256, ltri, lnw, lnb, ws, sb, wog, seq):
    n_tok = x2.shape[0]
    n_tiles = n_tok // TM
    row = lambda i: (i, 0)
    quad_spec = pl.BlockSpec((N_QUADS, TM, QUAD), lambda i: (0, i, 0))
    quad_shape = jax.ShapeDtypeStruct((N_QUADS, n_tok, QUAD), BF)
    full_spec = pl.BlockSpec((TM, D_MODEL), row)
    full_shape = jax.ShapeDtypeStruct((n_tok, D_MODEL), BF)
    n_wc = (TM // SCAN_CHUNK) * N_QUADS
    consts = (win, mu, w0, a0, w1, w2, k_k, k_a, r_k, g256, ltri, lnw, lnb, ws, sb, wog)
    return pl.pallas_call(
        functools.partial(_front_kernel, tiles_per_seq=seq // TM),
        grid=(n_tiles,),
        in_specs=[pl.BlockSpec((TM, D_MODEL), row)] + [_const_spec(c.shape) for c in consts],
        out_specs=[quad_spec] * 7 + [pl.BlockSpec((1, n_wc, QUAD), lambda i: (i, 0, 0))] + [full_spec] * 4,
        out_shape=[quad_shape] * 7 + [jax.ShapeDtypeStruct((n_tiles, n_wc, QUAD), F32)] + [full_shape] * 4,
        scratch_shapes=[pltpu.VMEM((1, C_RWKV_PAD), F32), pltpu.VMEM((TM, D_MODEL), BF)],
        compiler_params=pltpu.CompilerParams(
            dimension_semantics=("arbitrary",), vmem_limit_bytes=V7X_VMEM_LIMIT_BYTES),
        name="front",
    )(x2, *consts)


def _scan_kernel(rt_ref, at_ref, bt_ref, kt_ref, vv_ref, bh_ref, kh_ref, wc_ref,
                 g_ref, bg_ref, ga_ref, zb_ref, x_ref, gnw_ref, gnb_ref, g256_ref,
                 wor_ref, wout_ref, ln1w_ref, ln1b_ref,
                 h_ref, hp_ref, s_ref, *, tiles_per_seq):
    i = pl.program_id(0)

    @pl.when((i % tiles_per_seq) == 0)
    def _():
        s_ref[...] = jnp.zeros_like(s_ref)

    lane = lax.broadcasted_iota(jnp.int32, (1, QUAD), 1)
    lane_head = lane // HEAD_DIM
    lane_pos = lane % HEAD_DIM
    trow = lax.broadcasted_iota(jnp.int32, (SCAN_CHUNK, 1), 0)
    strict = lane_pos < trow
    incl = lane_pos <= trow
    row_head = lax.broadcasted_iota(jnp.int32, (QUAD, 1), 0) // HEAD_DIM
    diag_blocks = row_head == lane_head

    def bd4(m):
        zero = jnp.zeros_like(m)
        return jnp.concatenate([jnp.where(lane_head == j, m, zero) for j in range(4)], axis=0)

    n_chunks = SCAN_TM // SCAN_CHUNK
    chunks_per_front_tile = TM // SCAN_CHUNK
    items = [(c, q) for c in range(n_chunks) for q in range(N_QUADS)]
    rows_of = lambda c: slice(SCAN_CHUNK * c, SCAN_CHUNK * (c + 1))
    C = SCAN_CHUNK

    at, rt, bdv, a_ab, a_akrk, a_rb = {}, {}, {}, {}, {}, {}
    for it in items:
        c, q = it
        at[it] = at_ref[q, rows_of(c), :]
        rt[it] = rt_ref[q, rows_of(c), :]
        ra = jnp.concatenate([at[it], rt[it]], axis=0)
        gb = _dot_nt(ra, bd4(bt_ref[q, rows_of(c), :]))
        gk = _dot_nt(ra, bd4(kt_ref[q, rows_of(c), :]))
        a_ab[it] = jnp.where(strict, gb[:C], 0.0)
        a_rb[it] = jnp.where(incl, gb[C:], 0.0).astype(BF)
        a_akrk[it] = jnp.concatenate(
            [jnp.where(strict, gk[:C], 0.0), jnp.where(incl, gk[C:], 0.0)], axis=0).astype(BF)
        bdv[it] = bd4(vv_ref[q, rows_of(c), :])

    tm, qpow = {}, {}
    for it in items:
        qb = a_ab[it].astype(BF)
        tm[it] = a_ab[it]
        qpow[it] = _dot(qb, bd4(qb))
    for level in range(1, 6):
        for it in items:
            qb = qpow[it].astype(BF)
            bdq = bd4(qb)
            if level < 5:
                both = _dot(jnp.concatenate([tm[it].astype(BF), qb], axis=0), bdq)
                tm[it] = tm[it] + qpow[it] + both[:C]
                qpow[it] = both[C:]
            else:
                tm[it] = tm[it] + qpow[it] + _dot(tm[it].astype(BF), bdq)

    awr, u0, rkv = {}, {}, {}
    for it in items:
        tb = tm[it].astype(BF)
        aw = at[it].astype(F32) + _dot(tb, bd4(at[it]))
        awr[it] = jnp.concatenate([aw.astype(BF), rt[it]], axis=0)
        kv = _dot(a_akrk[it], bdv[it])
        u0[it] = kv[:C] + _dot(tb, bd4(kv[:C].astype(BF)))
        rkv[it] = kv[C:]

    state = [s_ref[q] for q in range(N_QUADS)]
    ys = {}
    for c in range(n_chunks):
        for q in range(N_QUADS):
            it = (c, q)
            us = _dot_nt(awr[it], state[q].astype(BF))
            ub = (u0[it] + us[:C]).astype(BF)
            ys[it] = us[C:] + rkv[it] + _dot(a_rb[it], bd4(ub))
            uv = jnp.concatenate([ub, vv_ref[q, rows_of(c), :]], axis=0)
            bk = jnp.concatenate([bh_ref[q, rows_of(c), :], kh_ref[q, rows_of(c), :]], axis=0)
            wrow = (c % chunks_per_front_tile) * N_QUADS + q
            wc = wc_ref[c // chunks_per_front_tile, wrow:wrow + 1, :]
            state[q] = state[q] * wc + jnp.where(diag_blocks, _dot_tn(uv, bk), 0.0)
    for q in range(N_QUADS):
        s_ref[q] = state[q]

    g256 = g256_ref[...]
    y = jnp.concatenate(
        [jnp.concatenate([ys[(c, q)] for c in range(n_chunks)], axis=0) for q in range(N_QUADS)], axis=1)
    inv_n = 1.0 / HEAD_DIM
    d = y - _headsum(y, g256) * inv_n
    var = _headsum(d * d, g256) * inv_n
    yn = d * lax.rsqrt(var + GN_EPS) * gnw_ref[...] + gnb_ref[...]
    ya = _dot((yn * g_ref[...].astype(F32) + bg_ref[...].astype(F32)).astype(BF), wor_ref[...])
    mix_in = ga_ref[...].astype(F32) * ya + zb_ref[...].astype(F32)
    mix = _dot(mix_in.astype(BF), wout_ref[...])
    h = _layer_norm(ALPHA * x_ref[...] + mix, ln1w_ref[...], ln1b_ref[...])
    h_ref[...] = h
    hb = pltpu.bitcast(h.astype(BF).astype(F32), jnp.uint32)
    half = D_MODEL // 2
    hp_ref[...] = (hb[:, :half] >> 16) | (hb[:, half:] & jnp.uint32(0xFFFF0000))


def _scan(quads, wc, g, bg, ga, zb, x2, gnw, gnb, g256, wor, wout, ln1w, ln1b, seq):
    n_tok = x2.shape[0]
    n_tiles = n_tok // SCAN_TM
    row = lambda i: (i, 0)
    quad_spec = pl.BlockSpec((N_QUADS, SCAN_TM, QUAD), lambda i: (0, i, 0))
    full_spec = pl.BlockSpec((SCAN_TM, D_MODEL), row)
    n_wc = wc.shape[1]
    consts = (gnw, gnb, g256, wor, wout, ln1w, ln1b)
    return pl.pallas_call(
        functools.partial(_scan_kernel, tiles_per_seq=seq // SCAN_TM),
        grid=(n_tiles,),
        in_specs=[quad_spec] * 7 + [pl.BlockSpec((SCAN_TM // TM, n_wc, QUAD), lambda i: (i, 0, 0))]
        + [full_spec] * 5 + [_const_spec(c.shape) for c in consts],
        out_specs=[full_spec, pl.BlockSpec((SCAN_TM, D_MODEL // 2), row)],
        out_shape=[jax.ShapeDtypeStruct((n_tok, D_MODEL), F32),
                   jax.ShapeDtypeStruct((n_tok, D_MODEL // 2), jnp.uint32)],
        scratch_shapes=[pltpu.VMEM((N_QUADS, QUAD, QUAD), F32)],
        compiler_params=pltpu.CompilerParams(
            dimension_semantics=("arbitrary",), vmem_limit_bytes=V7X_VMEM_LIMIT_BYTES),
        name="scan",
    )(*quads, wc, g, bg, ga, zb, x2, *consts)


def _first_max(x, idx, n):
    m = jnp.max(x, axis=0, keepdims=True)
    first = jnp.min(jnp.where(x == m, idx, n), axis=0, keepdims=True)
    return m, first


def _route_kernel(h_ref, wrh_ref, wrl_ref, bias_ref, ustrict_ref, e_ref, gate_ref, rank_ref, cnt_ref,
                  carry_ref):
    @pl.when(pl.program_id(0) == 0)
    def _():
        carry_ref[...] = jnp.zeros_like(carry_ref)

    hh, hl = _split2(h_ref[...])
    wrh = wrh_ref[...]
    logits = _dot_nt(wrh, hh) + _dot_nt(wrh, hl) + _dot_nt(wrl_ref[...], hh)
    scores = jax.nn.sigmoid(logits)
    choice = scores + bias_ref[...]
    n_tok = choice.shape[1]
    per = N_EXPERTS // N_EXPERT_GROUPS
    eidx = lax.broadcasted_iota(jnp.int32, (N_EXPERTS, 1), 0)
    gidx = lax.broadcasted_iota(jnp.int32, (N_EXPERT_GROUPS, 1), 0)
    lidx = lax.broadcasted_iota(jnp.int32, (per, 1), 0)
    neg = -jnp.inf

    gsc = []
    for gi in range(N_EXPERT_GROUPS):
        blk = choice[per * gi:per * (gi + 1), :]
        m1, i1 = _first_max(blk, lidx, per)
        m2 = jnp.max(jnp.where(lidx == i1, neg, blk), axis=0, keepdims=True)
        gsc.append(m1 + m2)
    gscore = jnp.concatenate(gsc, axis=0)
    gsel = jnp.zeros(gscore.shape, jnp.bool_)
    for _ in range(TOPK_GROUPS):
        _, gi1 = _first_max(gscore, gidx, N_EXPERT_GROUPS)
        hit = gidx == gi1
        gsel = gsel | hit
        gscore = jnp.where(hit, neg, gscore)
    emask = jnp.concatenate(
        [jnp.broadcast_to(gsel[gi:gi + 1, :], (per, n_tok)) for gi in range(N_EXPERT_GROUPS)], axis=0)
    cand = jnp.where(emask, choice, neg)
    es, gs, hits = [], [], []
    for _ in range(TOP_K):
        _, e1 = _first_max(cand, eidx, N_EXPERTS)
        hit = eidx == e1
        es.append(e1)
        hits.append(hit)
        gs.append(jnp.sum(jnp.where(hit, scores, 0.0), axis=0, keepdims=True))
        cand = jnp.where(hit, neg, cand)
    gates = jnp.concatenate(gs, axis=0)
    gates = gates / jnp.sum(gates, axis=0, keepdims=True) * ROUTED_SCALE
    e_ref[...] = jnp.concatenate(es, axis=0)
    gate_ref[...] = gates

    chosen = functools.reduce(jnp.logical_or, hits)
    sel = jnp.where(chosen, 1.0, 0.0)
    before = _dot(sel.astype(BF), ustrict_ref[...]) + carry_ref[:, 0:1]
    rank_ref[...] = jnp.concatenate(
        [jnp.sum(jnp.where(hit, before, 0.0), axis=0, keepdims=True) for hit in hits],
        axis=0).astype(jnp.int32)
    total = before[:, n_tok - 1:n_tok] + sel[:, n_tok - 1:n_tok]
    carry_ref[...] = jnp.broadcast_to(total, carry_ref.shape)
    cnt_ref[...] = jnp.broadcast_to(total, cnt_ref.shape).astype(jnp.int32)


def _route(h, wrh, wrl, bias_col, ustrict):
    n_tok = h.shape[0]
    col = lambda i: (0, i)
    tok_spec = pl.BlockSpec((TOP_K, ROUTE_TM), col)
    return pl.pallas_call(
        _route_kernel,
        grid=(n_tok // ROUTE_TM,),
        in_specs=[pl.BlockSpec((ROUTE_TM, D_MODEL), lambda i: (i, 0)),
                  _const_spec(wrh.shape), _const_spec(wrl.shape), _const_spec(bias_col.shape),
                  _const_spec(ustrict.shape)],
        out_specs=[tok_spec, tok_spec, tok_spec, pl.BlockSpec((N_EXPERTS, LANES), lambda i: (0, 0))],
        out_shape=[jax.ShapeDtypeStruct((TOP_K, n_tok), jnp.int32),
                   jax.ShapeDtypeStruct((TOP_K, n_tok), F32),
                   jax.ShapeDtypeStruct((TOP_K, n_tok), jnp.int32),
                   jax.ShapeDtypeStruct((N_EXPERTS, LANES), jnp.int32)],
        scratch_shapes=[pltpu.VMEM((N_EXPERTS, LANES), F32)],
        compiler_params=pltpu.CompilerParams(dimension_semantics=("arbitrary",)),
        name="route",
    )(h, wrh, wrl, bias_col, ustrict)


def _rows_kernel(e_ref, rank_ref, base_ref, pos_ref):
    eidx = lax.broadcasted_iota(jnp.int32, (N_EXPERTS, 1), 0)
    base = base_ref[...]
    pos_ref[...] = rank_ref[...] + jnp.concatenate(
        [jnp.sum(jnp.where(eidx == e_ref[kk:kk + 1, :], base, 0), axis=0, keepdims=True)
         for kk in range(TOP_K)], axis=0)


def _rows(top_e, rank, base_col):
    n_tok = top_e.shape[1]
    tok_spec = pl.BlockSpec((TOP_K, ROUTE_TM), lambda i: (0, i))
    return pl.pallas_call(
        _rows_kernel,
        grid=(n_tok // ROUTE_TM,),
        in_specs=[tok_spec, tok_spec, _const_spec(base_col.shape)],
        out_specs=tok_spec,
        out_shape=jax.ShapeDtypeStruct((TOP_K, n_tok), jnp.int32),
        compiler_params=pltpu.CompilerParams(dimension_semantics=("arbitrary",)),
        name="rows",
    )(top_e, rank, base_col)


SC_WINDOW = 128


def _sc_mesh():
    return plsc.VectorSubcoreMesh(core_axis_name="c", subcore_axis_name="s")


def _sc_workers():
    info = plsc.get_sparse_core_info()
    return info.num_cores, info.num_cores * info.num_subcores


def _sc_scatter_rows(src, pos, n_rows):
    n_tok, width = src.shape
    n_k = pos.shape[0]
    n_cores, n_workers = _sc_workers()
    n_win = n_tok // (n_workers * SC_WINDOW)

    @functools.partial(
        pl.kernel, out_type=jax.ShapeDtypeStruct((n_rows, width), src.dtype), mesh=_sc_mesh(),
        scratch_types=[pltpu.VMEM((n_k * n_win, SC_WINDOW), jnp.int32),
                       pltpu.VMEM((SC_WINDOW, width), src.dtype),
                       pltpu.SemaphoreType.DMA])
    def scatter(src_hbm, pos_hbm, out_hbm, idx_v, rows_v, sem):
        wid = lax.axis_index("s") * n_cores + lax.axis_index("c")
        pltpu.sync_copy(pos_hbm.at[wid], idx_v)

        @pl.loop(0, n_win)
        def _(j):
            pltpu.sync_copy(src_hbm.at[pl.ds((wid * n_win + j) * SC_WINDOW, SC_WINDOW)], rows_v)
            copies = [pltpu.async_copy(rows_v, out_hbm.at[idx_v.at[kk * n_win + j]], sem)
                      for kk in range(n_k)]
            for cp in copies:
                cp.wait()

    pos4 = pos.reshape(n_k, n_workers, n_win, SC_WINDOW).transpose(1, 0, 2, 3)
    return scatter(src, pos4.reshape(n_workers, n_k * n_win, SC_WINDOW))


def _sc_gather_rows(src, idx):
    n_idx = idx.shape[0]
    width = src.shape[1]
    n_cores, n_workers = _sc_workers()
    n_win = n_idx // (n_workers * SC_WINDOW)

    @functools.partial(
        pl.kernel, out_type=jax.ShapeDtypeStruct((n_idx, width), src.dtype), mesh=_sc_mesh(),
        scratch_types=[pltpu.VMEM((n_win, SC_WINDOW), jnp.int32),
                       pltpu.VMEM((SC_WINDOW, width), src.dtype),
                       pltpu.SemaphoreType.DMA])
    def gather(src_hbm, idx_hbm, out_hbm, idx_v, rows_v, sem):
        wid = lax.axis_index("s") * n_cores + lax.axis_index("c")
        pltpu.sync_copy(idx_hbm.at[wid], idx_v)

        @pl.loop(0, n_win)
        def _(j):
            pltpu.async_copy(src_hbm.at[idx_v.at[j]], rows_v, sem).wait()
            pltpu.sync_copy(rows_v, out_hbm.at[pl.ds((wid * n_win + j) * SC_WINDOW, SC_WINDOW)])

    return gather(src, idx.reshape(n_workers, n_win, SC_WINDOW))


def _unpack_pair(words):
    lo = pltpu.bitcast(words << 16, F32).astype(BF)
    hi = pltpu.bitcast(words & jnp.uint32(0xFFFF0000), F32).astype(BF)
    return lo, hi


def _pack_pair(val):
    bits = pltpu.bitcast(val.astype(BF).astype(F32), jnp.uint32)
    half = val.shape[1] // 2
    return (bits[:, :half] >> 16) | (bits[:, half:] & jnp.uint32(0xFFFF0000))


FFN_SLOTS = 4
FFN_AHEAD = FFN_SLOTS - 1


def _ffn_kernel(base_ref, cnt_ref, xs_ref, wgu_ref, wd_ref, out_ref,
                xbuf, obuf, wgu16_ref, wd16_ref, sem_in, sem_out, pend_ref, *, n_blocks):
    e = pl.program_id(0)
    n_rows_e = cnt_ref[e]
    n_blk = (n_rows_e + FFN_ROWS - 1) // FFN_ROWS
    first_blk = base_ref[e] // FFN_ROWS
    last_cnt = cnt_ref[N_EXPERTS - 1]
    used = base_ref[N_EXPERTS - 1] // FFN_ROWS + (last_cnt + FFN_ROWS - 1) // FFN_ROWS
    half = D_MODEL // 2

    def in_copy(blk, slot):
        row0 = pl.multiple_of(blk * FFN_ROWS, FFN_ROWS)
        return pltpu.make_async_copy(xs_ref.at[pl.ds(row0, FFN_ROWS), :], xbuf.at[slot], sem_in.at[slot])

    def out_copy(blk, slot):
        row0 = pl.multiple_of(blk * FFN_ROWS, FFN_ROWS)
        return pltpu.make_async_copy(obuf.at[slot], out_ref.at[pl.ds(row0, FFN_ROWS), :], sem_out.at[slot])

    @pl.when(e == 0)
    def _():
        for slot in range(FFN_SLOTS):
            pend_ref[slot] = 0
        for blk in range(FFN_AHEAD):
            @pl.when(blk < used)
            def _():
                in_copy(blk, blk).start()

    @pl.when(n_blk > 0)
    def _():
        wgu16_ref[...] = wgu_ref[0].astype(BF)
        wd16_ref[...] = wd_ref[0].astype(BF)

    row = lax.broadcasted_iota(jnp.int32, (FFN_ROWS, 1), 0)

    def block(j, carry):
        blk = first_blk + j
        slot = blk % FFN_SLOTS
        in_copy(blk, slot).wait()

        @pl.when(blk + FFN_AHEAD < used)
        def _():
            in_copy(blk + FFN_AHEAD, (blk + FFN_AHEAD) % FFN_SLOTS).start()

        words = jnp.where(row < n_rows_e - j * FFN_ROWS, xbuf[slot], jnp.uint32(0))
        lo, hi = _unpack_pair(words)
        gu = _dot(lo, wgu16_ref[0:half, :]) + _dot(hi, wgu16_ref[half:, :])
        act = (jax.nn.silu(gu[:, :EXPERT_FF]) * gu[:, EXPERT_FF:]).astype(BF)
        packed = _pack_pair(_dot(act, wd16_ref[...]))

        @pl.when(pend_ref[slot] == 1)
        def _():
            out_copy(blk, slot).wait()

        obuf[slot] = packed
        out_copy(blk, slot).start()
        pend_ref[slot] = 1
        return carry

    lax.fori_loop(0, n_blk, block, 0)

    @pl.when(e == N_EXPERTS - 1)
    def _():
        for slot in range(FFN_SLOTS):
            @pl.when(pend_ref[slot] == 1)
            def _():
                out_copy(0, slot).wait()
        obuf[0] = jnp.zeros((FFN_ROWS, half), jnp.uint32)

        def fill(blk, carry):
            cp = out_copy(blk, 0)
            cp.start()
            cp.wait()
            return carry

        lax.fori_loop(used, n_blocks, fill, 0)


def _ffn(base, counts, xs, wgu, wd):
    n_rows = xs.shape[0]
    half = D_MODEL // 2
    grid_spec = pltpu.PrefetchScalarGridSpec(
        num_scalar_prefetch=2,
        grid=(N_EXPERTS,),
        in_specs=[pl.BlockSpec(memory_space=pl.ANY),
                  pl.BlockSpec((1, D_MODEL, 2 * EXPERT_FF), lambda e, base, cnt: (e, 0, 0)),
                  pl.BlockSpec((1, EXPERT_FF, D_MODEL), lambda e, base, cnt: (e, 0, 0))],
        out_specs=pl.BlockSpec(memory_space=pl.ANY),
        scratch_shapes=[pltpu.VMEM((FFN_SLOTS, FFN_ROWS, half), jnp.uint32),
                        pltpu.VMEM((FFN_SLOTS, FFN_ROWS, half), jnp.uint32),
                        pltpu.VMEM((D_MODEL, 2 * EXPERT_FF), BF),
                        pltpu.VMEM((EXPERT_FF, D_MODEL), BF),
                        pltpu.SemaphoreType.DMA((FFN_SLOTS,)),
                        pltpu.SemaphoreType.DMA((FFN_SLOTS,)),
                        pltpu.SMEM((FFN_SLOTS,), jnp.int32)],
    )
    return pl.pallas_call(
        functools.partial(_ffn_kernel, n_blocks=n_rows // FFN_ROWS),
        grid_spec=grid_spec,
        out_shape=jax.ShapeDtypeStruct((n_rows, half), jnp.uint32),
        compiler_params=pltpu.CompilerParams(dimension_semantics=("arbitrary",)),
        name="ffn",
    )(base, counts, xs, wgu, wd)


def _final_kernel(rows_ref, gate_ref, h_ref, wsgu_ref, wsd_ref, ln2w_ref, ln2b_ref, o_ref):
    h = h_ref[...]
    hb = h.astype(BF)
    gu = _dot(hb, wsgu_ref[...])
    act = (jax.nn.silu(gu[:, :SHARED_FF]) * gu[:, SHARED_FF:]).astype(BF)
    shared = _dot(act, wsd_ref[...])

    half = D_MODEL // 2
    lo_acc = jnp.zeros((FINAL_TM, half), F32)
    hi_acc = jnp.zeros((FINAL_TM, half), F32)
    gates = gate_ref[...]
    for kk in range(TOP_K):
        words = rows_ref[kk]
        gk = gates[:, kk:kk + 1]
        lo_acc = lo_acc + gk * pltpu.bitcast(words << 16, F32)
        hi_acc = hi_acc + gk * pltpu.bitcast(words & jnp.uint32(0xFFFF0000), F32)
    routed = jnp.concatenate([lo_acc, hi_acc], axis=1)
    o_ref[...] = _layer_norm(ALPHA * h + routed + shared, ln2w_ref[...], ln2b_ref[...])


def _final(rows, gates_t, h, wsgu, wsd, ln2w, ln2b):
    n_tok = h.shape[0]
    n_tiles = n_tok // FINAL_TM
    half = D_MODEL // 2
    row = lambda i: (i, 0)
    consts = (wsgu, wsd, ln2w, ln2b)
    return pl.pallas_call(
        _final_kernel,
        grid=(n_tiles,),
        in_specs=[pl.BlockSpec((TOP_K, FINAL_TM, half), lambda i: (0, i, 0)),
                  pl.BlockSpec((FINAL_TM, TOP_K), row),
                  pl.BlockSpec((FINAL_TM, D_MODEL), row)] + [_const_spec(c.shape) for c in consts],
        out_specs=pl.BlockSpec((FINAL_TM, D_MODEL), row),
        out_shape=jax.ShapeDtypeStruct((n_tok, D_MODEL), F32),
        compiler_params=pltpu.CompilerParams(dimension_semantics=("arbitrary",)),
        name="final",
    )(rows, gates_t, h, *consts)


def _np_consts():
    idx = np.arange(QUAD)
    g256 = (idx[:, None] // HEAD_DIM == idx[None, :] // HEAD_DIM).astype(np.float32)
    t = np.arange(TM)
    same = t[:, None] // SCAN_CHUNK == t[None, :] // SCAN_CHUNK
    ltri = (same & (t[None, :] <= t[:, None])).astype(np.float32)
    return jnp.asarray(g256, BF), jnp.asarray(ltri, BF)


def _block_plan(counts, n_assign):
    padded = (counts + FFN_ROWS - 1) // FFN_ROWS * FFN_ROWS
    base = jnp.cumsum(padded) - padded
    n_blocks = (n_assign + N_EXPERTS * (FFN_ROWS - 1) + FFN_ROWS - 1) // FFN_ROWS
    return base.astype(jnp.int32), n_blocks * FFN_ROWS


def _layer(x2, seq, w_in, shift_mu, w0, decay_lora, a0, iclr_lora, gate_lora, k_k, k_a, r_k, gn_w, gn_b,
           lnw, lnb, spatial_w, spatial_b, w_o_rwkv, w_o_gmlp, w_out, ln1_w, ln1_b,
           w_router, router_bias, w_exp_gate_up, w_exp_down, w_sh_gate_up, w_sh_down, ln2_w, ln2_b):
    n_tok = x2.shape[0]
    c_rwkv = C_RKV + C_LORA
    pad = C_LORA_PAD - C_LORA
    win = jnp.concatenate([w_in[:, :c_rwkv], jnp.zeros((D_MODEL, pad), F32), w_in[:, c_rwkv:]], axis=1).astype(BF)
    mu = jnp.pad(shift_mu, (0, pad)).reshape(1, C_RWKV_PAD)
    row1 = lambda t: t.reshape(1, -1)
    w1 = jnp.zeros((LANES, 2 * D_MODEL), F32)
    w1 = w1.at[:DECAY_LORA, :D_MODEL].set(decay_lora).at[DECAY_LORA:, D_MODEL:].set(iclr_lora).astype(BF)
    w2 = jnp.zeros((C_LORA_PAD - LANES, D_MODEL), F32).at[:GATE_LORA].set(gate_lora).astype(BF)
    g256, ltri = _np_consts()
    sb_map = jnp.repeat(spatial_b.T, D_MODEL // GMLP_GROUPS, axis=1)

    outs = _front(x2, win, mu, row1(w0), row1(a0), w1, w2, row1(k_k), row1(k_a), row1(r_k),
                  g256, ltri, row1(lnw), row1(lnb), spatial_w, sb_map, w_o_gmlp.astype(BF), seq)
    quads, wc, (g, bg, ga, zb) = outs[:7], outs[7], outs[8:]
    h1, h1p = _scan(quads, wc, g, bg, ga, zb, x2, row1(gn_w), row1(gn_b), g256,
                    w_o_rwkv.astype(BF), w_out.astype(BF), row1(ln1_w), row1(ln1_b), seq)

    wrt = w_router.T
    wrh = wrt.astype(BF)
    wrl = (wrt - wrh.astype(F32)).astype(BF)
    tt = np.arange(ROUTE_TM)
    ustrict = jnp.asarray((tt[:, None] < tt[None, :]).astype(np.float32), BF)
    top_e, gates, rank, counts = _route(h1, wrh, wrl, router_bias.reshape(N_EXPERTS, 1), ustrict)
    counts = counts[:, 0]
    base, n_rows = _block_plan(counts, n_tok * TOP_K)
    pos = _rows(top_e, rank, base.reshape(N_EXPERTS, 1))
    xs = _sc_scatter_rows(h1p, pos, n_rows)
    rows = _ffn(base, counts, xs, w_exp_gate_up, w_exp_down)
    picked = _sc_gather_rows(rows, pos.reshape(-1)).reshape(TOP_K, n_tok, D_MODEL // 2)
    return _final(picked, gates.T, h1, w_sh_gate_up.astype(BF), w_sh_down.astype(BF),
                  row1(ln2_w), row1(ln2_b))


def kernel(x, w_in, shift_mu, rwkv_w0, rwkv_decay_lora, rwkv_a0, rwkv_iclr_lora, rwkv_gate_lora, rwkv_k_k, rwkv_k_a, rwkv_r_k, rwkv_gn_w, rwkv_gn_b, gmlp_ln_w, gmlp_ln_b, gmlp_spatial_w, gmlp_spatial_b, w_o_rwkv, w_o_gmlp, w_out, ln1_w, ln1_b, w_router, router_bias, w_exp_gate_up, w_exp_down, w_sh_gate_up, w_sh_down, ln2_w, ln2_b):
    bsz, seq, d = x.shape
    h = x.reshape(bsz * seq, d)
    for l in range(DEPTH):
        h = _layer(h, seq, w_in[l], shift_mu[l], rwkv_w0[l], rwkv_decay_lora[l], rwkv_a0[l],
                   rwkv_iclr_lora[l], rwkv_gate_lora[l], rwkv_k_k[l], rwkv_k_a[l], rwkv_r_k[l],
                   rwkv_gn_w[l], rwkv_gn_b[l], gmlp_ln_w[l], gmlp_ln_b[l], gmlp_spatial_w[l],
                   gmlp_spatial_b[l], w_o_rwkv[l], w_o_gmlp[l], w_out[l], ln1_w[l], ln1_b[l],
                   w_router[l], router_bias[l], w_exp_gate_up[l], w_exp_down[l], w_sh_gate_up[l],
                   w_sh_down[l], ln2_w[l], ln2_b[l])
    return h.reshape(bsz, seq, d)
```

```python
import functools

import numpy as np
import jax
import jax.numpy as jnp
from jax import lax
from jax.experimental import pallas as pl
from jax.experimental.pallas import tpu as pltpu
from jax.experimental.pallas import tpu_sc as plsc

F32 = jnp.float32
BF = jnp.bfloat16

D_MODEL = 1024
HEAD_DIM = 64
N_HEADS = 16
DECAY_LORA = 64
ICLR_LORA = 64
GATE_LORA = 160
GN_EPS = 64e-5
GMLP_GROUPS = 8
GMLP_CHUNK = 128
N_EXPERTS = 256
TOP_K = 8
N_EXPERT_GROUPS = 8
TOPK_GROUPS = 4
EXPERT_FF = 256
SHARED_FF = 256
ROUTED_SCALE = 2.5
LN_EPS = 1e-5
DEPTH = 1
ALPHA = (2 * DEPTH) ** 0.25

LANES = 128
V7X_VMEM_LIMIT_BYTES = 56 * 1024 * 1024

C_RKV = 3 * D_MODEL
C_LORA = DECAY_LORA + ICLR_LORA + GATE_LORA
C_LORA_PAD = 384
C_RWKV_PAD = C_RKV + C_LORA_PAD
C_GMLP = 2 * D_MODEL
C_GATES = 2 * D_MODEL
C_IN_PAD = C_RWKV_PAD + C_GMLP + C_GATES

SCAN_CHUNK = 64
QUAD = 4 * HEAD_DIM
N_QUADS = D_MODEL // QUAD
TM = 512
CUM_TILE = 256
SCAN_TM = 256
ROUTE_TM = 256
FFN_ROWS = 256
FINAL_TM = 256


def _dot(a, b):
    return jnp.dot(a, b, preferred_element_type=F32)


def _dot_nt(a, b):
    return lax.dot_general(a, b, (((1,), (1,)), ((), ())), preferred_element_type=F32)


def _dot_tn(a, b):
    return lax.dot_general(a, b, (((0,), (0,)), ((), ())), preferred_element_type=F32)


def _split2(z):
    hi = z.astype(BF)
    lo = (z - hi.astype(F32)).astype(BF)
    return hi, lo


def _headsum(z, g256, passes=2):
    outs = []
    for j in range(N_QUADS):
        zj = z[:, QUAD * j:QUAD * (j + 1)]
        if passes == 1:
            outs.append(_dot(zj.astype(BF), g256))
        else:
            hi, lo = _split2(zj)
            outs.append(_dot(hi, g256) + _dot(lo, g256))
    return jnp.concatenate(outs, axis=1)


def _layer_norm(v, w, b):
    mu = jnp.mean(v, axis=-1, keepdims=True)
    d = v - mu
    var = jnp.mean(d * d, axis=-1, keepdims=True)
    return d * lax.rsqrt(var + LN_EPS) * w + b


def _const_spec(shape):
    nd = len(shape)
    return pl.BlockSpec(shape, lambda *_: (0,) * nd, pipeline_mode=pl.Buffered(1))


def _front_kernel(x_ref, win_ref, mu_ref, w0_ref, a0_ref, w1_ref, w2_ref, kk_ref, ka_ref, rk_ref,
                  g256_ref, ltri_ref, lnw_ref, lnb_ref, ws_ref, sb_ref, wog_ref,
                  rt_ref, at_ref, bt_ref, kt_ref, vv_ref, bh_ref, kh_ref, wc_ref,
                  g_ref, bg_ref, ga_ref, zb_ref, prev_ref, xb_ref, *, tiles_per_seq):
    i = pl.program_id(0)
    first = (i % tiles_per_seq) == 0
    xb_ref[...] = x_ref[...].astype(BF)
    is_row0 = lax.broadcasted_iota(jnp.int32, (TM, 1), 0) == 0
    W = QUAD

    def proj(lo, width=W):
        return _dot(xb_ref[...], win_ref[:, lo:lo + width])

    def shift_mix(lo, width=W):
        p = proj(lo, width)
        prev = jnp.where(first, 0.0, prev_ref[:, lo:lo + width])
        sh = jnp.where(is_row0, prev, pltpu.roll(p, 1, 0))
        prev_ref[:, lo:lo + width] = p[TM - 1:TM, :]
        return p + mu_ref[:, lo:lo + width] * (sh - p)

    lora = shift_mix(C_RKV, C_LORA_PAD)
    lane = lax.broadcasted_iota(jnp.int32, (1, LANES), 1)
    act1 = jnp.where(lane < DECAY_LORA, jnp.tanh(lora[:, :LANES]), lora[:, :LANES]).astype(BF)
    act2 = jax.nn.sigmoid(lora[:, LANES:]).astype(BF)
    g256 = g256_ref[...]
    for q in range(N_QUADS):
        c0 = W * q
        cols = slice(c0, c0 + W)
        r = shift_mix(c0)
        k = shift_mix(D_MODEL + c0)
        v = shift_mix(2 * D_MODEL + c0)
        w = -jax.nn.softplus(-(w0_ref[:, cols] + _dot(act1, w1_ref[:, cols]))) - 0.5
        logd = -jnp.exp(w)
        a = jax.nn.sigmoid(a0_ref[:, cols] + _dot(act1, w1_ref[:, D_MODEL + c0:D_MODEL + c0 + W]))
        g = _dot(act2, w2_ref[:, cols])

        kkr = k * kk_ref[:, cols]
        kk = kkr * lax.rsqrt(jnp.maximum(_dot((kkr * kkr).astype(BF), g256), 1e-24))
        kmod = k * (1.0 + (a - 1.0) * ka_ref[:, cols])
        bonus = _dot((r * kmod * rk_ref[:, cols]).astype(BF), g256) * v
        kka = kk * a

        parts = _split2(logd)
        cum = jnp.concatenate(
            [sum(_dot(ltri_ref[...], part[CUM_TILE * hh:CUM_TILE * (hh + 1)]) for part in parts)
             for hh in range(TM // CUM_TILE)], axis=0)
        e_inv = jnp.exp(-cum)
        bt = kka * e_inv
        kt = kmod * e_inv
        ends = []
        for c in range(TM // SCAN_CHUNK):
            last = SCAN_CHUNK * (c + 1) - 1
            ewc_row = jnp.exp(cum[last:last + 1, :])
            wc_ref[c, q:q + 1, :] = ewc_row
            ends.append(jnp.broadcast_to(ewc_row, (SCAN_CHUNK, W)))
        ewc = jnp.concatenate(ends, axis=0)
        rt_ref[q] = (r * jnp.exp(cum)).astype(BF)
        at_ref[q] = (-kk * jnp.exp(cum - logd)).astype(BF)
        bt_ref[q] = bt.astype(BF)
        kt_ref[q] = kt.astype(BF)
        vv_ref[q] = v.astype(BF)
        bh_ref[q] = (bt * ewc).astype(BF)
        kh_ref[q] = (kt * ewc).astype(BF)
        g_ref[:, cols] = g.astype(BF)
        bg_ref[:, cols] = (bonus * g).astype(BF)

    gelu = lambda t: 0.5 * t * (1.0 + lax.erf(t * np.float32(np.sqrt(0.5))))
    u0 = C_RWKV_PAD
    zv = [gelu(proj(u0 + D_MODEL + W * j)) for j in range(N_QUADS)]
    mean = sum(jnp.sum(z, axis=-1, keepdims=True) for z in zv) * (1.0 / D_MODEL)
    dv = [z - mean for z in zv]
    var = sum(jnp.sum(d * d, axis=-1, keepdims=True) for d in dv) * (1.0 / D_MODEL)
    rstd = lax.rsqrt(var + LN_EPS)
    trow = lax.broadcasted_iota(jnp.int32, (GMLP_CHUNK, GMLP_CHUNK), 0)
    tcol = lax.broadcasted_iota(jnp.int32, (GMLP_CHUNK, GMLP_CHUNK), 1)
    causal = tcol <= trow
    n_time = TM // GMLP_CHUNK
    yb = jnp.zeros((TM, D_MODEL), F32)
    for j in range(N_QUADS):
        cols = slice(W * j, W * (j + 1))
        vn = (dv[j] * rstd * lnw_ref[:, cols] + lnb_ref[:, cols]).astype(BF)
        svs = []
        for gl in range(W // LANES):
            wsg = jnp.where(causal, ws_ref[(W // LANES) * j + gl], 0.0).astype(BF)
            svs.append(jnp.concatenate(
                [_dot(wsg, vn[GMLP_CHUNK * cc:GMLP_CHUNK * (cc + 1), LANES * gl:LANES * (gl + 1)])
                 for cc in range(n_time)], axis=0))
        sb = sb_ref[:, cols]
        sv = jnp.concatenate(svs, axis=1) + jnp.concatenate([sb] * n_time, axis=0)
        u = gelu(proj(u0 + W * j))
        yb = yb + _dot((u * sv).astype(BF), wog_ref[W * j:W * (j + 1), :])

    g0 = C_RWKV_PAD + C_GMLP
    for j in range(N_QUADS):
        cols = slice(W * j, W * (j + 1))
        ga_ref[:, cols] = jax.nn.sigmoid(proj(g0 + W * j)).astype(BF)
        zb_ref[:, cols] = (jax.nn.sigmoid(proj(g0 + D_MODEL + W * j)) * yb[:, cols]).astype(BF)


def _front(x2, win, mu, w0, a0, w1, w2, k_k, k_a, r_k, g256, ltri, lnw, lnb, ws, sb, wog, seq):
    n_tok = x2.shape[0]
    n_tiles = n_tok // TM
    row = lambda i: (i, 0)
    quad_spec = pl.BlockSpec((N_QUADS, TM, QUAD), lambda i: (0, i, 0))
    quad_shape = jax.ShapeDtypeStruct((N_QUADS, n_tok, QUAD), BF)
    full_spec = pl.BlockSpec((TM, D_MODEL), row)
    full_shape = jax.ShapeDtypeStruct((n_tok, D_MODEL), BF)
    wc_spec = pl.BlockSpec((TM // SCAN_CHUNK, N_QUADS, QUAD), lambda i: (i, 0, 0))
    wc_shape = jax.ShapeDtypeStruct((n_tok // SCAN_CHUNK, N_QUADS, QUAD), F32)
    consts = (win, mu, w0, a0, w1, w2, k_k, k_a, r_k, g256, ltri, lnw, lnb, ws, sb, wog)
    return pl.pallas_call(
        functools.partial(_front_kernel, tiles_per_seq=seq // TM),
        grid=(n_tiles,),
        in_specs=[pl.BlockSpec((TM, D_MODEL), row)] + [_const_spec(c.shape) for c in consts],
        out_specs=[quad_spec] * 7 + [wc_spec] + [full_spec] * 4,
        out_shape=[quad_shape] * 7 + [wc_shape] + [full_shape] * 4,
        scratch_shapes=[pltpu.VMEM((1, C_RWKV_PAD), F32), pltpu.VMEM((TM, D_MODEL), BF)],
        compiler_params=pltpu.CompilerParams(
            dimension_semantics=("arbitrary",), vmem_limit_bytes=V7X_VMEM_LIMIT_BYTES),
        name="front",
    )(x2, *consts)


def _scan_kernel(rt_ref, at_ref, bt_ref, kt_ref, vv_ref, bh_ref, kh_ref, wc_ref,
                 g_ref, bg_ref, ga_ref, zb_ref, x_ref, gnw_ref, gnb_ref, g256_ref,
                 wor_ref, wout_ref, ln1w_ref, ln1b_ref,
                 h_ref, hp_ref, s_ref, *, tiles_per_seq):
    i = pl.program_id(0)

    @pl.when((i % tiles_per_seq) == 0)
    def _():
        s_ref[...] = jnp.zeros_like(s_ref)

    lane = lax.broadcasted_iota(jnp.int32, (1, QUAD), 1)
    lane_head = lane // HEAD_DIM
    lane_pos = lane % HEAD_DIM
    trow = lax.broadcasted_iota(jnp.int32, (SCAN_CHUNK, 1), 0)
    strict = lane_pos < trow
    incl = lane_pos <= trow
    row_head = lax.broadcasted_iota(jnp.int32, (QUAD, 1), 0) // HEAD_DIM
    diag_blocks = row_head == lane_head

    def bd4(m):
        zero = jnp.zeros_like(m)
        return jnp.concatenate([jnp.where(lane_head == j, m, zero) for j in range(4)], axis=0)

    n_chunks = SCAN_TM // SCAN_CHUNK
    items = [(c, q) for c in range(n_chunks) for q in range(N_QUADS)]
    rows_of = lambda c: slice(SCAN_CHUNK * c, SCAN_CHUNK * (c + 1))
    C = SCAN_CHUNK

    at, rt, bdv, a_ab, a_akrk, a_rb = {}, {}, {}, {}, {}, {}
    for it in items:
        c, q = it
        at[it] = at_ref[q, rows_of(c), :]
        rt[it] = rt_ref[q, rows_of(c), :]
        ra = jnp.concatenate([at[it], rt[it]], axis=0)
        gb = _dot_nt(ra, bd4(bt_ref[q, rows_of(c), :]))
        gk = _dot_nt(ra, bd4(kt_ref[q, rows_of(c), :]))
        a_ab[it] = jnp.where(strict, gb[:C], 0.0)
        a_rb[it] = jnp.where(incl, gb[C:], 0.0).astype(BF)
        a_akrk[it] = jnp.concatenate(
            [jnp.where(strict, gk[:C], 0.0), jnp.where(incl, gk[C:], 0.0)], axis=0).astype(BF)
        bdv[it] = bd4(vv_ref[q, rows_of(c), :])

    tm, qpow = {}, {}
    for it in items:
        qb = a_ab[it].astype(BF)
        tm[it] = a_ab[it]
        qpow[it] = _dot(qb, bd4(qb))
    for level in range(1, 6):
        for it in items:
            qb = qpow[it].astype(BF)
            bdq = bd4(qb)
            if level < 5:
                both = _dot(jnp.concatenate([tm[it].astype(BF), qb], axis=0), bdq)
                tm[it] = tm[it] + qpow[it] + both[:C]
                qpow[it] = both[C:]
            else:
                tm[it] = tm[it] + qpow[it] + _dot(tm[it].astype(BF), bdq)

    awr, u0, rkv = {}, {}, {}
    for it in items:
        tb = tm[it].astype(BF)
        aw = at[it].astype(F32) + _dot(tb, bd4(at[it]))
        awr[it] = jnp.concatenate([aw.astype(BF), rt[it]], axis=0)
        kv = _dot(a_akrk[it], bdv[it])
        u0[it] = kv[:C] + _dot(tb, bd4(kv[:C].astype(BF)))
        rkv[it] = kv[C:]

    state = [s_ref[q] for q in range(N_QUADS)]
    ys = {}
    for c in range(n_chunks):
        for q in range(N_QUADS):
            it = (c, q)
            us = _dot_nt(awr[it], state[q].astype(BF))
            ub = (u0[it] + us[:C]).astype(BF)
            ys[it] = us[C:] + rkv[it] + _dot(a_rb[it], bd4(ub))
            uv = jnp.concatenate([ub, vv_ref[q, rows_of(c), :]], axis=0)
            bk = jnp.concatenate([bh_ref[q, rows_of(c), :], kh_ref[q, rows_of(c), :]], axis=0)
            wc = wc_ref[c, q:q + 1, :]
            state[q] = state[q] * wc + jnp.where(diag_blocks, _dot_tn(uv, bk), 0.0)
    for q in range(N_QUADS):
        s_ref[q] = state[q]

    g256 = g256_ref[...]
    y = jnp.concatenate(
        [jnp.concatenate([ys[(c, q)] for c in range(n_chunks)], axis=0) for q in range(N_QUADS)], axis=1)
    inv_n = 1.0 / HEAD_DIM
    d = y - _headsum(y, g256) * inv_n
    var = _headsum(d * d, g256) * inv_n
    yn = d * lax.rsqrt(var + GN_EPS) * gnw_ref[...] + gnb_ref[...]
    ya = _dot((yn * g_ref[...].astype(F32) + bg_ref[...].astype(F32)).astype(BF), wor_ref[...])
    mix_in = ga_ref[...].astype(F32) * ya + zb_ref[...].astype(F32)
    mix = _dot(mix_in.astype(BF), wout_ref[...])
    h = _layer_norm(ALPHA * x_ref[...] + mix, ln1w_ref[...], ln1b_ref[...])
    h_ref[...] = h
    hb = pltpu.bitcast(h.astype(BF).astype(F32), jnp.uint32)
    half = D_MODEL // 2
    hp_ref[...] = (hb[:, :half] >> 16) | (hb[:, half:] & jnp.uint32(0xFFFF0000))


def _scan(quads, wc, g, bg, ga, zb, x2, gnw, gnb, g256, wor, wout, ln1w, ln1b, seq):
    n_tok = x2.shape[0]
    n_tiles = n_tok // SCAN_TM
    row = lambda i: (i, 0)
    quad_spec = pl.BlockSpec((N_QUADS, SCAN_TM, QUAD), lambda i: (0, i, 0))
    full_spec = pl.BlockSpec((SCAN_TM, D_MODEL), row)
    wc_spec = pl.BlockSpec((SCAN_TM // SCAN_CHUNK, N_QUADS, QUAD), lambda i: (i, 0, 0))
    consts = (gnw, gnb, g256, wor, wout, ln1w, ln1b)
    return pl.pallas_call(
        functools.partial(_scan_kernel, tiles_per_seq=seq // SCAN_TM),
        grid=(n_tiles,),
        in_specs=[quad_spec] * 7 + [wc_spec]
        + [full_spec] * 5 + [_const_spec(c.shape) for c in consts],
        out_specs=[full_spec, pl.BlockSpec((SCAN_TM, D_MODEL // 2), row)],
        out_shape=[jax.ShapeDtypeStruct((n_tok, D_MODEL), F32),
                   jax.ShapeDtypeStruct((n_tok, D_MODEL // 2), jnp.uint32)],
        scratch_shapes=[pltpu.VMEM((N_QUADS, QUAD, QUAD), F32)],
        compiler_params=pltpu.CompilerParams(
            dimension_semantics=("arbitrary",), vmem_limit_bytes=V7X_VMEM_LIMIT_BYTES),
        name="scan",
    )(*quads, wc, g, bg, ga, zb, x2, *consts)


def _first_max(x, idx, n):
    m = jnp.max(x, axis=0, keepdims=True)
    first = jnp.min(jnp.where(x == m, idx, n), axis=0, keepdims=True)
    return m, first


def _route_kernel(h_ref, wrh_ref, wrl_ref, bias_ref, ustrict_ref, e_ref, gate_ref, rank_ref, cnt_ref,
                  carry_ref):
    @pl.when(pl.program_id(0) == 0)
    def _():
        carry_ref[...] = jnp.zeros_like(carry_ref)

    hh, hl = _split2(h_ref[...])
    wrh = wrh_ref[...]
    logits = _dot_nt(wrh, hh) + _dot_nt(wrh, hl) + _dot_nt(wrl_ref[...], hh)
    scores = jax.nn.sigmoid(logits)
    choice = scores + bias_ref[...]
    n_tok = choice.shape[1]
    per = N_EXPERTS // N_EXPERT_GROUPS
    eidx = lax.broadcasted_iota(jnp.int32, (N_EXPERTS, 1), 0)
    gidx = lax.broadcasted_iota(jnp.int32, (N_EXPERT_GROUPS, 1), 0)
    lidx = lax.broadcasted_iota(jnp.int32, (per, 1), 0)
    neg = -jnp.inf

    gsc = []
    for gi in range(N_EXPERT_GROUPS):
        blk = choice[per * gi:per * (gi + 1), :]
        m1, i1 = _first_max(blk, lidx, per)
        m2 = jnp.max(jnp.where(lidx == i1, neg, blk), axis=0, keepdims=True)
        gsc.append(m1 + m2)
    gscore = jnp.concatenate(gsc, axis=0)
    gsel = jnp.zeros(gscore.shape, jnp.bool_)
    for _ in range(TOPK_GROUPS):
        _, gi1 = _first_max(gscore, gidx, N_EXPERT_GROUPS)
        hit = gidx == gi1
        gsel = gsel | hit
        gscore = jnp.where(hit, neg, gscore)
    emask = jnp.concatenate(
        [jnp.broadcast_to(gsel[gi:gi + 1, :], (per, n_tok)) for gi in range(N_EXPERT_GROUPS)], axis=0)
    cand = jnp.where(emask, choice, neg)
    es, gs, hits = [], [], []
    for _ in range(TOP_K):
        _, e1 = _first_max(cand, eidx, N_EXPERTS)
        hit = eidx == e1
        es.append(e1)
        hits.append(hit)
        gs.append(jnp.sum(jnp.where(hit, scores, 0.0), axis=0, keepdims=True))
        cand = jnp.where(hit, neg, cand)
    gates = jnp.concatenate(gs, axis=0)
    gates = gates / jnp.sum(gates, axis=0, keepdims=True) * ROUTED_SCALE
    e_ref[...] = jnp.concatenate(es, axis=0)
    gate_ref[...] = gates

    chosen = functools.reduce(jnp.logical_or, hits)
    sel = jnp.where(chosen, 1.0, 0.0)
    before = _dot(sel.astype(BF), ustrict_ref[...]) + carry_ref[:, 0:1]
    rank_ref[...] = jnp.concatenate(
        [jnp.sum(jnp.where(hit, before, 0.0), axis=0, keepdims=True) for hit in hits],
        axis=0).astype(jnp.int32)
    total = before[:, n_tok - 1:n_tok] + sel[:, n_tok - 1:n_tok]
    carry_ref[...] = jnp.broadcast_to(total, carry_ref.shape)
    cnt_ref[...] = jnp.broadcast_to(total, cnt_ref.shape).astype(jnp.int32)


def _route(h, wrh, wrl, bias_col, ustrict):
    n_tok = h.shape[0]
    col = lambda i: (0, i)
    tok_spec = pl.BlockSpec((TOP_K, ROUTE_TM), col)
    return pl.pallas_call(
        _route_kernel,
        grid=(n_tok // ROUTE_TM,),
        in_specs=[pl.BlockSpec((ROUTE_TM, D_MODEL), lambda i: (i, 0)),
                  _const_spec(wrh.shape), _const_spec(wrl.shape), _const_spec(bias_col.shape),
                  _const_spec(ustrict.shape)],
        out_specs=[tok_spec, tok_spec, tok_spec, pl.BlockSpec((N_EXPERTS, LANES), lambda i: (0, 0))],
        out_shape=[jax.ShapeDtypeStruct((TOP_K, n_tok), jnp.int32),
                   jax.ShapeDtypeStruct((TOP_K, n_tok), F32),
                   jax.ShapeDtypeStruct((TOP_K, n_tok), jnp.int32),
                   jax.ShapeDtypeStruct((N_EXPERTS, LANES), jnp.int32)],
        scratch_shapes=[pltpu.VMEM((N_EXPERTS, LANES), F32)],
        compiler_params=pltpu.CompilerParams(dimension_semantics=("arbitrary",)),
        name="route",
    )(h, wrh, wrl, bias_col, ustrict)


def _rows_kernel(e_ref, rank_ref, base_ref, pos_ref):
    eidx = lax.broadcasted_iota(jnp.int32, (N_EXPERTS, 1), 0)
    base = base_ref[...]
    pos_ref[...] = rank_ref[...] + jnp.concatenate(
        [jnp.sum(jnp.where(eidx == e_ref[kk:kk + 1, :], base, 0), axis=0, keepdims=True)
         for kk in range(TOP_K)], axis=0)


def _rows(top_e, rank, base_col):
    n_tok = top_e.shape[1]
    tok_spec = pl.BlockSpec((TOP_K, ROUTE_TM), lambda i: (0, i))
    return pl.pallas_call(
        _rows_kernel,
        grid=(n_tok // ROUTE_TM,),
        in_specs=[tok_spec, tok_spec, _const_spec(base_col.shape)],
        out_specs=tok_spec,
        out_shape=jax.ShapeDtypeStruct((TOP_K, n_tok), jnp.int32),
        compiler_params=pltpu.CompilerParams(dimension_semantics=("arbitrary",)),
        name="rows",
    )(top_e, rank, base_col)


SC_WINDOW = 128


def _sc_mesh():
    return plsc.VectorSubcoreMesh(core_axis_name="c", subcore_axis_name="s")


def _sc_workers():
    info = plsc.get_sparse_core_info()
    return info.num_cores, info.num_cores * info.num_subcores


def _sc_scatter_rows(src, pos, n_rows):
    n_tok, width = src.shape
    n_k = pos.shape[0]
    n_cores, n_workers = _sc_workers()
    n_win = n_tok // (n_workers * SC_WINDOW)

    @functools.partial(
        pl.kernel, out_type=jax.ShapeDtypeStruct((n_rows, width), src.dtype), mesh=_sc_mesh(),
        scratch_types=[pltpu.VMEM((n_k * n_win, SC_WINDOW), jnp.int32),
                       pltpu.VMEM((SC_WINDOW, width), src.dtype),
                       pltpu.SemaphoreType.DMA])
    def scatter(src_hbm, pos_hbm, out_hbm, idx_v, rows_v, sem):
        wid = lax.axis_index("s") * n_cores + lax.axis_index("c")
        pltpu.sync_copy(pos_hbm.at[wid], idx_v)

        @pl.loop(0, n_win)
        def _(j):
            pltpu.sync_copy(src_hbm.at[pl.ds((wid * n_win + j) * SC_WINDOW, SC_WINDOW)], rows_v)
            copies = [pltpu.async_copy(rows_v, out_hbm.at[idx_v.at[kk * n_win + j]], sem)
                      for kk in range(n_k)]
            for cp in copies:
                cp.wait()

    pos4 = pos.reshape(n_k, n_workers, n_win, SC_WINDOW).transpose(1, 0, 2, 3)
    return scatter(src, pos4.reshape(n_workers, n_k * n_win, SC_WINDOW))


def _sc_gather_rows(src, idx):
    n_idx = idx.shape[0]
    width = src.shape[1]
    n_cores, n_workers = _sc_workers()
    n_win = n_idx // (n_workers * SC_WINDOW)

    @functools.partial(
        pl.kernel, out_type=jax.ShapeDtypeStruct((n_idx, width), src.dtype), mesh=_sc_mesh(),
        scratch_types=[pltpu.VMEM((n_win, SC_WINDOW), jnp.int32),
                       pltpu.VMEM((SC_WINDOW, width), src.dtype),
                       pltpu.SemaphoreType.DMA])
    def gather(src_hbm, idx_hbm, out_hbm, idx_v, rows_v, sem):
        wid = lax.axis_index("s") * n_cores + lax.axis_index("c")
        pltpu.sync_copy(idx_hbm.at[wid], idx_v)

        @pl.loop(0, n_win)
        def _(j):
            pltpu.async_copy(src_hbm.at[idx_v.at[j]], rows_v, sem).wait()
            pltpu.sync_copy(rows_v, out_hbm.at[pl.ds((wid * n_win + j) * SC_WINDOW, SC_WINDOW)])

    return gather(src, idx.reshape(n_workers, n_win, SC_WINDOW))


def _unpack_pair(words):
    lo = pltpu.bitcast(words << 16, F32).astype(BF)
    hi = pltpu.bitcast(words & jnp.uint32(0xFFFF0000), F32).astype(BF)
    return lo, hi


def _pack_pair(val):
    bits = pltpu.bitcast(val.astype(BF).astype(F32), jnp.uint32)
    half = val.shape[1] // 2
    return (bits[:, :half] >> 16) | (bits[:, half:] & jnp.uint32(0xFFFF0000))


FFN_SLOTS = 6
FFN_AHEAD = FFN_SLOTS - 1


def _ffn_kernel(base_ref, cnt_ref, xs_ref, wgu_ref, wd_ref, out_ref,
                xbuf, obuf, wgu16_ref, wd16_ref, sem_in, sem_out, pend_ref, *, n_blocks):
    e = pl.program_id(0)
    n_rows_e = cnt_ref[e]
    n_blk = (n_rows_e + FFN_ROWS - 1) // FFN_ROWS
    first_blk = base_ref[e] // FFN_ROWS
    last_cnt = cnt_ref[N_EXPERTS - 1]
    used = base_ref[N_EXPERTS - 1] // FFN_ROWS + (last_cnt + FFN_ROWS - 1) // FFN_ROWS
    half = D_MODEL // 2

    def in_copy(blk, slot):
        row0 = pl.multiple_of(blk * FFN_ROWS, FFN_ROWS)
        return pltpu.make_async_copy(xs_ref.at[pl.ds(row0, FFN_ROWS), :], xbuf.at[slot], sem_in.at[slot])

    def out_copy(blk, slot):
        row0 = pl.multiple_of(blk * FFN_ROWS, FFN_ROWS)
        return pltpu.make_async_copy(obuf.at[slot], out_ref.at[pl.ds(row0, FFN_ROWS), :], sem_out.at[slot])

    @pl.when(e == 0)
    def _():
        for slot in range(FFN_SLOTS):
            pend_ref[slot] = 0
        for blk in range(FFN_AHEAD):
            @pl.when(blk < used)
            def _():
                in_copy(blk, blk).start()

    @pl.when(n_blk > 0)
    def _():
        wgu16_ref[...] = wgu_ref[0].astype(BF)
        wd16_ref[...] = wd_ref[0].astype(BF)

    row = lax.broadcasted_iota(jnp.int32, (FFN_ROWS, 1), 0)

    def block(j, carry):
        blk = first_blk + j
        slot = blk % FFN_SLOTS
        in_copy(blk, slot).wait()

        @pl.when(blk + FFN_AHEAD < used)
        def _():
            in_copy(blk + FFN_AHEAD, (blk + FFN_AHEAD) % FFN_SLOTS).start()

        words = jnp.where(row < n_rows_e - j * FFN_ROWS, xbuf[slot], jnp.uint32(0))
        lo, hi = _unpack_pair(words)
        gu = _dot(lo, wgu16_ref[0:half, :]) + _dot(hi, wgu16_ref[half:, :])
        act = (jax.nn.silu(gu[:, :EXPERT_FF]) * gu[:, EXPERT_FF:]).astype(BF)
        packed = _pack_pair(_dot(act, wd16_ref[...]))

        @pl.when(pend_ref[slot] == 1)
        def _():
            out_copy(blk, slot).wait()

        obuf[slot] = packed
        out_copy(blk, slot).start()
        pend_ref[slot] = 1
        return carry

    lax.fori_loop(0, n_blk, block, 0)

    @pl.when(e == N_EXPERTS - 1)
    def _():
        for slot in range(FFN_SLOTS):
            @pl.when(pend_ref[slot] == 1)
            def _():
                out_copy(0, slot).wait()
        obuf[0] = jnp.zeros((FFN_ROWS, half), jnp.uint32)

        def fill(blk, carry):
            cp = out_copy(blk, 0)
            cp.start()
            cp.wait()
            return carry

        lax.fori_loop(used, n_blocks, fill, 0)


def _ffn(base, counts, xs, wgu, wd):
    n_rows = xs.shape[0]
    half = D_MODEL // 2
    grid_spec = pltpu.PrefetchScalarGridSpec(
        num_scalar_prefetch=2,
        grid=(N_EXPERTS,),
        in_specs=[pl.BlockSpec(memory_space=pl.ANY),
                  pl.BlockSpec((1, D_MODEL, 2 * EXPERT_FF), lambda e, base, cnt: (e, 0, 0)),
                  pl.BlockSpec((1, EXPERT_FF, D_MODEL), lambda e, base, cnt: (e, 0, 0))],
        out_specs=pl.BlockSpec(memory_space=pl.ANY),
        scratch_shapes=[pltpu.VMEM((FFN_SLOTS, FFN_ROWS, half), jnp.uint32),
                        pltpu.VMEM((FFN_SLOTS, FFN_ROWS, half), jnp.uint32),
                        pltpu.VMEM((D_MODEL, 2 * EXPERT_FF), BF),
                        pltpu.VMEM((EXPERT_FF, D_MODEL), BF),
                        pltpu.SemaphoreType.DMA((FFN_SLOTS,)),
                        pltpu.SemaphoreType.DMA((FFN_SLOTS,)),
                        pltpu.SMEM((FFN_SLOTS,), jnp.int32)],
    )
    return pl.pallas_call(
        functools.partial(_ffn_kernel, n_blocks=n_rows // FFN_ROWS),
        grid_spec=grid_spec,
        out_shape=jax.ShapeDtypeStruct((n_rows, half), jnp.uint32),
        compiler_params=pltpu.CompilerParams(dimension_semantics=("arbitrary",)),
        name="ffn",
    )(base, counts, xs, wgu, wd)


def _final_kernel(rows_ref, gate_ref, h_ref, wsgu_ref, wsd_ref, ln2w_ref, ln2b_ref, o_ref):
    h = h_ref[...]
    hb = h.astype(BF)
    gu = _dot(hb, wsgu_ref[...])
    act = (jax.nn.silu(gu[:, :SHARED_FF]) * gu[:, SHARED_FF:]).astype(BF)
    shared = _dot(act, wsd_ref[...])

    half = D_MODEL // 2
    lo_acc = jnp.zeros((FINAL_TM, half), F32)
    hi_acc = jnp.zeros((FINAL_TM, half), F32)
    gates = gate_ref[...]
    for kk in range(TOP_K):
        words = rows_ref[kk]
        gk = gates[:, kk:kk + 1]
        lo_acc = lo_acc + gk * pltpu.bitcast(words << 16, F32)
        hi_acc = hi_acc + gk * pltpu.bitcast(words & jnp.uint32(0xFFFF0000), F32)
    routed = jnp.concatenate([lo_acc, hi_acc], axis=1)
    o_ref[...] = _layer_norm(ALPHA * h + routed + shared, ln2w_ref[...], ln2b_ref[...])


def _final(rows, gates_t, h, wsgu, wsd, ln2w, ln2b):
    n_tok = h.shape[0]
    n_tiles = n_tok // FINAL_TM
    half = D_MODEL // 2
    row = lambda i: (i, 0)
    consts = (wsgu, wsd, ln2w, ln2b)
    return pl.pallas_call(
        _final_kernel,
        grid=(n_tiles,),
        in_specs=[pl.BlockSpec((TOP_K, FINAL_TM, half), lambda i: (0, i, 0)),
                  pl.BlockSpec((FINAL_TM, TOP_K), row),
                  pl.BlockSpec((FINAL_TM, D_MODEL), row)] + [_const_spec(c.shape) for c in consts],
        out_specs=pl.BlockSpec((FINAL_TM, D_MODEL), row),
        out_shape=jax.ShapeDtypeStruct((n_tok, D_MODEL), F32),
        compiler_params=pltpu.CompilerParams(dimension_semantics=("arbitrary",)),
        name="final",
    )(rows, gates_t, h, *consts)


def _np_consts():
    idx = np.arange(QUAD)
    g256 = (idx[:, None] // HEAD_DIM == idx[None, :] // HEAD_DIM).astype(np.float32)
    t = np.arange(CUM_TILE)
    same = t[:, None] // SCAN_CHUNK == t[None, :] // SCAN_CHUNK
    ltri = (same & (t[None, :] <= t[:, None])).astype(np.float32)
    return jnp.asarray(g256, BF), jnp.asarray(ltri, BF)


def _block_plan(counts, n_assign):
    padded = (counts + FFN_ROWS - 1) // FFN_ROWS * FFN_ROWS
    base = jnp.cumsum(padded) - padded
    n_blocks = (n_assign + N_EXPERTS * (FFN_ROWS - 1) + FFN_ROWS - 1) // FFN_ROWS
    return base.astype(jnp.int32), n_blocks * FFN_ROWS


def _layer(x2, seq, w_in, shift_mu, w0, decay_lora, a0, iclr_lora, gate_lora, k_k, k_a, r_k, gn_w, gn_b,
           lnw, lnb, spatial_w, spatial_b, w_o_rwkv, w_o_gmlp, w_out, ln1_w, ln1_b,
           w_router, router_bias, w_exp_gate_up, w_exp_down, w_sh_gate_up, w_sh_down, ln2_w, ln2_b):
    n_tok = x2.shape[0]
    c_rwkv = C_RKV + C_LORA
    pad = C_LORA_PAD - C_LORA
    win = jnp.concatenate([w_in[:, :c_rwkv], jnp.zeros((D_MODEL, pad), F32), w_in[:, c_rwkv:]], axis=1).astype(BF)
    mu = jnp.pad(shift_mu, (0, pad)).reshape(1, C_RWKV_PAD)
    row1 = lambda t: t.reshape(1, -1)
    w1 = jnp.zeros((LANES, 2 * D_MODEL), F32)
    w1 = w1.at[:DECAY_LORA, :D_MODEL].set(decay_lora).at[DECAY_LORA:, D_MODEL:].set(iclr_lora).astype(BF)
    w2 = jnp.zeros((C_LORA_PAD - LANES, D_MODEL), F32).at[:GATE_LORA].set(gate_lora).astype(BF)
    g256, ltri = _np_consts()
    sb_map = jnp.repeat(spatial_b.T, D_MODEL // GMLP_GROUPS, axis=1)

    outs = _front(x2, win, mu, row1(w0), row1(a0), w1, w2, row1(k_k), row1(k_a), row1(r_k),
                  g256, ltri, row1(lnw), row1(lnb), spatial_w, sb_map, w_o_gmlp.astype(BF), seq)
    quads, wc, (g, bg, ga, zb) = outs[:7], outs[7], outs[8:]
    h1, h1p = _scan(quads, wc, g, bg, ga, zb, x2, row1(gn_w), row1(gn_b), g256,
                    w_o_rwkv.astype(BF), w_out.astype(BF), row1(ln1_w), row1(ln1_b), seq)

    wrt = w_router.T
    wrh = wrt.astype(BF)
    wrl = (wrt - wrh.astype(F32)).astype(BF)
    tt = np.arange(ROUTE_TM)
    ustrict = jnp.asarray((tt[:, None] < tt[None, :]).astype(np.float32), BF)
    top_e, gates, rank, counts = _route(h1, wrh, wrl, router_bias.reshape(N_EXPERTS, 1), ustrict)
    counts = counts[:, 0]
    base, n_rows = _block_plan(counts, n_tok * TOP_K)
    pos = _rows(top_e, rank, base.reshape(N_EXPERTS, 1))
    xs = _sc_scatter_rows(h1p, pos, n_rows)
    rows = _ffn(base, counts, xs, w_exp_gate_up, w_exp_down)
    picked = _sc_gather_rows(rows, pos.reshape(-1)).reshape(TOP_K, n_tok, D_MODEL // 2)
    return _final(picked, gates.T, h1, w_sh_gate_up.astype(BF), w_sh_down.astype(BF),
                  row1(ln2_w), row1(ln2_b))


def kernel(x, w_in, shift_mu, rwkv_w0, rwkv_decay_lora, rwkv_a0, rwkv_iclr_lora, rwkv_gate_lora, rwkv_k_k, rwkv_k_a, rwkv_r_k, rwkv_gn_w, rwkv_gn_b, gmlp_ln_w, gmlp_ln_b, gmlp_spatial_w, gmlp_spatial_b, w_o_rwkv, w_o_gmlp, w_out, ln1_w, ln1_b, w_router, router_bias, w_exp_gate_up, w_exp_down, w_sh_gate_up, w_sh_down, ln2_w, ln2_b):
    bsz, seq, d = x.shape
    h = x.reshape(bsz * seq, d)
    for l in range(DEPTH):
        h = _layer(h, seq, w_in[l], shift_mu[l], rwkv_w0[l], rwkv_decay_lora[l], rwkv_a0[l],
                   rwkv_iclr_lora[l], rwkv_gate_lora[l], rwkv_k_k[l], rwkv_k_a[l], rwkv_r_k[l],
                   rwkv_gn_w[l], rwkv_gn_b[l], gmlp_ln_w[l], gmlp_ln_b[l], gmlp_spatial_w[l],
                   gmlp_spatial_b[l], w_o_rwkv[l], w_o_gmlp[l], w_out[l], ln1_w[l], ln1_b[l],
                   w_router[l], router_bias[l], w_exp_gate_up[l], w_exp_down[l], w_sh_gate_up[l],
                   w_sh_down[l], ln2_w[l], ln2_b[l])
    return h.reshape(bsz, seq, d)
```

```python
import functools

import numpy as np
import jax
import jax.numpy as jnp
from jax import lax
from jax.experimental import pallas as pl
from jax.experimental.pallas import tpu as pltpu
from jax.experimental.pallas import tpu_sc as plsc

F32 = jnp.float32
BF = jnp.bfloat16

D_MODEL = 1024
HEAD_DIM = 64
N_HEADS = 16
DECAY_LORA = 64
ICLR_LORA = 64
GATE_LORA = 160
GN_EPS = 64e-5
GMLP_GROUPS = 8
GMLP_CHUNK = 128
N_EXPERTS = 256
TOP_K = 8
N_EXPERT_GROUPS = 8
TOPK_GROUPS = 4
EXPERT_FF = 256
SHARED_FF = 256
ROUTED_SCALE = 2.5
LN_EPS = 1e-5
DEPTH = 1
ALPHA = (2 * DEPTH) ** 0.25

LANES = 128
V7X_VMEM_LIMIT_BYTES = 56 * 1024 * 1024

C_RKV = 3 * D_MODEL
C_LORA = DECAY_LORA + ICLR_LORA + GATE_LORA
C_LORA_PAD = 384
C_RWKV_PAD = C_RKV + C_LORA_PAD
C_GMLP = 2 * D_MODEL
C_GATES = 2 * D_MODEL
C_IN_PAD = C_RWKV_PAD + C_GMLP + C_GATES

SCAN_CHUNK = 64
QUAD = 4 * HEAD_DIM
N_QUADS = D_MODEL // QUAD
TM = 512
CUM_TILE = 256
SCAN_TM = 256
ROUTE_TM = 256
FFN_ROWS = 256
FINAL_TM = 256


def _dot(a, b):
    return jnp.dot(a, b, preferred_element_type=F32)


def _dot_nt(a, b):
    return lax.dot_general(a, b, (((1,), (1,)), ((), ())), preferred_element_type=F32)


def _dot_tn(a, b):
    return lax.dot_general(a, b, (((0,), (0,)), ((), ())), preferred_element_type=F32)


def _split2(z):
    hi = z.astype(BF)
    lo = (z - hi.astype(F32)).astype(BF)
    return hi, lo


def _headsum(z, g256, passes=2):
    outs = []
    for j in range(N_QUADS):
        zj = z[:, QUAD * j:QUAD * (j + 1)]
        if passes == 1:
            outs.append(_dot(zj.astype(BF), g256))
        else:
            hi, lo = _split2(zj)
            outs.append(_dot(hi, g256) + _dot(lo, g256))
    return jnp.concatenate(outs, axis=1)


def _layer_norm(v, w, b):
    mu = jnp.mean(v, axis=-1, keepdims=True)
    d = v - mu
    var = jnp.mean(d * d, axis=-1, keepdims=True)
    return d * lax.rsqrt(var + LN_EPS) * w + b


def _const_spec(shape):
    nd = len(shape)
    return pl.BlockSpec(shape, lambda *_: (0,) * nd, pipeline_mode=pl.Buffered(1))


def _front_kernel(x_ref, win_ref, mu_ref, w0_ref, a0_ref, w1_ref, w2_ref, kk_ref, ka_ref, rk_ref,
                  g256_ref, ltri_ref, lnw_ref, lnb_ref, ws_ref, sb_ref, wog_ref,
                  rt_ref, at_ref, bt_ref, kt_ref, vv_ref, bh_ref, kh_ref, wc_ref,
                  g_ref, bg_ref, ga_ref, zb_ref, prev_ref, xb_ref, *, tiles_per_seq):
    i = pl.program_id(0)
    first = (i % tiles_per_seq) == 0
    xb_ref[...] = x_ref[...].astype(BF)
    is_row0 = lax.broadcasted_iota(jnp.int32, (TM, 1), 0) == 0
    W = QUAD

    def proj(lo, width=W):
        return _dot(xb_ref[...], win_ref[:, lo:lo + width])

    def shift_mix(lo, width=W):
        p = proj(lo, width)
        prev = jnp.where(first, 0.0, prev_ref[:, lo:lo + width])
        sh = jnp.where(is_row0, prev, pltpu.roll(p, 1, 0))
        prev_ref[:, lo:lo + width] = p[TM - 1:TM, :]
        return p + mu_ref[:, lo:lo + width] * (sh - p)

    lora = shift_mix(C_RKV, C_LORA_PAD)
    lane = lax.broadcasted_iota(jnp.int32, (1, LANES), 1)
    act1 = jnp.where(lane < DECAY_LORA, jnp.tanh(lora[:, :LANES]), lora[:, :LANES]).astype(BF)
    act2 = jax.nn.sigmoid(lora[:, LANES:]).astype(BF)
    g256 = g256_ref[...]
    for q in range(N_QUADS):
        c0 = W * q
        cols = slice(c0, c0 + W)
        r = shift_mix(c0)
        k = shift_mix(D_MODEL + c0)
        v = shift_mix(2 * D_MODEL + c0)
        w = -jax.nn.softplus(-(w0_ref[:, cols] + _dot(act1, w1_ref[:, cols]))) - 0.5
        logd = -jnp.exp(w)
        a = jax.nn.sigmoid(a0_ref[:, cols] + _dot(act1, w1_ref[:, D_MODEL + c0:D_MODEL + c0 + W]))
        g = _dot(act2, w2_ref[:, cols])

        kkr = k * kk_ref[:, cols]
        kk = kkr * lax.rsqrt(jnp.maximum(_dot((kkr * kkr).astype(BF), g256), 1e-24))
        kmod = k * (1.0 + (a - 1.0) * ka_ref[:, cols])
        bonus = _dot((r * kmod * rk_ref[:, cols]).astype(BF), g256) * v
        kka = kk * a

        parts = _split2(logd)
        cum = jnp.concatenate(
            [sum(_dot(ltri_ref[...], part[CUM_TILE * hh:CUM_TILE * (hh + 1)]) for part in parts)
             for hh in range(TM // CUM_TILE)], axis=0)
        e_inv = jnp.exp(-cum)
        bt = kka * e_inv
        kt = kmod * e_inv
        ends = []
        for c in range(TM // SCAN_CHUNK):
            last = SCAN_CHUNK * (c + 1) - 1
            ewc_row = jnp.exp(cum[last:last + 1, :])
            wc_ref[c, q:q + 1, :] = ewc_row
            ends.append(jnp.broadcast_to(ewc_row, (SCAN_CHUNK, W)))
        ewc = jnp.concatenate(ends, axis=0)
        rt_ref[q] = (r * jnp.exp(cum)).astype(BF)
        at_ref[q] = (-kk * jnp.exp(cum - logd)).astype(BF)
        bt_ref[q] = bt.astype(BF)
        kt_ref[q] = kt.astype(BF)
        vv_ref[q] = v.astype(BF)
        bh_ref[q] = (bt * ewc).astype(BF)
        kh_ref[q] = (kt * ewc).astype(BF)
        g_ref[:, cols] = g.astype(BF)
        bg_ref[:, cols] = (bonus * g).astype(BF)

    gelu = lambda t: 0.5 * t * (1.0 + lax.erf(t * np.float32(np.sqrt(0.5))))
    u0 = C_RWKV_PAD
    zv = [gelu(proj(u0 + D_MODEL + W * j)) for j in range(N_QUADS)]
    mean = sum(jnp.sum(z, axis=-1, keepdims=True) for z in zv) * (1.0 / D_MODEL)
    dv = [z - mean for z in zv]
    var = sum(jnp.sum(d * d, axis=-1, keepdims=True) for d in dv) * (1.0 / D_MODEL)
    rstd = lax.rsqrt(var + LN_EPS)
    trow = lax.broadcasted_iota(jnp.int32, (GMLP_CHUNK, GMLP_CHUNK), 0)
    tcol = lax.broadcasted_iota(jnp.int32, (GMLP_CHUNK, GMLP_CHUNK), 1)
    causal = tcol <= trow
    n_time = TM // GMLP_CHUNK
    yb = jnp.zeros((TM, D_MODEL), F32)
    for j in range(N_QUADS):
        cols = slice(W * j, W * (j + 1))
        vn = (dv[j] * rstd * lnw_ref[:, cols] + lnb_ref[:, cols]).astype(BF)
        svs = []
        for gl in range(W // LANES):
            wsg = jnp.where(causal, ws_ref[(W // LANES) * j + gl], 0.0).astype(BF)
            svs.append(jnp.concatenate(
                [_dot(wsg, vn[GMLP_CHUNK * cc:GMLP_CHUNK * (cc + 1), LANES * gl:LANES * (gl + 1)])
                 for cc in range(n_time)], axis=0))
        sb = sb_ref[:, cols]
        sv = jnp.concatenate(svs, axis=1) + jnp.concatenate([sb] * n_time, axis=0)
        u = gelu(proj(u0 + W * j))
        yb = yb + _dot((u * sv).astype(BF), wog_ref[W * j:W * (j + 1), :])

    g0 = C_RWKV_PAD + C_GMLP
    for j in range(N_QUADS):
        cols = slice(W * j, W * (j + 1))
        ga_ref[:, cols] = jax.nn.sigmoid(proj(g0 + W * j)).astype(BF)
        zb_ref[:, cols] = (jax.nn.sigmoid(proj(g0 + D_MODEL + W * j)) * yb[:, cols]).astype(BF)


def _front(x2, win, mu, w0, a0, w1, w2, k_k, k_a, r_k, g256, ltri, lnw, lnb, ws, sb, wog, seq):
    n_tok = x2.shape[0]
    n_tiles = n_tok // TM
    row = lambda i: (i, 0)
    quad_spec = pl.BlockSpec((N_QUADS, TM, QUAD), lambda i: (0, i, 0))
    quad_shape = jax.ShapeDtypeStruct((N_QUADS, n_tok, QUAD), BF)
    full_spec = pl.BlockSpec((TM, D_MODEL), row)
    full_shape = jax.ShapeDtypeStruct((n_tok, D_MODEL), BF)
    wc_spec = pl.BlockSpec((TM // SCAN_CHUNK, N_QUADS, QUAD), lambda i: (i, 0, 0))
    wc_shape = jax.ShapeDtypeStruct((n_tok // SCAN_CHUNK, N_QUADS, QUAD), F32)
    consts = (win, mu, w0, a0, w1, w2, k_k, k_a, r_k, g256, ltri, lnw, lnb, ws, sb, wog)
    return pl.pallas_call(
        functools.partial(_front_kernel, tiles_per_seq=seq // TM),
        grid=(n_tiles,),
        in_specs=[pl.BlockSpec((TM, D_MODEL), row)] + [_const_spec(c.shape) for c in consts],
        out_specs=[quad_spec] * 7 + [wc_spec] + [full_spec] * 4,
        out_shape=[quad_shape] * 7 + [wc_shape] + [full_shape] * 4,
        scratch_shapes=[pltpu.VMEM((1, C_RWKV_PAD), F32), pltpu.VMEM((TM, D_MODEL), BF)],
        compiler_params=pltpu.CompilerParams(
            dimension_semantics=("arbitrary",), vmem_limit_bytes=V7X_VMEM_LIMIT_BYTES),
        name="front",
    )(x2, *consts)


def _scan_kernel(rt_ref, at_ref, bt_ref, kt_ref, vv_ref, bh_ref, kh_ref, wc_ref,
                 g_ref, bg_ref, ga_ref, zb_ref, x_ref, gnw_ref, gnb_ref, g256_ref,
                 wor_ref, wout_ref, ln1w_ref, ln1b_ref,
                 h_ref, hp_ref, s_ref, *, tiles_per_seq):
    i = pl.program_id(0)

    @pl.when((i % tiles_per_seq) == 0)
    def _():
        s_ref[...] = jnp.zeros_like(s_ref)

    lane = lax.broadcasted_iota(jnp.int32, (1, QUAD), 1)
    lane_head = lane // HEAD_DIM
    lane_pos = lane % HEAD_DIM
    trow = lax.broadcasted_iota(jnp.int32, (SCAN_CHUNK, 1), 0)
    strict = lane_pos < trow
    incl = lane_pos <= trow
    row_head = lax.broadcasted_iota(jnp.int32, (QUAD, 1), 0) // HEAD_DIM
    diag_blocks = row_head == lane_head

    def bd4(m):
        zero = jnp.zeros_like(m)
        return jnp.concatenate([jnp.where(lane_head == j, m, zero) for j in range(4)], axis=0)

    n_chunks = SCAN_TM // SCAN_CHUNK
    items = [(c, q) for c in range(n_chunks) for q in range(N_QUADS)]
    rows_of = lambda c: slice(SCAN_CHUNK * c, SCAN_CHUNK * (c + 1))
    C = SCAN_CHUNK

    at, rt, bdv, a_ab, a_akrk, a_rb = {}, {}, {}, {}, {}, {}
    for it in items:
        c, q = it
        at[it] = at_ref[q, rows_of(c), :]
        rt[it] = rt_ref[q, rows_of(c), :]
        ra = jnp.concatenate([at[it], rt[it]], axis=0)
        gb = _dot_nt(ra, bd4(bt_ref[q, rows_of(c), :]))
        gk = _dot_nt(ra, bd4(kt_ref[q, rows_of(c), :]))
        a_ab[it] = jnp.where(strict, gb[:C], 0.0)
        a_rb[it] = jnp.where(incl, gb[C:], 0.0).astype(BF)
        a_akrk[it] = jnp.concatenate(
            [jnp.where(strict, gk[:C], 0.0), jnp.where(incl, gk[C:], 0.0)], axis=0).astype(BF)
        bdv[it] = bd4(vv_ref[q, rows_of(c), :])

    tm, qpow = {}, {}
    for it in items:
        qb = a_ab[it].astype(BF)
        tm[it] = a_ab[it]
        qpow[it] = _dot(qb, bd4(qb))
    for level in range(1, 6):
        for it in items:
            qb = qpow[it].astype(BF)
            bdq = bd4(qb)
            if level < 5:
                both = _dot(jnp.concatenate([tm[it].astype(BF), qb], axis=0), bdq)
                tm[it] = tm[it] + qpow[it] + both[:C]
                qpow[it] = both[C:]
            else:
                tm[it] = tm[it] + qpow[it] + _dot(tm[it].astype(BF), bdq)

    awr, u0, rkv = {}, {}, {}
    for it in items:
        tb = tm[it].astype(BF)
        aw = at[it].astype(F32) + _dot(tb, bd4(at[it]))
        awr[it] = jnp.concatenate([aw.astype(BF), rt[it]], axis=0)
        kv = _dot(a_akrk[it], bdv[it])
        u0[it] = kv[:C] + _dot(tb, bd4(kv[:C].astype(BF)))
        rkv[it] = kv[C:]

    state = [s_ref[q] for q in range(N_QUADS)]
    ys = {}
    for c in range(n_chunks):
        for q in range(N_QUADS):
            it = (c, q)
            us = _dot_nt(awr[it], state[q].astype(BF))
            ub = (u0[it] + us[:C]).astype(BF)
            ys[it] = us[C:] + rkv[it] + _dot(a_rb[it], bd4(ub))
            uv = jnp.concatenate([ub, vv_ref[q, rows_of(c), :]], axis=0)
            bk = jnp.concatenate([bh_ref[q, rows_of(c), :], kh_ref[q, rows_of(c), :]], axis=0)
            wc = wc_ref[c, q:q + 1, :]
            state[q] = state[q] * wc + jnp.where(diag_blocks, _dot_tn(uv, bk), 0.0)
    for q in range(N_QUADS):
        s_ref[q] = state[q]

    g256 = g256_ref[...]
    y = jnp.concatenate(
        [jnp.concatenate([ys[(c, q)] for c in range(n_chunks)], axis=0) for q in range(N_QUADS)], axis=1)
    inv_n = 1.0 / HEAD_DIM
    d = y - _headsum(y, g256) * inv_n
    var = _headsum(d * d, g256) * inv_n
    yn = d * lax.rsqrt(var + GN_EPS) * gnw_ref[...] + gnb_ref[...]
    ya = _dot((yn * g_ref[...].astype(F32) + bg_ref[...].astype(F32)).astype(BF), wor_ref[...])
    mix_in = ga_ref[...].astype(F32) * ya + zb_ref[...].astype(F32)
    mix = _dot(mix_in.astype(BF), wout_ref[...])
    h = _layer_norm(ALPHA * x_ref[...] + mix, ln1w_ref[...], ln1b_ref[...])
    h_ref[...] = h
    hb = pltpu.bitcast(h.astype(BF).astype(F32), jnp.uint32)
    half = D_MODEL // 2
    hp_ref[...] = (hb[:, :half] >> 16) | (hb[:, half:] & jnp.uint32(0xFFFF0000))


def _scan(quads, wc, g, bg, ga, zb, x2, gnw, gnb, g256, wor, wout, ln1w, ln1b, seq):
    n_tok = x2.shape[0]
    n_tiles = n_tok // SCAN_TM
    row = lambda i: (i, 0)
    quad_spec = pl.BlockSpec((N_QUADS, SCAN_TM, QUAD), lambda i: (0, i, 0))
    full_spec = pl.BlockSpec((SCAN_TM, D_MODEL), row)
    wc_spec = pl.BlockSpec((SCAN_TM // SCAN_CHUNK, N_QUADS, QUAD), lambda i: (i, 0, 0))
    consts = (gnw, gnb, g256, wor, wout, ln1w, ln1b)
    return pl.pallas_call(
        functools.partial(_scan_kernel, tiles_per_seq=seq // SCAN_TM),
        grid=(n_tiles,),
        in_specs=[quad_spec] * 7 + [wc_spec]
        + [full_spec] * 5 + [_const_spec(c.shape) for c in consts],
        out_specs=[full_spec, pl.BlockSpec((SCAN_TM, D_MODEL // 2), row)],
        out_shape=[jax.ShapeDtypeStruct((n_tok, D_MODEL), F32),
                   jax.ShapeDtypeStruct((n_tok, D_MODEL // 2), jnp.uint32)],
        scratch_shapes=[pltpu.VMEM((N_QUADS, QUAD, QUAD), F32)],
        compiler_params=pltpu.CompilerParams(
            dimension_semantics=("arbitrary",), vmem_limit_bytes=V7X_VMEM_LIMIT_BYTES),
        name="scan",
    )(*quads, wc, g, bg, ga, zb, x2, *consts)


def _first_max(x, idx, n):
    m = jnp.max(x, axis=0, keepdims=True)
    first = jnp.min(jnp.where(x == m, idx, n), axis=0, keepdims=True)
    return m, first


def _route_kernel(h_ref, wrh_ref, wrl_ref, bias_ref, ustrict_ref, e_ref, gate_ref, rank_ref, cnt_ref,
                  carry_ref):
    @pl.when(pl.program_id(0) == 0)
    def _():
        carry_ref[...] = jnp.zeros_like(carry_ref)

    hh, hl = _split2(h_ref[...])
    wrh = wrh_ref[...]
    logits = _dot_nt(wrh, hh) + _dot_nt(wrh, hl) + _dot_nt(wrl_ref[...], hh)
    scores = jax.nn.sigmoid(logits)
    choice = scores + bias_ref[...]
    n_tok = choice.shape[1]
    per = N_EXPERTS // N_EXPERT_GROUPS
    eidx = lax.broadcasted_iota(jnp.int32, (N_EXPERTS, 1), 0)
    gidx = lax.broadcasted_iota(jnp.int32, (N_EXPERT_GROUPS, 1), 0)
    lidx = lax.broadcasted_iota(jnp.int32, (per, 1), 0)
    neg = -jnp.inf

    gsc = []
    for gi in range(N_EXPERT_GROUPS):
        blk = choice[per * gi:per * (gi + 1), :]
        m1, i1 = _first_max(blk, lidx, per)
        m2 = jnp.max(jnp.where(lidx == i1, neg, blk), axis=0, keepdims=True)
        gsc.append(m1 + m2)
    gscore = jnp.concatenate(gsc, axis=0)
    gsel = jnp.zeros(gscore.shape, jnp.bool_)
    for _ in range(TOPK_GROUPS):
        _, gi1 = _first_max(gscore, gidx, N_EXPERT_GROUPS)
        hit = gidx == gi1
        gsel = gsel | hit
        gscore = jnp.where(hit, neg, gscore)
    emask = jnp.concatenate(
        [jnp.broadcast_to(gsel[gi:gi + 1, :], (per, n_tok)) for gi in range(N_EXPERT_GROUPS)], axis=0)
    cand = jnp.where(emask, choice, neg)
    es, gs, hits = [], [], []
    for _ in range(TOP_K):
        _, e1 = _first_max(cand, eidx, N_EXPERTS)
        hit = eidx == e1
        es.append(e1)
        hits.append(hit)
        gs.append(jnp.sum(jnp.where(hit, scores, 0.0), axis=0, keepdims=True))
        cand = jnp.where(hit, neg, cand)
    gates = jnp.concatenate(gs, axis=0)
    gates = gates / jnp.sum(gates, axis=0, keepdims=True) * ROUTED_SCALE
    e_ref[...] = jnp.concatenate(es, axis=0)
    gate_ref[...] = gates

    chosen = functools.reduce(jnp.logical_or, hits)
    sel = jnp.where(chosen, 1.0, 0.0)
    before = _dot(sel.astype(BF), ustrict_ref[...]) + carry_ref[:, 0:1]
    rank_ref[...] = jnp.concatenate(
        [jnp.sum(jnp.where(hit, before, 0.0), axis=0, keepdims=True) for hit in hits],
        axis=0).astype(jnp.int32)
    total = before[:, n_tok - 1:n_tok] + sel[:, n_tok - 1:n_tok]
    carry_ref[...] = jnp.broadcast_to(total, carry_ref.shape)
    cnt_ref[...] = jnp.broadcast_to(total, cnt_ref.shape).astype(jnp.int32)


def _route(h, wrh, wrl, bias_col, ustrict):
    n_tok = h.shape[0]
    col = lambda i: (0, i)
    tok_spec = pl.BlockSpec((TOP_K, ROUTE_TM), col)
    return pl.pallas_call(
        _route_kernel,
        grid=(n_tok // ROUTE_TM,),
        in_specs=[pl.BlockSpec((ROUTE_TM, D_MODEL), lambda i: (i, 0)),
                  _const_spec(wrh.shape), _const_spec(wrl.shape), _const_spec(bias_col.shape),
                  _const_spec(ustrict.shape)],
        out_specs=[tok_spec, tok_spec, tok_spec, pl.BlockSpec((N_EXPERTS, LANES), lambda i: (0, 0))],
        out_shape=[jax.ShapeDtypeStruct((TOP_K, n_tok), jnp.int32),
                   jax.ShapeDtypeStruct((TOP_K, n_tok), F32),
                   jax.ShapeDtypeStruct((TOP_K, n_tok), jnp.int32),
                   jax.ShapeDtypeStruct((N_EXPERTS, LANES), jnp.int32)],
        scratch_shapes=[pltpu.VMEM((N_EXPERTS, LANES), F32)],
        compiler_params=pltpu.CompilerParams(dimension_semantics=("arbitrary",)),
        name="route",
    )(h, wrh, wrl, bias_col, ustrict)


def _rows_kernel(e_ref, rank_ref, base_ref, pos_ref):
    eidx = lax.broadcasted_iota(jnp.int32, (N_EXPERTS, 1), 0)
    base = base_ref[...]
    pos_ref[...] = rank_ref[...] + jnp.concatenate(
        [jnp.sum(jnp.where(eidx == e_ref[kk:kk + 1, :], base, 0), axis=0, keepdims=True)
         for kk in range(TOP_K)], axis=0)


def _rows(top_e, rank, base_col):
    n_tok = top_e.shape[1]
    tok_spec = pl.BlockSpec((TOP_K, ROUTE_TM), lambda i: (0, i))
    return pl.pallas_call(
        _rows_kernel,
        grid=(n_tok // ROUTE_TM,),
        in_specs=[tok_spec, tok_spec, _const_spec(base_col.shape)],
        out_specs=tok_spec,
        out_shape=jax.ShapeDtypeStruct((TOP_K, n_tok), jnp.int32),
        compiler_params=pltpu.CompilerParams(dimension_semantics=("arbitrary",)),
        name="rows",
    )(top_e, rank, base_col)


SC_WINDOW = 128


def _sc_mesh():
    return plsc.VectorSubcoreMesh(core_axis_name="c", subcore_axis_name="s")


def _sc_workers():
    info = plsc.get_sparse_core_info()
    return info.num_cores, info.num_cores * info.num_subcores


def _sc_scatter_rows(src, pos, n_rows):
    n_tok, width = src.shape
    n_k = pos.shape[0]
    n_cores, n_workers = _sc_workers()
    n_win = n_tok // (n_workers * SC_WINDOW)

    @functools.partial(
        pl.kernel, out_type=jax.ShapeDtypeStruct((n_rows, width), src.dtype), mesh=_sc_mesh(),
        scratch_types=[pltpu.VMEM((n_k * n_win, SC_WINDOW), jnp.int32),
                       pltpu.VMEM((SC_WINDOW, width), src.dtype),
                       pltpu.SemaphoreType.DMA])
    def scatter(src_hbm, pos_hbm, out_hbm, idx_v, rows_v, sem):
        wid = lax.axis_index("s") * n_cores + lax.axis_index("c")
        pltpu.sync_copy(pos_hbm.at[wid], idx_v)

        @pl.loop(0, n_win)
        def _(j):
            pltpu.sync_copy(src_hbm.at[pl.ds((wid * n_win + j) * SC_WINDOW, SC_WINDOW)], rows_v)
            copies = [pltpu.async_copy(rows_v, out_hbm.at[idx_v.at[kk * n_win + j]], sem)
                      for kk in range(n_k)]
            for cp in copies:
                cp.wait()

    pos4 = pos.reshape(n_k, n_workers, n_win, SC_WINDOW).transpose(1, 0, 2, 3)
    return scatter(src, pos4.reshape(n_workers, n_k * n_win, SC_WINDOW))


def _sc_gather_rows(src, idx):
    n_idx = idx.shape[0]
    width = src.shape[1]
    n_cores, n_workers = _sc_workers()
    n_win = n_idx // (n_workers * SC_WINDOW)

    @functools.partial(
        pl.kernel, out_type=jax.ShapeDtypeStruct((n_idx, width), src.dtype), mesh=_sc_mesh(),
        scratch_types=[pltpu.VMEM((n_win, SC_WINDOW), jnp.int32),
                       pltpu.VMEM((SC_WINDOW, width), src.dtype),
                       pltpu.SemaphoreType.DMA])
    def gather(src_hbm, idx_hbm, out_hbm, idx_v, rows_v, sem):
        wid = lax.axis_index("s") * n_cores + lax.axis_index("c")
        pltpu.sync_copy(idx_hbm.at[wid], idx_v)

        @pl.loop(0, n_win)
        def _(j):
            pltpu.async_copy(src_hbm.at[idx_v.at[j]], rows_v, sem).wait()
            pltpu.sync_copy(rows_v, out_hbm.at[pl.ds((wid * n_win + j) * SC_WINDOW, SC_WINDOW)])

    return gather(src, idx.reshape(n_workers, n_win, SC_WINDOW))


def _unpack_pair(words):
    lo = pltpu.bitcast(words << 16, F32).astype(BF)
    hi = pltpu.bitcast(words & jnp.uint32(0xFFFF0000), F32).astype(BF)
    return lo, hi


def _pack_pair(val):
    bits = pltpu.bitcast(val.astype(BF).astype(F32), jnp.uint32)
    half = val.shape[1] // 2
    return (bits[:, :half] >> 16) | (bits[:, half:] & jnp.uint32(0xFFFF0000))


FFN_SLOTS = 6
FFN_AHEAD = FFN_SLOTS - 1


def _ffn_kernel(base_ref, cnt_ref, xs_ref, wgu_ref, wd_ref, out_ref,
                xbuf, obuf, wgu16_ref, wd16_ref, sem_in, sem_out, pend_ref, *, n_blocks):
    e = pl.program_id(0)
    n_rows_e = cnt_ref[e]
    n_blk = (n_rows_e + FFN_ROWS - 1) // FFN_ROWS
    first_blk = base_ref[e] // FFN_ROWS
    last_cnt = cnt_ref[N_EXPERTS - 1]
    used = base_ref[N_EXPERTS - 1] // FFN_ROWS + (last_cnt + FFN_ROWS - 1) // FFN_ROWS
    half = D_MODEL // 2

    def in_copy(blk, slot):
        row0 = pl.multiple_of(blk * FFN_ROWS, FFN_ROWS)
        return pltpu.make_async_copy(xs_ref.at[pl.ds(row0, FFN_ROWS), :], xbuf.at[slot], sem_in.at[slot])

    def out_copy(blk, slot):
        row0 = pl.multiple_of(blk * FFN_ROWS, FFN_ROWS)
        return pltpu.make_async_copy(obuf.at[slot], out_ref.at[pl.ds(row0, FFN_ROWS), :], sem_out.at[slot])

    @pl.when(e == 0)
    def _():
        for slot in range(FFN_SLOTS):
            pend_ref[slot] = 0
        for blk in range(FFN_AHEAD):
            @pl.when(blk < used)
            def _():
                in_copy(blk, blk).start()

    @pl.when(n_blk > 0)
    def _():
        wgu16_ref[...] = wgu_ref[0].astype(BF)
        wd16_ref[...] = wd_ref[0].astype(BF)

    def fetch_ahead(blk):
        @pl.when(blk + FFN_AHEAD < used)
        def _():
            in_copy(blk + FFN_AHEAD, (blk + FFN_AHEAD) % FFN_SLOTS).start()

    def ffn_rows(words, j):
        row = lax.broadcasted_iota(jnp.int32, (words.shape[0], 1), 0)
        lo, hi = _unpack_pair(jnp.where(row < n_rows_e - j * FFN_ROWS, words, jnp.uint32(0)))
        gu = _dot(lo, wgu16_ref[0:half, :]) + _dot(hi, wgu16_ref[half:, :])
        act = (jax.nn.silu(gu[:, :EXPERT_FF]) * gu[:, EXPERT_FF:]).astype(BF)
        return _pack_pair(_dot(act, wd16_ref[...]))

    def write_back(blk, slot, packed):
        @pl.when(pend_ref[slot] == 1)
        def _():
            out_copy(blk, slot).wait()

        obuf[slot] = packed
        out_copy(blk, slot).start()
        pend_ref[slot] = 1

    def single(j):
        blk = first_blk + j
        slot = blk % FFN_SLOTS
        in_copy(blk, slot).wait()
        fetch_ahead(blk)
        write_back(blk, slot, ffn_rows(xbuf[slot], j))

    def pair(p, carry):
        j = lead + 2 * p
        blk = first_blk + j
        slot = pl.multiple_of(blk % FFN_SLOTS, 2)
        in_copy(blk, slot).wait()
        in_copy(blk + 1, slot + 1).wait()
        fetch_ahead(blk)
        packed = ffn_rows(xbuf[pl.ds(slot, 2)].reshape(2 * FFN_ROWS, half), j)
        fetch_ahead(blk + 1)
        write_back(blk, slot, packed[:FFN_ROWS])
        write_back(blk + 1, slot + 1, packed[FFN_ROWS:])
        return carry

    lead = jnp.where(n_blk > 0, first_blk % 2, 0)

    @pl.when(lead == 1)
    def _():
        single(0)

    lax.fori_loop(0, (n_blk - lead) // 2, pair, 0)

    @pl.when((n_blk - lead) % 2 == 1)
    def _():
        single(n_blk - 1)

    @pl.when(e == N_EXPERTS - 1)
    def _():
        for slot in range(FFN_SLOTS):
            @pl.when(pend_ref[slot] == 1)
            def _():
                out_copy(0, slot).wait()
        obuf[0] = jnp.zeros((FFN_ROWS, half), jnp.uint32)

        def fill(blk, carry):
            cp = out_copy(blk, 0)
            cp.start()
            cp.wait()
            return carry

        lax.fori_loop(used, n_blocks, fill, 0)


def _ffn(base, counts, xs, wgu, wd):
    n_rows = xs.shape[0]
    half = D_MODEL // 2
    grid_spec = pltpu.PrefetchScalarGridSpec(
        num_scalar_prefetch=2,
        grid=(N_EXPERTS,),
        in_specs=[pl.BlockSpec(memory_space=pl.ANY),
                  pl.BlockSpec((1, D_MODEL, 2 * EXPERT_FF), lambda e, base, cnt: (e, 0, 0)),
                  pl.BlockSpec((1, EXPERT_FF, D_MODEL), lambda e, base, cnt: (e, 0, 0))],
        out_specs=pl.BlockSpec(memory_space=pl.ANY),
        scratch_shapes=[pltpu.VMEM((FFN_SLOTS, FFN_ROWS, half), jnp.uint32),
                        pltpu.VMEM((FFN_SLOTS, FFN_ROWS, half), jnp.uint32),
                        pltpu.VMEM((D_MODEL, 2 * EXPERT_FF), BF),
                        pltpu.VMEM((EXPERT_FF, D_MODEL), BF),
                        pltpu.SemaphoreType.DMA((FFN_SLOTS,)),
                        pltpu.SemaphoreType.DMA((FFN_SLOTS,)),
                        pltpu.SMEM((FFN_SLOTS,), jnp.int32)],
    )
    return pl.pallas_call(
        functools.partial(_ffn_kernel, n_blocks=n_rows // FFN_ROWS),
        grid_spec=grid_spec,
        out_shape=jax.ShapeDtypeStruct((n_rows, half), jnp.uint32),
        compiler_params=pltpu.CompilerParams(dimension_semantics=("arbitrary",)),
        name="ffn",
    )(base, counts, xs, wgu, wd)


def _final_kernel(rows_ref, gate_ref, h_ref, wsgu_ref, wsd_ref, ln2w_ref, ln2b_ref, o_ref):
    h = h_ref[...]
    hb = h.astype(BF)
    gu = _dot(hb, wsgu_ref[...])
    act = (jax.nn.silu(gu[:, :SHARED_FF]) * gu[:, SHARED_FF:]).astype(BF)
    shared = _dot(act, wsd_ref[...])

    half = D_MODEL // 2
    lo_acc = jnp.zeros((FINAL_TM, half), F32)
    hi_acc = jnp.zeros((FINAL_TM, half), F32)
    gates = gate_ref[...]
    for kk in range(TOP_K):
        words = rows_ref[kk]
        gk = gates[:, kk:kk + 1]
        lo_acc = lo_acc + gk * pltpu.bitcast(words << 16, F32)
        hi_acc = hi_acc + gk * pltpu.bitcast(words & jnp.uint32(0xFFFF0000), F32)
    routed = jnp.concatenate([lo_acc, hi_acc], axis=1)
    o_ref[...] = _layer_norm(ALPHA * h + routed + shared, ln2w_ref[...], ln2b_ref[...])


def _final(rows, gates_t, h, wsgu, wsd, ln2w, ln2b):
    n_tok = h.shape[0]
    n_tiles = n_tok // FINAL_TM
    half = D_MODEL // 2
    row = lambda i: (i, 0)
    consts = (wsgu, wsd, ln2w, ln2b)
    return pl.pallas_call(
        _final_kernel,
        grid=(n_tiles,),
        in_specs=[pl.BlockSpec((TOP_K, FINAL_TM, half), lambda i: (0, i, 0)),
                  pl.BlockSpec((FINAL_TM, TOP_K), row),
                  pl.BlockSpec((FINAL_TM, D_MODEL), row)] + [_const_spec(c.shape) for c in consts],
        out_specs=pl.BlockSpec((FINAL_TM, D_MODEL), row),
        out_shape=jax.ShapeDtypeStruct((n_tok, D_MODEL), F32),
        compiler_params=pltpu.CompilerParams(dimension_semantics=("arbitrary",)),
        name="final",
    )(rows, gates_t, h, *consts)


def _np_consts():
    idx = np.arange(QUAD)
    g256 = (idx[:, None] // HEAD_DIM == idx[None, :] // HEAD_DIM).astype(np.float32)
    t = np.arange(CUM_TILE)
    same = t[:, None] // SCAN_CHUNK == t[None, :] // SCAN_CHUNK
    ltri = (same & (t[None, :] <= t[:, None])).astype(np.float32)
    return jnp.asarray(g256, BF), jnp.asarray(ltri, BF)


def _block_plan(counts, n_assign):
    padded = (counts + FFN_ROWS - 1) // FFN_ROWS * FFN_ROWS
    base = jnp.cumsum(padded) - padded
    n_blocks = (n_assign + N_EXPERTS * (FFN_ROWS - 1) + FFN_ROWS - 1) // FFN_ROWS
    return base.astype(jnp.int32), n_blocks * FFN_ROWS


def _layer(x2, seq, w_in, shift_mu, w0, decay_lora, a0, iclr_lora, gate_lora, k_k, k_a, r_k, gn_w, gn_b,
           lnw, lnb, spatial_w, spatial_b, w_o_rwkv, w_o_gmlp, w_out, ln1_w, ln1_b,
           w_router, router_bias, w_exp_gate_up, w_exp_down, w_sh_gate_up, w_sh_down, ln2_w, ln2_b):
    n_tok = x2.shape[0]
    c_rwkv = C_RKV + C_LORA
    pad = C_LORA_PAD - C_LORA
    win = jnp.concatenate([w_in[:, :c_rwkv], jnp.zeros((D_MODEL, pad), F32), w_in[:, c_rwkv:]], axis=1).astype(BF)
    mu = jnp.pad(shift_mu, (0, pad)).reshape(1, C_RWKV_PAD)
    row1 = lambda t: t.reshape(1, -1)
    w1 = jnp.zeros((LANES, 2 * D_MODEL), F32)
    w1 = w1.at[:DECAY_LORA, :D_MODEL].set(decay_lora).at[DECAY_LORA:, D_MODEL:].set(iclr_lora).astype(BF)
    w2 = jnp.zeros((C_LORA_PAD - LANES, D_MODEL), F32).at[:GATE_LORA].set(gate_lora).astype(BF)
    g256, ltri = _np_consts()
    sb_map = jnp.repeat(spatial_b.T, D_MODEL // GMLP_GROUPS, axis=1)

    outs = _front(x2, win, mu, row1(w0), row1(a0), w1, w2, row1(k_k), row1(k_a), row1(r_k),
                  g256, ltri, row1(lnw), row1(lnb), spatial_w, sb_map, w_o_gmlp.astype(BF), seq)
    quads, wc, (g, bg, ga, zb) = outs[:7], outs[7], outs[8:]
    h1, h1p = _scan(quads, wc, g, bg, ga, zb, x2, row1(gn_w), row1(gn_b), g256,
                    w_o_rwkv.astype(BF), w_out.astype(BF), row1(ln1_w), row1(ln1_b), seq)

    wrt = w_router.T
    wrh = wrt.astype(BF)
    wrl = (wrt - wrh.astype(F32)).astype(BF)
    tt = np.arange(ROUTE_TM)
    ustrict = jnp.asarray((tt[:, None] < tt[None, :]).astype(np.float32), BF)
    top_e, gates, rank, counts = _route(h1, wrh, wrl, router_bias.reshape(N_EXPERTS, 1), ustrict)
    counts = counts[:, 0]
    base, n_rows = _block_plan(counts, n_tok * TOP_K)
    pos = _rows(top_e, rank, base.reshape(N_EXPERTS, 1))
    xs = _sc_scatter_rows(h1p, pos, n_rows)
    rows = _ffn(base, counts, xs, w_exp_gate_up, w_exp_down)
    picked = _sc_gather_rows(rows, pos.reshape(-1)).reshape(TOP_K, n_tok, D_MODEL // 2)
    return _final(picked, gates.T, h1, w_sh_gate_up.astype(BF), w_sh_down.astype(BF),
                  row1(ln2_w), row1(ln2_b))


def kernel(x, w_in, shift_mu, rwkv_w0, rwkv_decay_lora, rwkv_a0, rwkv_iclr_lora, rwkv_gate_lora, rwkv_k_k, rwkv_k_a, rwkv_r_k, rwkv_gn_w, rwkv_gn_b, gmlp_ln_w, gmlp_ln_b, gmlp_spatial_w, gmlp_spatial_b, w_o_rwkv, w_o_gmlp, w_out, ln1_w, ln1_b, w_router, router_bias, w_exp_gate_up, w_exp_down, w_sh_gate_up, w_sh_down, ln2_w, ln2_b):
    bsz, seq, d = x.shape
    h = x.reshape(bsz * seq, d)
    for l in range(DEPTH):
        h = _layer(h, seq, w_in[l], shift_mu[l], rwkv_w0[l], rwkv_decay_lora[l], rwkv_a0[l],
                   rwkv_iclr_lora[l], rwkv_gate_lora[l], rwkv_k_k[l], rwkv_k_a[l], rwkv_r_k[l],
                   rwkv_gn_w[l], rwkv_gn_b[l], gmlp_ln_w[l], gmlp_ln_b[l], gmlp_spatial_w[l],
                   gmlp_spatial_b[l], w_o_rwkv[l], w_o_gmlp[l], w_out[l], ln1_w[l], ln1_b[l],
                   w_router[l], router_bias[l], w_exp_gate_up[l], w_exp_down[l], w_sh_gate_up[l],
                   w_sh_down[l], ln2_w[l], ln2_b[l])
    return h.reshape(bsz, seq, d)
```

```python
import functools

import numpy as np
import jax
import jax.numpy as jnp
from jax import lax
from jax.experimental import pallas as pl
from jax.experimental.pallas import tpu as pltpu
from jax.experimental.pallas import tpu_sc as plsc

F32 = jnp.float32
BF = jnp.bfloat16

D_MODEL = 1024
HEAD_DIM = 64
N_HEADS = 16
DECAY_LORA = 64
ICLR_LORA = 64
GATE_LORA = 160
GN_EPS = 64e-5
GMLP_GROUPS = 8
GMLP_CHUNK = 128
N_EXPERTS = 256
TOP_K = 8
N_EXPERT_GROUPS = 8
TOPK_GROUPS = 4
EXPERT_FF = 256
SHARED_FF = 256
ROUTED_SCALE = 2.5
LN_EPS = 1e-5
DEPTH = 1
ALPHA = (2 * DEPTH) ** 0.25

LANES = 128
V7X_VMEM_LIMIT_BYTES = 56 * 1024 * 1024

C_RKV = 3 * D_MODEL
C_LORA = DECAY_LORA + ICLR_LORA + GATE_LORA
C_LORA_PAD = 384
C_RWKV_PAD = C_RKV + C_LORA_PAD
C_GMLP = 2 * D_MODEL
C_GATES = 2 * D_MODEL
C_IN_PAD = C_RWKV_PAD + C_GMLP + C_GATES

SCAN_CHUNK = 64
QUAD = 4 * HEAD_DIM
N_QUADS = D_MODEL // QUAD
TM = 512
CUM_TILE = 256
SCAN_TM = 256
ROUTE_TM = 512
FFN_ROWS = 256
FINAL_TM = 512


def _dot(a, b):
    return jnp.dot(a, b, preferred_element_type=F32)


def _dot_nt(a, b):
    return lax.dot_general(a, b, (((1,), (1,)), ((), ())), preferred_element_type=F32)


def _dot_tn(a, b):
    return lax.dot_general(a, b, (((0,), (0,)), ((), ())), preferred_element_type=F32)


def _split2(z):
    hi = z.astype(BF)
    lo = (z - hi.astype(F32)).astype(BF)
    return hi, lo


def _headsum(z, g256, passes=2):
    outs = []
    for j in range(N_QUADS):
        zj = z[:, QUAD * j:QUAD * (j + 1)]
        if passes == 1:
            outs.append(_dot(zj.astype(BF), g256))
        else:
            hi, lo = _split2(zj)
            outs.append(_dot(hi, g256) + _dot(lo, g256))
    return jnp.concatenate(outs, axis=1)


def _layer_norm(v, w, b):
    mu = jnp.mean(v, axis=-1, keepdims=True)
    d = v - mu
    var = jnp.mean(d * d, axis=-1, keepdims=True)
    return d * lax.rsqrt(var + LN_EPS) * w + b


def _const_spec(shape):
    nd = len(shape)
    return pl.BlockSpec(shape, lambda *_: (0,) * nd, pipeline_mode=pl.Buffered(1))


def _front_kernel(x_ref, win_ref, mu_ref, w0_ref, a0_ref, w1_ref, w2_ref, kk_ref, ka_ref, rk_ref,
                  g256_ref, ltri_ref, lnw_ref, lnb_ref, ws_ref, sb_ref, wog_ref,
                  rt_ref, at_ref, bt_ref, kt_ref, vv_ref, bh_ref, kh_ref, wc_ref,
                  g_ref, bg_ref, ga_ref, zb_ref, prev_ref, xb_ref, *, tiles_per_seq):
    i = pl.program_id(0)
    first = (i % tiles_per_seq) == 0
    xb_ref[...] = x_ref[...].astype(BF)
    is_row0 = lax.broadcasted_iota(jnp.int32, (TM, 1), 0) == 0
    W = QUAD

    def proj(lo, width=W):
        return _dot(xb_ref[...], win_ref[:, lo:lo + width])

    def shift_mix(lo, width=W):
        p = proj(lo, width)
        prev = jnp.where(first, 0.0, prev_ref[:, lo:lo + width])
        sh = jnp.where(is_row0, prev, pltpu.roll(p, 1, 0))
        prev_ref[:, lo:lo + width] = p[TM - 1:TM, :]
        return p + mu_ref[:, lo:lo + width] * (sh - p)

    lora = shift_mix(C_RKV, C_LORA_PAD)
    lane = lax.broadcasted_iota(jnp.int32, (1, LANES), 1)
    act1 = jnp.where(lane < DECAY_LORA, jnp.tanh(lora[:, :LANES]), lora[:, :LANES]).astype(BF)
    act2 = jax.nn.sigmoid(lora[:, LANES:]).astype(BF)
    g256 = g256_ref[...]
    for q in range(N_QUADS):
        c0 = W * q
        cols = slice(c0, c0 + W)
        r = shift_mix(c0)
        k = shift_mix(D_MODEL + c0)
        v = shift_mix(2 * D_MODEL + c0)
        w = -jax.nn.softplus(-(w0_ref[:, cols] + _dot(act1, w1_ref[:, cols]))) - 0.5
        logd = -jnp.exp(w)
        a = jax.nn.sigmoid(a0_ref[:, cols] + _dot(act1, w1_ref[:, D_MODEL + c0:D_MODEL + c0 + W]))
        g = _dot(act2, w2_ref[:, cols])

        kkr = k * kk_ref[:, cols]
        kk = kkr * lax.rsqrt(jnp.maximum(_dot((kkr * kkr).astype(BF), g256), 1e-24))
        kmod = k * (1.0 + (a - 1.0) * ka_ref[:, cols])
        bonus = _dot((r * kmod * rk_ref[:, cols]).astype(BF), g256) * v
        kka = kk * a

        parts = _split2(logd)
        cum = jnp.concatenate(
            [sum(_dot(ltri_ref[...], part[CUM_TILE * hh:CUM_TILE * (hh + 1)]) for part in parts)
             for hh in range(TM // CUM_TILE)], axis=0)
        e_inv = jnp.exp(-cum)
        bt = kka * e_inv
        kt = kmod * e_inv
        ends = []
        for c in range(TM // SCAN_CHUNK):
            last = SCAN_CHUNK * (c + 1) - 1
            ewc_row = jnp.exp(cum[last:last + 1, :])
            wc_ref[c, q:q + 1, :] = ewc_row
            ends.append(jnp.broadcast_to(ewc_row, (SCAN_CHUNK, W)))
        ewc = jnp.concatenate(ends, axis=0)
        rt_ref[q] = (r * jnp.exp(cum)).astype(BF)
        at_ref[q] = (-kk * jnp.exp(cum - logd)).astype(BF)
        bt_ref[q] = bt.astype(BF)
        kt_ref[q] = kt.astype(BF)
        vv_ref[q] = v.astype(BF)
        bh_ref[q] = (bt * ewc).astype(BF)
        kh_ref[q] = (kt * ewc).astype(BF)
        g_ref[:, cols] = g.astype(BF)
        bg_ref[:, cols] = (bonus * g).astype(BF)

    gelu = lambda t: 0.5 * t * (1.0 + lax.erf(t * np.float32(np.sqrt(0.5))))
    u0 = C_RWKV_PAD
    zv = [gelu(proj(u0 + D_MODEL + W * j)) for j in range(N_QUADS)]
    mean = sum(jnp.sum(z, axis=-1, keepdims=True) for z in zv) * (1.0 / D_MODEL)
    dv = [z - mean for z in zv]
    var = sum(jnp.sum(d * d, axis=-1, keepdims=True) for d in dv) * (1.0 / D_MODEL)
    rstd = lax.rsqrt(var + LN_EPS)
    trow = lax.broadcasted_iota(jnp.int32, (GMLP_CHUNK, GMLP_CHUNK), 0)
    tcol = lax.broadcasted_iota(jnp.int32, (GMLP_CHUNK, GMLP_CHUNK), 1)
    causal = tcol <= trow
    n_time = TM // GMLP_CHUNK
    yb = jnp.zeros((TM, D_MODEL), F32)
    for j in range(N_QUADS):
        cols = slice(W * j, W * (j + 1))
        vn = (dv[j] * rstd * lnw_ref[:, cols] + lnb_ref[:, cols]).astype(BF)
        svs = []
        for gl in range(W // LANES):
            wsg = jnp.where(causal, ws_ref[(W // LANES) * j + gl], 0.0).astype(BF)
            svs.append(jnp.concatenate(
                [_dot(wsg, vn[GMLP_CHUNK * cc:GMLP_CHUNK * (cc + 1), LANES * gl:LANES * (gl + 1)])
                 for cc in range(n_time)], axis=0))
        sb = sb_ref[:, cols]
        sv = jnp.concatenate(svs, axis=1) + jnp.concatenate([sb] * n_time, axis=0)
        u = gelu(proj(u0 + W * j))
        yb = yb + _dot((u * sv).astype(BF), wog_ref[W * j:W * (j + 1), :])

    g0 = C_RWKV_PAD + C_GMLP
    for j in range(N_QUADS):
        cols = slice(W * j, W * (j + 1))
        ga_ref[:, cols] = jax.nn.sigmoid(proj(g0 + W * j)).astype(BF)
        zb_ref[:, cols] = (jax.nn.sigmoid(proj(g0 + D_MODEL + W * j)) * yb[:, cols]).astype(BF)


def _front(x2, win, mu, w0, a0, w1, w2, k_k, k_a, r_k, g256, ltri, lnw, lnb, ws, sb, wog, seq):
    n_tok = x2.shape[0]
    n_tiles = n_tok // TM
    row = lambda i: (i, 0)
    quad_spec = pl.BlockSpec((N_QUADS, TM, QUAD), lambda i: (0, i, 0))
    quad_shape = jax.ShapeDtypeStruct((N_QUADS, n_tok, QUAD), BF)
    full_spec = pl.BlockSpec((TM, D_MODEL), row)
    full_shape = jax.ShapeDtypeStruct((n_tok, D_MODEL), BF)
    wc_spec = pl.BlockSpec((TM // SCAN_CHUNK, N_QUADS, QUAD), lambda i: (i, 0, 0))
    wc_shape = jax.ShapeDtypeStruct((n_tok // SCAN_CHUNK, N_QUADS, QUAD), F32)
    consts = (win, mu, w0, a0, w1, w2, k_k, k_a, r_k, g256, ltri, lnw, lnb, ws, sb, wog)
    return pl.pallas_call(
        functools.partial(_front_kernel, tiles_per_seq=seq // TM),
        grid=(n_tiles,),
        in_specs=[pl.BlockSpec((TM, D_MODEL), row)] + [_const_spec(c.shape) for c in consts],
        out_specs=[quad_spec] * 7 + [wc_spec] + [full_spec] * 4,
        out_shape=[quad_shape] * 7 + [wc_shape] + [full_shape] * 4,
        scratch_shapes=[pltpu.VMEM((1, C_RWKV_PAD), F32), pltpu.VMEM((TM, D_MODEL), BF)],
        compiler_params=pltpu.CompilerParams(
            dimension_semantics=("arbitrary",), vmem_limit_bytes=V7X_VMEM_LIMIT_BYTES),
        name="front",
    )(x2, *consts)


def _scan_kernel(rt_ref, at_ref, bt_ref, kt_ref, vv_ref, bh_ref, kh_ref, wc_ref,
                 g_ref, bg_ref, ga_ref, zb_ref, x_ref, gnw_ref, gnb_ref, g256_ref,
                 wor_ref, wout_ref, ln1w_ref, ln1b_ref,
                 h_ref, hp_ref, s_ref, *, tiles_per_seq):
    i = pl.program_id(0)

    @pl.when((i % tiles_per_seq) == 0)
    def _():
        s_ref[...] = jnp.zeros_like(s_ref)

    lane = lax.broadcasted_iota(jnp.int32, (1, QUAD), 1)
    lane_head = lane // HEAD_DIM
    lane_pos = lane % HEAD_DIM
    trow = lax.broadcasted_iota(jnp.int32, (SCAN_CHUNK, 1), 0)
    strict = lane_pos < trow
    incl = lane_pos <= trow
    row_head = lax.broadcasted_iota(jnp.int32, (QUAD, 1), 0) // HEAD_DIM
    diag_blocks = row_head == lane_head

    def bd4(m):
        zero = jnp.zeros_like(m)
        return jnp.concatenate([jnp.where(lane_head == j, m, zero) for j in range(4)], axis=0)

    n_chunks = SCAN_TM // SCAN_CHUNK
    items = [(c, q) for c in range(n_chunks) for q in range(N_QUADS)]
    rows_of = lambda c: slice(SCAN_CHUNK * c, SCAN_CHUNK * (c + 1))
    C = SCAN_CHUNK

    at, rt, bdv, a_ab, a_akrk, a_rb = {}, {}, {}, {}, {}, {}
    for it in items:
        c, q = it
        at[it] = at_ref[q, rows_of(c), :]
        rt[it] = rt_ref[q, rows_of(c), :]
        ra = jnp.concatenate([at[it], rt[it]], axis=0)
        gb = _dot_nt(ra, bd4(bt_ref[q, rows_of(c), :]))
        gk = _dot_nt(ra, bd4(kt_ref[q, rows_of(c), :]))
        a_ab[it] = jnp.where(strict, gb[:C], 0.0)
        a_rb[it] = jnp.where(incl, gb[C:], 0.0).astype(BF)
        a_akrk[it] = jnp.concatenate(
            [jnp.where(strict, gk[:C], 0.0), jnp.where(incl, gk[C:], 0.0)], axis=0).astype(BF)
        bdv[it] = bd4(vv_ref[q, rows_of(c), :])

    tm, qpow = {}, {}
    for it in items:
        qb = a_ab[it].astype(BF)
        tm[it] = a_ab[it]
        qpow[it] = _dot(qb, bd4(qb))
    for level in range(1, 6):
        for it in items:
            qb = qpow[it].astype(BF)
            bdq = bd4(qb)
            if level < 5:
                both = _dot(jnp.concatenate([tm[it].astype(BF), qb], axis=0), bdq)
                tm[it] = tm[it] + qpow[it] + both[:C]
                qpow[it] = both[C:]
            else:
                tm[it] = tm[it] + qpow[it] + _dot(tm[it].astype(BF), bdq)

    awr, u0, rkv = {}, {}, {}
    for it in items:
        tb = tm[it].astype(BF)
        aw = at[it].astype(F32) + _dot(tb, bd4(at[it]))
        awr[it] = jnp.concatenate([aw.astype(BF), rt[it]], axis=0)
        kv = _dot(a_akrk[it], bdv[it])
        u0[it] = kv[:C] + _dot(tb, bd4(kv[:C].astype(BF)))
        rkv[it] = kv[C:]

    state = [s_ref[q] for q in range(N_QUADS)]
    ys = {}
    for c in range(n_chunks):
        for q in range(N_QUADS):
            it = (c, q)
            us = _dot_nt(awr[it], state[q].astype(BF))
            ub = (u0[it] + us[:C]).astype(BF)
            ys[it] = us[C:] + rkv[it] + _dot(a_rb[it], bd4(ub))
            uv = jnp.concatenate([ub, vv_ref[q, rows_of(c), :]], axis=0)
            bk = jnp.concatenate([bh_ref[q, rows_of(c), :], kh_ref[q, rows_of(c), :]], axis=0)
            wc = wc_ref[c, q:q + 1, :]
            state[q] = state[q] * wc + jnp.where(diag_blocks, _dot_tn(uv, bk), 0.0)
    for q in range(N_QUADS):
        s_ref[q] = state[q]

    g256 = g256_ref[...]
    y = jnp.concatenate(
        [jnp.concatenate([ys[(c, q)] for c in range(n_chunks)], axis=0) for q in range(N_QUADS)], axis=1)
    inv_n = 1.0 / HEAD_DIM
    d = y - _headsum(y, g256) * inv_n
    var = _headsum(d * d, g256) * inv_n
    yn = d * lax.rsqrt(var + GN_EPS) * gnw_ref[...] + gnb_ref[...]
    ya = _dot((yn * g_ref[...].astype(F32) + bg_ref[...].astype(F32)).astype(BF), wor_ref[...])
    mix_in = ga_ref[...].astype(F32) * ya + zb_ref[...].astype(F32)
    mix = _dot(mix_in.astype(BF), wout_ref[...])
    h = _layer_norm(ALPHA * x_ref[...] + mix, ln1w_ref[...], ln1b_ref[...])
    h_ref[...] = h
    hb = pltpu.bitcast(h.astype(BF).astype(F32), jnp.uint32)
    half = D_MODEL // 2
    hp_ref[...] = (hb[:, :half] >> 16) | (hb[:, half:] & jnp.uint32(0xFFFF0000))


def _scan(quads, wc, g, bg, ga, zb, x2, gnw, gnb, g256, wor, wout, ln1w, ln1b, seq):
    n_tok = x2.shape[0]
    n_tiles = n_tok // SCAN_TM
    row = lambda i: (i, 0)
    quad_spec = pl.BlockSpec((N_QUADS, SCAN_TM, QUAD), lambda i: (0, i, 0))
    full_spec = pl.BlockSpec((SCAN_TM, D_MODEL), row)
    wc_spec = pl.BlockSpec((SCAN_TM // SCAN_CHUNK, N_QUADS, QUAD), lambda i: (i, 0, 0))
    consts = (gnw, gnb, g256, wor, wout, ln1w, ln1b)
    return pl.pallas_call(
        functools.partial(_scan_kernel, tiles_per_seq=seq // SCAN_TM),
        grid=(n_tiles,),
        in_specs=[quad_spec] * 7 + [wc_spec]
        + [full_spec] * 5 + [_const_spec(c.shape) for c in consts],
        out_specs=[full_spec, pl.BlockSpec((SCAN_TM, D_MODEL // 2), row)],
        out_shape=[jax.ShapeDtypeStruct((n_tok, D_MODEL), F32),
                   jax.ShapeDtypeStruct((n_tok, D_MODEL // 2), jnp.uint32)],
        scratch_shapes=[pltpu.VMEM((N_QUADS, QUAD, QUAD), F32)],
        compiler_params=pltpu.CompilerParams(
            dimension_semantics=("arbitrary",), vmem_limit_bytes=V7X_VMEM_LIMIT_BYTES),
        name="scan",
    )(*quads, wc, g, bg, ga, zb, x2, *consts)


def _first_max(x, idx, n):
    m = jnp.max(x, axis=0, keepdims=True)
    first = jnp.min(jnp.where(x == m, idx, n), axis=0, keepdims=True)
    return m, first


def _route_kernel(h_ref, wrh_ref, wrl_ref, bias_ref, ustrict_ref, e_ref, gate_ref, rank_ref, cnt_ref,
                  carry_ref):
    @pl.when(pl.program_id(0) == 0)
    def _():
        carry_ref[...] = jnp.zeros_like(carry_ref)

    hh, hl = _split2(h_ref[...])
    wrh = wrh_ref[...]
    logits = _dot_nt(wrh, hh) + _dot_nt(wrh, hl) + _dot_nt(wrl_ref[...], hh)
    scores = jax.nn.sigmoid(logits)
    choice = scores + bias_ref[...]
    n_tok = choice.shape[1]
    per = N_EXPERTS // N_EXPERT_GROUPS
    eidx = lax.broadcasted_iota(jnp.int32, (N_EXPERTS, 1), 0)
    gidx = lax.broadcasted_iota(jnp.int32, (N_EXPERT_GROUPS, 1), 0)
    lidx = lax.broadcasted_iota(jnp.int32, (per, 1), 0)
    neg = -jnp.inf

    gsc = []
    for gi in range(N_EXPERT_GROUPS):
        blk = choice[per * gi:per * (gi + 1), :]
        m1, i1 = _first_max(blk, lidx, per)
        m2 = jnp.max(jnp.where(lidx == i1, neg, blk), axis=0, keepdims=True)
        gsc.append(m1 + m2)
    gscore = jnp.concatenate(gsc, axis=0)
    gsel = jnp.zeros(gscore.shape, jnp.bool_)
    for _ in range(TOPK_GROUPS):
        _, gi1 = _first_max(gscore, gidx, N_EXPERT_GROUPS)
        hit = gidx == gi1
        gsel = gsel | hit
        gscore = jnp.where(hit, neg, gscore)
    emask = jnp.concatenate(
        [jnp.broadcast_to(gsel[gi:gi + 1, :], (per, n_tok)) for gi in range(N_EXPERT_GROUPS)], axis=0)
    cand = jnp.where(emask, choice, neg)
    es, gs, hits = [], [], []
    for _ in range(TOP_K):
        _, e1 = _first_max(cand, eidx, N_EXPERTS)
        hit = eidx == e1
        es.append(e1)
        hits.append(hit)
        gs.append(jnp.sum(jnp.where(hit, scores, 0.0), axis=0, keepdims=True))
        cand = jnp.where(hit, neg, cand)
    gates = jnp.concatenate(gs, axis=0)
    gates = gates / jnp.sum(gates, axis=0, keepdims=True) * ROUTED_SCALE
    e_ref[...] = jnp.concatenate(es, axis=0)
    gate_ref[...] = gates

    chosen = functools.reduce(jnp.logical_or, hits)
    sel = jnp.where(chosen, 1.0, 0.0)
    before = _dot(sel.astype(BF), ustrict_ref[...]) + carry_ref[:, 0:1]
    rank_ref[...] = jnp.concatenate(
        [jnp.sum(jnp.where(hit, before, 0.0), axis=0, keepdims=True) for hit in hits],
        axis=0).astype(jnp.int32)
    total = before[:, n_tok - 1:n_tok] + sel[:, n_tok - 1:n_tok]
    carry_ref[...] = jnp.broadcast_to(total, carry_ref.shape)
    cnt_ref[...] = jnp.broadcast_to(total, cnt_ref.shape).astype(jnp.int32)


def _route(h, wrh, wrl, bias_col, ustrict):
    n_tok = h.shape[0]
    col = lambda i: (0, i)
    tok_spec = pl.BlockSpec((TOP_K, ROUTE_TM), col)
    return pl.pallas_call(
        _route_kernel,
        grid=(n_tok // ROUTE_TM,),
        in_specs=[pl.BlockSpec((ROUTE_TM, D_MODEL), lambda i: (i, 0)),
                  _const_spec(wrh.shape), _const_spec(wrl.shape), _const_spec(bias_col.shape),
                  _const_spec(ustrict.shape)],
        out_specs=[tok_spec, tok_spec, tok_spec, pl.BlockSpec((N_EXPERTS, LANES), lambda i: (0, 0))],
        out_shape=[jax.ShapeDtypeStruct((TOP_K, n_tok), jnp.int32),
                   jax.ShapeDtypeStruct((TOP_K, n_tok), F32),
                   jax.ShapeDtypeStruct((TOP_K, n_tok), jnp.int32),
                   jax.ShapeDtypeStruct((N_EXPERTS, LANES), jnp.int32)],
        scratch_shapes=[pltpu.VMEM((N_EXPERTS, LANES), F32)],
        compiler_params=pltpu.CompilerParams(dimension_semantics=("arbitrary",)),
        name="route",
    )(h, wrh, wrl, bias_col, ustrict)


def _rows_kernel(e_ref, rank_ref, base_ref, pos_ref):
    eidx = lax.broadcasted_iota(jnp.int32, (N_EXPERTS, 1), 0)
    base = base_ref[...]
    pos_ref[...] = rank_ref[...] + jnp.concatenate(
        [jnp.sum(jnp.where(eidx == e_ref[kk:kk + 1, :], base, 0), axis=0, keepdims=True)
         for kk in range(TOP_K)], axis=0)


def _rows(top_e, rank, base_col):
    n_tok = top_e.shape[1]
    tok_spec = pl.BlockSpec((TOP_K, ROUTE_TM), lambda i: (0, i))
    return pl.pallas_call(
        _rows_kernel,
        grid=(n_tok // ROUTE_TM,),
        in_specs=[tok_spec, tok_spec, _const_spec(base_col.shape)],
        out_specs=tok_spec,
        out_shape=jax.ShapeDtypeStruct((TOP_K, n_tok), jnp.int32),
        compiler_params=pltpu.CompilerParams(dimension_semantics=("arbitrary",)),
        name="rows",
    )(top_e, rank, base_col)


SC_WINDOW = 128


def _sc_mesh():
    return plsc.VectorSubcoreMesh(core_axis_name="c", subcore_axis_name="s")


def _sc_workers():
    info = plsc.get_sparse_core_info()
    return info.num_cores, info.num_cores * info.num_subcores


def _sc_scatter_rows(src, pos, n_rows):
    n_tok, width = src.shape
    n_k = pos.shape[0]
    n_cores, n_workers = _sc_workers()
    n_win = n_tok // (n_workers * SC_WINDOW)

    @functools.partial(
        pl.kernel, out_type=jax.ShapeDtypeStruct((n_rows, width), src.dtype), mesh=_sc_mesh(),
        scratch_types=[pltpu.VMEM((n_k * n_win, SC_WINDOW), jnp.int32),
                       pltpu.VMEM((SC_WINDOW, width), src.dtype),
                       pltpu.SemaphoreType.DMA])
    def scatter(src_hbm, pos_hbm, out_hbm, idx_v, rows_v, sem):
        wid = lax.axis_index("s") * n_cores + lax.axis_index("c")
        pltpu.sync_copy(pos_hbm.at[wid], idx_v)

        @pl.loop(0, n_win)
        def _(j):
            pltpu.sync_copy(src_hbm.at[pl.ds((wid * n_win + j) * SC_WINDOW, SC_WINDOW)], rows_v)
            copies = [pltpu.async_copy(rows_v, out_hbm.at[idx_v.at[kk * n_win + j]], sem)
                      for kk in range(n_k)]
            for cp in copies:
                cp.wait()

    pos4 = pos.reshape(n_k, n_workers, n_win, SC_WINDOW).transpose(1, 0, 2, 3)
    return scatter(src, pos4.reshape(n_workers, n_k * n_win, SC_WINDOW))


def _sc_gather_rows(src, idx):
    n_idx = idx.shape[0]
    width = src.shape[1]
    n_cores, n_workers = _sc_workers()
    n_win = n_idx // (n_workers * SC_WINDOW)

    @functools.partial(
        pl.kernel, out_type=jax.ShapeDtypeStruct((n_idx, width), src.dtype), mesh=_sc_mesh(),
        scratch_types=[pltpu.VMEM((n_win, SC_WINDOW), jnp.int32),
                       pltpu.VMEM((SC_WINDOW, width), src.dtype),
                       pltpu.SemaphoreType.DMA])
    def gather(src_hbm, idx_hbm, out_hbm, idx_v, rows_v, sem):
        wid = lax.axis_index("s") * n_cores + lax.axis_index("c")
        pltpu.sync_copy(idx_hbm.at[wid], idx_v)

        @pl.loop(0, n_win)
        def _(j):
            pltpu.async_copy(src_hbm.at[idx_v.at[j]], rows_v, sem).wait()
            pltpu.sync_copy(rows_v, out_hbm.at[pl.ds((wid * n_win + j) * SC_WINDOW, SC_WINDOW)])

    return gather(src, idx.reshape(n_workers, n_win, SC_WINDOW))


def _unpack_pair(words):
    lo = pltpu.bitcast(words << 16, F32).astype(BF)
    hi = pltpu.bitcast(words & jnp.uint32(0xFFFF0000), F32).astype(BF)
    return lo, hi


def _pack_pair(val):
    bits = pltpu.bitcast(val.astype(BF).astype(F32), jnp.uint32)
    half = val.shape[1] // 2
    return (bits[:, :half] >> 16) | (bits[:, half:] & jnp.uint32(0xFFFF0000))


FFN_SLOTS = 6
FFN_AHEAD = FFN_SLOTS - 1


def _ffn_kernel(base_ref, cnt_ref, xs_ref, wgu_ref, wd_ref, out_ref,
                xbuf, obuf, wgu16_ref, wd16_ref, sem_in, sem_out, pend_ref, *, n_blocks):
    e = pl.program_id(0)
    n_rows_e = cnt_ref[e]
    n_blk = (n_rows_e + FFN_ROWS - 1) // FFN_ROWS
    first_blk = base_ref[e] // FFN_ROWS
    last_cnt = cnt_ref[N_EXPERTS - 1]
    used = base_ref[N_EXPERTS - 1] // FFN_ROWS + (last_cnt + FFN_ROWS - 1) // FFN_ROWS
    half = D_MODEL // 2

    def in_copy(blk, slot):
        row0 = pl.multiple_of(blk * FFN_ROWS, FFN_ROWS)
        return pltpu.make_async_copy(xs_ref.at[pl.ds(row0, FFN_ROWS), :], xbuf.at[slot], sem_in.at[slot])

    def out_copy(blk, slot):
        row0 = pl.multiple_of(blk * FFN_ROWS, FFN_ROWS)
        return pltpu.make_async_copy(obuf.at[slot], out_ref.at[pl.ds(row0, FFN_ROWS), :], sem_out.at[slot])

    @pl.when(e == 0)
    def _():
        for slot in range(FFN_SLOTS):
            pend_ref[slot] = 0
        for blk in range(FFN_AHEAD):
            @pl.when(blk < used)
            def _():
                in_copy(blk, blk).start()

    @pl.when(n_blk > 0)
    def _():
        wgu16_ref[...] = wgu_ref[0].astype(BF)
        wd16_ref[...] = wd_ref[0].astype(BF)

    def fetch_ahead(blk):
        @pl.when(blk + FFN_AHEAD < used)
        def _():
            in_copy(blk + FFN_AHEAD, (blk + FFN_AHEAD) % FFN_SLOTS).start()

    def ffn_rows(words, j):
        row = lax.broadcasted_iota(jnp.int32, (words.shape[0], 1), 0)
        lo, hi = _unpack_pair(jnp.where(row < n_rows_e - j * FFN_ROWS, words, jnp.uint32(0)))
        gu = _dot(lo, wgu16_ref[0:half, :]) + _dot(hi, wgu16_ref[half:, :])
        act = (jax.nn.silu(gu[:, :EXPERT_FF]) * gu[:, EXPERT_FF:]).astype(BF)
        return _pack_pair(_dot(act, wd16_ref[...]))

    def write_back(blk, slot, packed):
        @pl.when(pend_ref[slot] == 1)
        def _():
            out_copy(blk, slot).wait()

        obuf[slot] = packed
        out_copy(blk, slot).start()
        pend_ref[slot] = 1

    def single(j):
        blk = first_blk + j
        slot = blk % FFN_SLOTS
        in_copy(blk, slot).wait()
        fetch_ahead(blk)
        write_back(blk, slot, ffn_rows(xbuf[slot], j))

    def pair(p, carry):
        j = lead + 2 * p
        blk = first_blk + j
        slot = pl.multiple_of(blk % FFN_SLOTS, 2)
        in_copy(blk, slot).wait()
        in_copy(blk + 1, slot + 1).wait()
        fetch_ahead(blk)
        packed = ffn_rows(xbuf[pl.ds(slot, 2)].reshape(2 * FFN_ROWS, half), j)
        fetch_ahead(blk + 1)
        write_back(blk, slot, packed[:FFN_ROWS])
        write_back(blk + 1, slot + 1, packed[FFN_ROWS:])
        return carry

    lead = jnp.where(n_blk > 0, first_blk % 2, 0)

    @pl.when(lead == 1)
    def _():
        single(0)

    lax.fori_loop(0, (n_blk - lead) // 2, pair, 0)

    @pl.when((n_blk - lead) % 2 == 1)
    def _():
        single(n_blk - 1)

    @pl.when(e == N_EXPERTS - 1)
    def _():
        for slot in range(FFN_SLOTS):
            @pl.when(pend_ref[slot] == 1)
            def _():
                out_copy(0, slot).wait()
        obuf[0] = jnp.zeros((FFN_ROWS, half), jnp.uint32)

        def fill(blk, carry):
            cp = out_copy(blk, 0)
            cp.start()
            cp.wait()
            return carry

        lax.fori_loop(used, n_blocks, fill, 0)


def _ffn(base, counts, xs, wgu, wd):
    n_rows = xs.shape[0]
    half = D_MODEL // 2
    grid_spec = pltpu.PrefetchScalarGridSpec(
        num_scalar_prefetch=2,
        grid=(N_EXPERTS,),
        in_specs=[pl.BlockSpec(memory_space=pl.ANY),
                  pl.BlockSpec((1, D_MODEL, 2 * EXPERT_FF), lambda e, base, cnt: (e, 0, 0)),
                  pl.BlockSpec((1, EXPERT_FF, D_MODEL), lambda e, base, cnt: (e, 0, 0))],
        out_specs=pl.BlockSpec(memory_space=pl.ANY),
        scratch_shapes=[pltpu.VMEM((FFN_SLOTS, FFN_ROWS, half), jnp.uint32),
                        pltpu.VMEM((FFN_SLOTS, FFN_ROWS, half), jnp.uint32),
                        pltpu.VMEM((D_MODEL, 2 * EXPERT_FF), BF),
                        pltpu.VMEM((EXPERT_FF, D_MODEL), BF),
                        pltpu.SemaphoreType.DMA((FFN_SLOTS,)),
                        pltpu.SemaphoreType.DMA((FFN_SLOTS,)),
                        pltpu.SMEM((FFN_SLOTS,), jnp.int32)],
    )
    return pl.pallas_call(
        functools.partial(_ffn_kernel, n_blocks=n_rows // FFN_ROWS),
        grid_spec=grid_spec,
        out_shape=jax.ShapeDtypeStruct((n_rows, half), jnp.uint32),
        compiler_params=pltpu.CompilerParams(dimension_semantics=("arbitrary",)),
        name="ffn",
    )(base, counts, xs, wgu, wd)


def _final_kernel(rows_ref, gate_ref, h_ref, wsgu_ref, wsd_ref, ln2w_ref, ln2b_ref, o_ref):
    h = h_ref[...]
    hb = h.astype(BF)
    gu = _dot(hb, wsgu_ref[...])
    act = (jax.nn.silu(gu[:, :SHARED_FF]) * gu[:, SHARED_FF:]).astype(BF)
    shared = _dot(act, wsd_ref[...])

    half = D_MODEL // 2
    lo_acc = jnp.zeros((FINAL_TM, half), F32)
    hi_acc = jnp.zeros((FINAL_TM, half), F32)
    gates = gate_ref[...]
    for kk in range(TOP_K):
        words = rows_ref[kk]
        gk = gates[:, kk:kk + 1]
        lo_acc = lo_acc + gk * pltpu.bitcast(words << 16, F32)
        hi_acc = hi_acc + gk * pltpu.bitcast(words & jnp.uint32(0xFFFF0000), F32)
    routed = jnp.concatenate([lo_acc, hi_acc], axis=1)
    o_ref[...] = _layer_norm(ALPHA * h + routed + shared, ln2w_ref[...], ln2b_ref[...])


def _final(rows, gates_t, h, wsgu, wsd, ln2w, ln2b):
    n_tok = h.shape[0]
    n_tiles = n_tok // FINAL_TM
    half = D_MODEL // 2
    row = lambda i: (i, 0)
    consts = (wsgu, wsd, ln2w, ln2b)
    return pl.pallas_call(
        _final_kernel,
        grid=(n_tiles,),
        in_specs=[pl.BlockSpec((TOP_K, FINAL_TM, half), lambda i: (0, i, 0)),
                  pl.BlockSpec((FINAL_TM, TOP_K), row),
                  pl.BlockSpec((FINAL_TM, D_MODEL), row)] + [_const_spec(c.shape) for c in consts],
        out_specs=pl.BlockSpec((FINAL_TM, D_MODEL), row),
        out_shape=jax.ShapeDtypeStruct((n_tok, D_MODEL), F32),
        compiler_params=pltpu.CompilerParams(dimension_semantics=("arbitrary",)),
        name="final",
    )(rows, gates_t, h, *consts)


def _np_consts():
    idx = np.arange(QUAD)
    g256 = (idx[:, None] // HEAD_DIM == idx[None, :] // HEAD_DIM).astype(np.float32)
    t = np.arange(CUM_TILE)
    same = t[:, None] // SCAN_CHUNK == t[None, :] // SCAN_CHUNK
    ltri = (same & (t[None, :] <= t[:, None])).astype(np.float32)
    return jnp.asarray(g256, BF), jnp.asarray(ltri, BF)


def _block_plan(counts, n_assign):
    padded = (counts + FFN_ROWS - 1) // FFN_ROWS * FFN_ROWS
    base = jnp.cumsum(padded) - padded
    n_blocks = (n_assign + N_EXPERTS * (FFN_ROWS - 1) + FFN_ROWS - 1) // FFN_ROWS
    return base.astype(jnp.int32), n_blocks * FFN_ROWS


def _layer(x2, seq, w_in, shift_mu, w0, decay_lora, a0, iclr_lora, gate_lora, k_k, k_a, r_k, gn_w, gn_b,
           lnw, lnb, spatial_w, spatial_b, w_o_rwkv, w_o_gmlp, w_out, ln1_w, ln1_b,
           w_router, router_bias, w_exp_gate_up, w_exp_down, w_sh_gate_up, w_sh_down, ln2_w, ln2_b):
    n_tok = x2.shape[0]
    c_rwkv = C_RKV + C_LORA
    pad = C_LORA_PAD - C_LORA
    win = jnp.concatenate([w_in[:, :c_rwkv], jnp.zeros((D_MODEL, pad), F32), w_in[:, c_rwkv:]], axis=1).astype(BF)
    mu = jnp.pad(shift_mu, (0, pad)).reshape(1, C_RWKV_PAD)
    row1 = lambda t: t.reshape(1, -1)
    w1 = jnp.zeros((LANES, 2 * D_MODEL), F32)
    w1 = w1.at[:DECAY_LORA, :D_MODEL].set(decay_lora).at[DECAY_LORA:, D_MODEL:].set(iclr_lora).astype(BF)
    w2 = jnp.zeros((C_LORA_PAD - LANES, D_MODEL), F32).at[:GATE_LORA].set(gate_lora).astype(BF)
    g256, ltri = _np_consts()
    sb_map = jnp.repeat(spatial_b.T, D_MODEL // GMLP_GROUPS, axis=1)

    outs = _front(x2, win, mu, row1(w0), row1(a0), w1, w2, row1(k_k), row1(k_a), row1(r_k),
                  g256, ltri, row1(lnw), row1(lnb), spatial_w, sb_map, w_o_gmlp.astype(BF), seq)
    quads, wc, (g, bg, ga, zb) = outs[:7], outs[7], outs[8:]
    h1, h1p = _scan(quads, wc, g, bg, ga, zb, x2, row1(gn_w), row1(gn_b), g256,
                    w_o_rwkv.astype(BF), w_out.astype(BF), row1(ln1_w), row1(ln1_b), seq)

    wrt = w_router.T
    wrh = wrt.astype(BF)
    wrl = (wrt - wrh.astype(F32)).astype(BF)
    tt = np.arange(ROUTE_TM)
    ustrict = jnp.asarray((tt[:, None] < tt[None, :]).astype(np.float32), BF)
    top_e, gates, rank, counts = _route(h1, wrh, wrl, router_bias.reshape(N_EXPERTS, 1), ustrict)
    counts = counts[:, 0]
    base, n_rows = _block_plan(counts, n_tok * TOP_K)
    pos = _rows(top_e, rank, base.reshape(N_EXPERTS, 1))
    xs = _sc_scatter_rows(h1p, pos, n_rows)
    rows = _ffn(base, counts, xs, w_exp_gate_up, w_exp_down)
    picked = _sc_gather_rows(rows, pos.reshape(-1)).reshape(TOP_K, n_tok, D_MODEL // 2)
    return _final(picked, gates.T, h1, w_sh_gate_up.astype(BF), w_sh_down.astype(BF),
                  row1(ln2_w), row1(ln2_b))


def kernel(x, w_in, shift_mu, rwkv_w0, rwkv_decay_lora, rwkv_a0, rwkv_iclr_lora, rwkv_gate_lora, rwkv_k_k, rwkv_k_a, rwkv_r_k, rwkv_gn_w, rwkv_gn_b, gmlp_ln_w, gmlp_ln_b, gmlp_spatial_w, gmlp_spatial_b, w_o_rwkv, w_o_gmlp, w_out, ln1_w, ln1_b, w_router, router_bias, w_exp_gate_up, w_exp_down, w_sh_gate_up, w_sh_down, ln2_w, ln2_b):
    bsz, seq, d = x.shape
    h = x.reshape(bsz * seq, d)
    for l in range(DEPTH):
        h = _layer(h, seq, w_in[l], shift_mu[l], rwkv_w0[l], rwkv_decay_lora[l], rwkv_a0[l],
                   rwkv_iclr_lora[l], rwkv_gate_lora[l], rwkv_k_k[l], rwkv_k_a[l], rwkv_r_k[l],
                   rwkv_gn_w[l], rwkv_gn_b[l], gmlp_ln_w[l], gmlp_ln_b[l], gmlp_spatial_w[l],
                   gmlp_spatial_b[l], w_o_rwkv[l], w_o_gmlp[l], w_out[l], ln1_w[l], ln1_b[l],
                   w_router[l], router_bias[l], w_exp_gate_up[l], w_exp_down[l], w_sh_gate_up[l],
                   w_sh_down[l], ln2_w[l], ln2_b[l])
    return h.reshape(bsz, seq, d)
```

```python
import functools

import numpy as np
import jax
import jax.numpy as jnp
from jax import lax
from jax.experimental import pallas as pl
from jax.experimental.pallas import tpu as pltpu
from jax.experimental.pallas import tpu_sc as plsc

F32 = jnp.float32
BF = jnp.bfloat16

D_MODEL = 1024
HEAD_DIM = 64
N_HEADS = 16
DECAY_LORA = 64
ICLR_LORA = 64
GATE_LORA = 160
GN_EPS = 64e-5
GMLP_GROUPS = 8
GMLP_CHUNK = 128
N_EXPERTS = 256
TOP_K = 8
N_EXPERT_GROUPS = 8
TOPK_GROUPS = 4
EXPERT_FF = 256
SHARED_FF = 256
ROUTED_SCALE = 2.5
LN_EPS = 1e-5
DEPTH = 1
ALPHA = (2 * DEPTH) ** 0.25

LANES = 128
V7X_VMEM_LIMIT_BYTES = 56 * 1024 * 1024

C_RKV = 3 * D_MODEL
C_LORA = DECAY_LORA + ICLR_LORA + GATE_LORA
C_LORA_PAD = 384
C_RWKV_PAD = C_RKV + C_LORA_PAD
C_GMLP = 2 * D_MODEL
C_GATES = 2 * D_MODEL
C_IN_PAD = C_RWKV_PAD + C_GMLP + C_GATES

SCAN_CHUNK = 64
QUAD = 4 * HEAD_DIM
N_QUADS = D_MODEL // QUAD
TM = 512
CUM_TILE = 256
SCAN_TM = 256
ROUTE_TM = 1024
FFN_ROWS = 256
FINAL_TM = 512


def _dot(a, b):
    return jnp.dot(a, b, preferred_element_type=F32)


def _dot_nt(a, b):
    return lax.dot_general(a, b, (((1,), (1,)), ((), ())), preferred_element_type=F32)


def _dot_tn(a, b):
    return lax.dot_general(a, b, (((0,), (0,)), ((), ())), preferred_element_type=F32)


def _split2(z):
    hi = z.astype(BF)
    lo = (z - hi.astype(F32)).astype(BF)
    return hi, lo


def _headsum(z, g256, passes=2):
    outs = []
    for j in range(N_QUADS):
        zj = z[:, QUAD * j:QUAD * (j + 1)]
        if passes == 1:
            outs.append(_dot(zj.astype(BF), g256))
        else:
            hi, lo = _split2(zj)
            outs.append(_dot(hi, g256) + _dot(lo, g256))
    return jnp.concatenate(outs, axis=1)


def _layer_norm(v, w, b):
    mu = jnp.mean(v, axis=-1, keepdims=True)
    d = v - mu
    var = jnp.mean(d * d, axis=-1, keepdims=True)
    return d * lax.rsqrt(var + LN_EPS) * w + b


def _const_spec(shape):
    nd = len(shape)
    return pl.BlockSpec(shape, lambda *_: (0,) * nd, pipeline_mode=pl.Buffered(1))


def _front_kernel(x_ref, win_ref, mu_ref, w0_ref, a0_ref, w1_ref, w2_ref, kk_ref, ka_ref, rk_ref,
                  g256_ref, ltri_ref, lnw_ref, lnb_ref, ws_ref, sb_ref, wog_ref,
                  rt_ref, at_ref, bt_ref, kt_ref, vv_ref, bh_ref, kh_ref, wc_ref,
                  g_ref, bg_ref, ga_ref, zb_ref, prev_ref, xb_ref, *, tiles_per_seq):
    i = pl.program_id(0)
    first = (i % tiles_per_seq) == 0
    xb_ref[...] = x_ref[...].astype(BF)
    is_row0 = lax.broadcasted_iota(jnp.int32, (TM, 1), 0) == 0
    W = QUAD

    def proj(lo, width=W):
        return _dot(xb_ref[...], win_ref[:, lo:lo + width])

    def shift_mix(lo, width=W):
        p = proj(lo, width)
        prev = jnp.where(first, 0.0, prev_ref[:, lo:lo + width])
        sh = jnp.where(is_row0, prev, pltpu.roll(p, 1, 0))
        prev_ref[:, lo:lo + width] = p[TM - 1:TM, :]
        return p + mu_ref[:, lo:lo + width] * (sh - p)

    lora = shift_mix(C_RKV, C_LORA_PAD)
    lane = lax.broadcasted_iota(jnp.int32, (1, LANES), 1)
    act1 = jnp.where(lane < DECAY_LORA, jnp.tanh(lora[:, :LANES]), lora[:, :LANES]).astype(BF)
    act2 = jax.nn.sigmoid(lora[:, LANES:]).astype(BF)
    g256 = g256_ref[...]
    for q in range(N_QUADS):
        c0 = W * q
        cols = slice(c0, c0 + W)
        r = shift_mix(c0)
        k = shift_mix(D_MODEL + c0)
        v = shift_mix(2 * D_MODEL + c0)
        w = -jax.nn.softplus(-(w0_ref[:, cols] + _dot(act1, w1_ref[:, cols]))) - 0.5
        logd = -jnp.exp(w)
        a = jax.nn.sigmoid(a0_ref[:, cols] + _dot(act1, w1_ref[:, D_MODEL + c0:D_MODEL + c0 + W]))
        g = _dot(act2, w2_ref[:, cols])

        kkr = k * kk_ref[:, cols]
        kk = kkr * lax.rsqrt(jnp.maximum(_dot((kkr * kkr).astype(BF), g256), 1e-24))
        kmod = k * (1.0 + (a - 1.0) * ka_ref[:, cols])
        bonus = _dot((r * kmod * rk_ref[:, cols]).astype(BF), g256) * v
        kka = kk * a

        parts = _split2(logd)
        cum = jnp.concatenate(
            [sum(_dot(ltri_ref[...], part[CUM_TILE * hh:CUM_TILE * (hh + 1)]) for part in parts)
             for hh in range(TM // CUM_TILE)], axis=0)
        e_inv = jnp.exp(-cum)
        bt = kka * e_inv
        kt = kmod * e_inv
        ends = []
        for c in range(TM // SCAN_CHUNK):
            last = SCAN_CHUNK * (c + 1) - 1
            ewc_row = jnp.exp(cum[last:last + 1, :])
            wc_ref[c, q:q + 1, :] = ewc_row
            ends.append(jnp.broadcast_to(ewc_row, (SCAN_CHUNK, W)))
        ewc = jnp.concatenate(ends, axis=0)
        rt_ref[q] = (r * jnp.exp(cum)).astype(BF)
        at_ref[q] = (-kk * jnp.exp(cum - logd)).astype(BF)
        bt_ref[q] = bt.astype(BF)
        kt_ref[q] = kt.astype(BF)
        vv_ref[q] = v.astype(BF)
        bh_ref[q] = (bt * ewc).astype(BF)
        kh_ref[q] = (kt * ewc).astype(BF)
        g_ref[:, cols] = g.astype(BF)
        bg_ref[:, cols] = (bonus * g).astype(BF)

    gelu = lambda t: 0.5 * t * (1.0 + lax.erf(t * np.float32(np.sqrt(0.5))))
    u0 = C_RWKV_PAD
    zv = [gelu(proj(u0 + D_MODEL + W * j)) for j in range(N_QUADS)]
    mean = sum(jnp.sum(z, axis=-1, keepdims=True) for z in zv) * (1.0 / D_MODEL)
    dv = [z - mean for z in zv]
    var = sum(jnp.sum(d * d, axis=-1, keepdims=True) for d in dv) * (1.0 / D_MODEL)
    rstd = lax.rsqrt(var + LN_EPS)
    trow = lax.broadcasted_iota(jnp.int32, (GMLP_CHUNK, GMLP_CHUNK), 0)
    tcol = lax.broadcasted_iota(jnp.int32, (GMLP_CHUNK, GMLP_CHUNK), 1)
    causal = tcol <= trow
    n_time = TM // GMLP_CHUNK
    yb = jnp.zeros((TM, D_MODEL), F32)
    for j in range(N_QUADS):
        cols = slice(W * j, W * (j + 1))
        vn = (dv[j] * rstd * lnw_ref[:, cols] + lnb_ref[:, cols]).astype(BF)
        svs = []
        for gl in range(W // LANES):
            wsg = jnp.where(causal, ws_ref[(W // LANES) * j + gl], 0.0).astype(BF)
            svs.append(jnp.concatenate(
                [_dot(wsg, vn[GMLP_CHUNK * cc:GMLP_CHUNK * (cc + 1), LANES * gl:LANES * (gl + 1)])
                 for cc in range(n_time)], axis=0))
        sb = sb_ref[:, cols]
        sv = jnp.concatenate(svs, axis=1) + jnp.concatenate([sb] * n_time, axis=0)
        u = gelu(proj(u0 + W * j))
        yb = yb + _dot((u * sv).astype(BF), wog_ref[W * j:W * (j + 1), :])

    g0 = C_RWKV_PAD + C_GMLP
    for j in range(N_QUADS):
        cols = slice(W * j, W * (j + 1))
        ga_ref[:, cols] = jax.nn.sigmoid(proj(g0 + W * j)).astype(BF)
        zb_ref[:, cols] = (jax.nn.sigmoid(proj(g0 + D_MODEL + W * j)) * yb[:, cols]).astype(BF)


def _front(x2, win, mu, w0, a0, w1, w2, k_k, k_a, r_k, g256, ltri, lnw, lnb, ws, sb, wog, seq):
    n_tok = x2.shape[0]
    n_tiles = n_tok // TM
    row = lambda i: (i, 0)
    quad_spec = pl.BlockSpec((N_QUADS, TM, QUAD), lambda i: (0, i, 0))
    quad_shape = jax.ShapeDtypeStruct((N_QUADS, n_tok, QUAD), BF)
    full_spec = pl.BlockSpec((TM, D_MODEL), row)
    full_shape = jax.ShapeDtypeStruct((n_tok, D_MODEL), BF)
    wc_spec = pl.BlockSpec((TM // SCAN_CHUNK, N_QUADS, QUAD), lambda i: (i, 0, 0))
    wc_shape = jax.ShapeDtypeStruct((n_tok // SCAN_CHUNK, N_QUADS, QUAD), F32)
    consts = (win, mu, w0, a0, w1, w2, k_k, k_a, r_k, g256, ltri, lnw, lnb, ws, sb, wog)
    return pl.pallas_call(
        functools.partial(_front_kernel, tiles_per_seq=seq // TM),
        grid=(n_tiles,),
        in_specs=[pl.BlockSpec((TM, D_MODEL), row)] + [_const_spec(c.shape) for c in consts],
        out_specs=[quad_spec] * 7 + [wc_spec] + [full_spec] * 4,
        out_shape=[quad_shape] * 7 + [wc_shape] + [full_shape] * 4,
        scratch_shapes=[pltpu.VMEM((1, C_RWKV_PAD), F32), pltpu.VMEM((TM, D_MODEL), BF)],
        compiler_params=pltpu.CompilerParams(
            dimension_semantics=("arbitrary",), vmem_limit_bytes=V7X_VMEM_LIMIT_BYTES),
        name="front",
    )(x2, *consts)


def _scan_kernel(rt_ref, at_ref, bt_ref, kt_ref, vv_ref, bh_ref, kh_ref, wc_ref,
                 g_ref, bg_ref, ga_ref, zb_ref, x_ref, gnw_ref, gnb_ref, g256_ref,
                 wor_ref, wout_ref, ln1w_ref, ln1b_ref,
                 h_ref, hp_ref, s_ref, *, tiles_per_seq):
    i = pl.program_id(0)

    @pl.when((i % tiles_per_seq) == 0)
    def _():
        s_ref[...] = jnp.zeros_like(s_ref)

    lane = lax.broadcasted_iota(jnp.int32, (1, QUAD), 1)
    lane_head = lane // HEAD_DIM
    lane_pos = lane % HEAD_DIM
    trow = lax.broadcasted_iota(jnp.int32, (SCAN_CHUNK, 1), 0)
    strict = lane_pos < trow
    incl = lane_pos <= trow
    row_head = lax.broadcasted_iota(jnp.int32, (QUAD, 1), 0) // HEAD_DIM
    diag_blocks = row_head == lane_head

    def bd4(m):
        zero = jnp.zeros_like(m)
        return jnp.concatenate([jnp.where(lane_head == j, m, zero) for j in range(4)], axis=0)

    n_chunks = SCAN_TM // SCAN_CHUNK
    items = [(c, q) for c in range(n_chunks) for q in range(N_QUADS)]
    rows_of = lambda c: slice(SCAN_CHUNK * c, SCAN_CHUNK * (c + 1))
    C = SCAN_CHUNK

    at, rt, bdv, a_ab, a_akrk, a_rb = {}, {}, {}, {}, {}, {}
    for it in items:
        c, q = it
        at[it] = at_ref[q, rows_of(c), :]
        rt[it] = rt_ref[q, rows_of(c), :]
        ra = jnp.concatenate([at[it], rt[it]], axis=0)
        gb = _dot_nt(ra, bd4(bt_ref[q, rows_of(c), :]))
        gk = _dot_nt(ra, bd4(kt_ref[q, rows_of(c), :]))
        a_ab[it] = jnp.where(strict, gb[:C], 0.0)
        a_rb[it] = jnp.where(incl, gb[C:], 0.0).astype(BF)
        a_akrk[it] = jnp.concatenate(
            [jnp.where(strict, gk[:C], 0.0), jnp.where(incl, gk[C:], 0.0)], axis=0).astype(BF)
        bdv[it] = bd4(vv_ref[q, rows_of(c), :])

    tm, qpow = {}, {}
    for it in items:
        qb = a_ab[it].astype(BF)
        tm[it] = a_ab[it]
        qpow[it] = _dot(qb, bd4(qb))
    for level in range(1, 6):
        for it in items:
            qb = qpow[it].astype(BF)
            bdq = bd4(qb)
            if level < 5:
                both = _dot(jnp.concatenate([tm[it].astype(BF), qb], axis=0), bdq)
                tm[it] = tm[it] + qpow[it] + both[:C]
                qpow[it] = both[C:]
            else:
                tm[it] = tm[it] + qpow[it] + _dot(tm[it].astype(BF), bdq)

    awr, u0, rkv = {}, {}, {}
    for it in items:
        tb = tm[it].astype(BF)
        aw = at[it].astype(F32) + _dot(tb, bd4(at[it]))
        awr[it] = jnp.concatenate([aw.astype(BF), rt[it]], axis=0)
        kv = _dot(a_akrk[it], bdv[it])
        u0[it] = kv[:C] + _dot(tb, bd4(kv[:C].astype(BF)))
        rkv[it] = kv[C:]

    state = [s_ref[q] for q in range(N_QUADS)]
    ys = {}
    for c in range(n_chunks):
        for q in range(N_QUADS):
            it = (c, q)
            us = _dot_nt(awr[it], state[q].astype(BF))
            ub = (u0[it] + us[:C]).astype(BF)
            ys[it] = us[C:] + rkv[it] + _dot(a_rb[it], bd4(ub))
            uv = jnp.concatenate([ub, vv_ref[q, rows_of(c), :]], axis=0)
            bk = jnp.concatenate([bh_ref[q, rows_of(c), :], kh_ref[q, rows_of(c), :]], axis=0)
            wc = wc_ref[c, q:q + 1, :]
            state[q] = state[q] * wc + jnp.where(diag_blocks, _dot_tn(uv, bk), 0.0)
    for q in range(N_QUADS):
        s_ref[q] = state[q]

    g256 = g256_ref[...]
    y = jnp.concatenate(
        [jnp.concatenate([ys[(c, q)] for c in range(n_chunks)], axis=0) for q in range(N_QUADS)], axis=1)
    inv_n = 1.0 / HEAD_DIM
    d = y - _headsum(y, g256) * inv_n
    var = _headsum(d * d, g256) * inv_n
    yn = d * lax.rsqrt(var + GN_EPS) * gnw_ref[...] + gnb_ref[...]
    ya = _dot((yn * g_ref[...].astype(F32) + bg_ref[...].astype(F32)).astype(BF), wor_ref[...])
    mix_in = ga_ref[...].astype(F32) * ya + zb_ref[...].astype(F32)
    mix = _dot(mix_in.astype(BF), wout_ref[...])
    h = _layer_norm(ALPHA * x_ref[...] + mix, ln1w_ref[...], ln1b_ref[...])
    h_ref[...] = h
    hb = pltpu.bitcast(h.astype(BF).astype(F32), jnp.uint32)
    half = D_MODEL // 2
    hp_ref[...] = (hb[:, :half] >> 16) | (hb[:, half:] & jnp.uint32(0xFFFF0000))


def _scan(quads, wc, g, bg, ga, zb, x2, gnw, gnb, g256, wor, wout, ln1w, ln1b, seq):
    n_tok = x2.shape[0]
    n_tiles = n_tok // SCAN_TM
    row = lambda i: (i, 0)
    quad_spec = pl.BlockSpec((N_QUADS, SCAN_TM, QUAD), lambda i: (0, i, 0))
    full_spec = pl.BlockSpec((SCAN_TM, D_MODEL), row)
    wc_spec = pl.BlockSpec((SCAN_TM // SCAN_CHUNK, N_QUADS, QUAD), lambda i: (i, 0, 0))
    consts = (gnw, gnb, g256, wor, wout, ln1w, ln1b)
    return pl.pallas_call(
        functools.partial(_scan_kernel, tiles_per_seq=seq // SCAN_TM),
        grid=(n_tiles,),
        in_specs=[quad_spec] * 7 + [wc_spec]
        + [full_spec] * 5 + [_const_spec(c.shape) for c in consts],
        out_specs=[full_spec, pl.BlockSpec((SCAN_TM, D_MODEL // 2), row)],
        out_shape=[jax.ShapeDtypeStruct((n_tok, D_MODEL), F32),
                   jax.ShapeDtypeStruct((n_tok, D_MODEL // 2), jnp.uint32)],
        scratch_shapes=[pltpu.VMEM((N_QUADS, QUAD, QUAD), F32)],
        compiler_params=pltpu.CompilerParams(
            dimension_semantics=("arbitrary",), vmem_limit_bytes=V7X_VMEM_LIMIT_BYTES),
        name="scan",
    )(*quads, wc, g, bg, ga, zb, x2, *consts)


def _first_max(x, idx, n):
    m = jnp.max(x, axis=0, keepdims=True)
    first = jnp.min(jnp.where(x == m, idx, n), axis=0, keepdims=True)
    return m, first


def _route_kernel(h_ref, wrh_ref, wrl_ref, bias_ref, ustrict_ref, e_ref, gate_ref, rank_ref, cnt_ref,
                  carry_ref):
    @pl.when(pl.program_id(0) == 0)
    def _():
        carry_ref[...] = jnp.zeros_like(carry_ref)

    hh, hl = _split2(h_ref[...])
    wrh = wrh_ref[...]
    logits = _dot_nt(wrh, hh) + _dot_nt(wrh, hl) + _dot_nt(wrl_ref[...], hh)
    scores = jax.nn.sigmoid(logits)
    choice = scores + bias_ref[...]
    n_tok = choice.shape[1]
    per = N_EXPERTS // N_EXPERT_GROUPS
    eidx = lax.broadcasted_iota(jnp.int32, (N_EXPERTS, 1), 0)
    gidx = lax.broadcasted_iota(jnp.int32, (N_EXPERT_GROUPS, 1), 0)
    lidx = lax.broadcasted_iota(jnp.int32, (per, 1), 0)
    neg = -jnp.inf

    gsc = []
    for gi in range(N_EXPERT_GROUPS):
        blk = choice[per * gi:per * (gi + 1), :]
        m1, i1 = _first_max(blk, lidx, per)
        m2 = jnp.max(jnp.where(lidx == i1, neg, blk), axis=0, keepdims=True)
        gsc.append(m1 + m2)
    gscore = jnp.concatenate(gsc, axis=0)
    gsel = jnp.zeros(gscore.shape, jnp.bool_)
    for _ in range(TOPK_GROUPS):
        _, gi1 = _first_max(gscore, gidx, N_EXPERT_GROUPS)
        hit = gidx == gi1
        gsel = gsel | hit
        gscore = jnp.where(hit, neg, gscore)
    emask = jnp.concatenate(
        [jnp.broadcast_to(gsel[gi:gi + 1, :], (per, n_tok)) for gi in range(N_EXPERT_GROUPS)], axis=0)
    cand = jnp.where(emask, choice, neg)
    es, gs, hits = [], [], []
    for _ in range(TOP_K):
        _, e1 = _first_max(cand, eidx, N_EXPERTS)
        hit = eidx == e1
        es.append(e1)
        hits.append(hit)
        gs.append(jnp.sum(jnp.where(hit, scores, 0.0), axis=0, keepdims=True))
        cand = jnp.where(hit, neg, cand)
    gates = jnp.concatenate(gs, axis=0)
    gates = gates / jnp.sum(gates, axis=0, keepdims=True) * ROUTED_SCALE
    e_ref[...] = jnp.concatenate(es, axis=0)
    gate_ref[...] = gates

    chosen = functools.reduce(jnp.logical_or, hits)
    sel = jnp.where(chosen, 1.0, 0.0)
    before = _dot(sel.astype(BF), ustrict_ref[...]) + carry_ref[:, 0:1]
    rank_ref[...] = jnp.concatenate(
        [jnp.sum(jnp.where(hit, before, 0.0), axis=0, keepdims=True) for hit in hits],
        axis=0).astype(jnp.int32)
    total = before[:, n_tok - 1:n_tok] + sel[:, n_tok - 1:n_tok]
    carry_ref[...] = jnp.broadcast_to(total, carry_ref.shape)
    cnt_ref[...] = jnp.broadcast_to(total, cnt_ref.shape).astype(jnp.int32)


def _route(h, wrh, wrl, bias_col, ustrict):
    n_tok = h.shape[0]
    col = lambda i: (0, i)
    tok_spec = pl.BlockSpec((TOP_K, ROUTE_TM), col)
    return pl.pallas_call(
        _route_kernel,
        grid=(n_tok // ROUTE_TM,),
        in_specs=[pl.BlockSpec((ROUTE_TM, D_MODEL), lambda i: (i, 0)),
                  _const_spec(wrh.shape), _const_spec(wrl.shape), _const_spec(bias_col.shape),
                  _const_spec(ustrict.shape)],
        out_specs=[tok_spec, tok_spec, tok_spec, pl.BlockSpec((N_EXPERTS, LANES), lambda i: (0, 0))],
        out_shape=[jax.ShapeDtypeStruct((TOP_K, n_tok), jnp.int32),
                   jax.ShapeDtypeStruct((TOP_K, n_tok), F32),
                   jax.ShapeDtypeStruct((TOP_K, n_tok), jnp.int32),
                   jax.ShapeDtypeStruct((N_EXPERTS, LANES), jnp.int32)],
        scratch_shapes=[pltpu.VMEM((N_EXPERTS, LANES), F32)],
        compiler_params=pltpu.CompilerParams(dimension_semantics=("arbitrary",)),
        name="route",
    )(h, wrh, wrl, bias_col, ustrict)


def _rows_kernel(e_ref, rank_ref, base_ref, pos_ref):
    eidx = lax.broadcasted_iota(jnp.int32, (N_EXPERTS, 1), 0)
    base = base_ref[...]
    pos_ref[...] = rank_ref[...] + jnp.concatenate(
        [jnp.sum(jnp.where(eidx == e_ref[kk:kk + 1, :], base, 0), axis=0, keepdims=True)
         for kk in range(TOP_K)], axis=0)


def _rows(top_e, rank, base_col):
    n_tok = top_e.shape[1]
    tok_spec = pl.BlockSpec((TOP_K, ROUTE_TM), lambda i: (0, i))
    return pl.pallas_call(
        _rows_kernel,
        grid=(n_tok // ROUTE_TM,),
        in_specs=[tok_spec, tok_spec, _const_spec(base_col.shape)],
        out_specs=tok_spec,
        out_shape=jax.ShapeDtypeStruct((TOP_K, n_tok), jnp.int32),
        compiler_params=pltpu.CompilerParams(dimension_semantics=("arbitrary",)),
        name="rows",
    )(top_e, rank, base_col)


SC_WINDOW = 128


def _sc_mesh():
    return plsc.VectorSubcoreMesh(core_axis_name="c", subcore_axis_name="s")


def _sc_workers():
    info = plsc.get_sparse_core_info()
    return info.num_cores, info.num_cores * info.num_subcores


def _sc_scatter_rows(src, pos, n_rows):
    n_tok, width = src.shape
    n_k = pos.shape[0]
    n_cores, n_workers = _sc_workers()
    n_win = n_tok // (n_workers * SC_WINDOW)

    @functools.partial(
        pl.kernel, out_type=jax.ShapeDtypeStruct((n_rows, width), src.dtype), mesh=_sc_mesh(),
        scratch_types=[pltpu.VMEM((n_k * n_win, SC_WINDOW), jnp.int32),
                       pltpu.VMEM((SC_WINDOW, width), src.dtype),
                       pltpu.SemaphoreType.DMA])
    def scatter(src_hbm, pos_hbm, out_hbm, idx_v, rows_v, sem):
        wid = lax.axis_index("s") * n_cores + lax.axis_index("c")
        pltpu.sync_copy(pos_hbm.at[wid], idx_v)

        @pl.loop(0, n_win)
        def _(j):
            pltpu.sync_copy(src_hbm.at[pl.ds((wid * n_win + j) * SC_WINDOW, SC_WINDOW)], rows_v)
            copies = [pltpu.async_copy(rows_v, out_hbm.at[idx_v.at[kk * n_win + j]], sem)
                      for kk in range(n_k)]
            for cp in copies:
                cp.wait()

    pos4 = pos.reshape(n_k, n_workers, n_win, SC_WINDOW).transpose(1, 0, 2, 3)
    return scatter(src, pos4.reshape(n_workers, n_k * n_win, SC_WINDOW))


def _sc_gather_rows(src, idx):
    n_idx = idx.shape[0]
    width = src.shape[1]
    n_cores, n_workers = _sc_workers()
    n_win = n_idx // (n_workers * SC_WINDOW)

    @functools.partial(
        pl.kernel, out_type=jax.ShapeDtypeStruct((n_idx, width), src.dtype), mesh=_sc_mesh(),
        scratch_types=[pltpu.VMEM((n_win, SC_WINDOW), jnp.int32),
                       pltpu.VMEM((SC_WINDOW, width), src.dtype),
                       pltpu.SemaphoreType.DMA])
    def gather(src_hbm, idx_hbm, out_hbm, idx_v, rows_v, sem):
        wid = lax.axis_index("s") * n_cores + lax.axis_index("c")
        pltpu.sync_copy(idx_hbm.at[wid], idx_v)

        @pl.loop(0, n_win)
        def _(j):
            pltpu.async_copy(src_hbm.at[idx_v.at[j]], rows_v, sem).wait()
            pltpu.sync_copy(rows_v, out_hbm.at[pl.ds((wid * n_win + j) * SC_WINDOW, SC_WINDOW)])

    return gather(src, idx.reshape(n_workers, n_win, SC_WINDOW))


def _unpack_pair(words):
    lo = pltpu.bitcast(words << 16, F32).astype(BF)
    hi = pltpu.bitcast(words & jnp.uint32(0xFFFF0000), F32).astype(BF)
    return lo, hi


def _pack_pair(val):
    bits = pltpu.bitcast(val.astype(BF).astype(F32), jnp.uint32)
    half = val.shape[1] // 2
    return (bits[:, :half] >> 16) | (bits[:, half:] & jnp.uint32(0xFFFF0000))


FFN_SLOTS = 6
FFN_AHEAD = FFN_SLOTS - 1


def _ffn_kernel(base_ref, cnt_ref, xs_ref, wgu_ref, wd_ref, out_ref,
                xbuf, obuf, wgu16_ref, wd16_ref, sem_in, sem_out, pend_ref, *, n_blocks):
    e = pl.program_id(0)
    n_rows_e = cnt_ref[e]
    n_blk = (n_rows_e + FFN_ROWS - 1) // FFN_ROWS
    first_blk = base_ref[e] // FFN_ROWS
    last_cnt = cnt_ref[N_EXPERTS - 1]
    used = base_ref[N_EXPERTS - 1] // FFN_ROWS + (last_cnt + FFN_ROWS - 1) // FFN_ROWS
    half = D_MODEL // 2

    def in_copy(blk, slot):
        row0 = pl.multiple_of(blk * FFN_ROWS, FFN_ROWS)
        return pltpu.make_async_copy(xs_ref.at[pl.ds(row0, FFN_ROWS), :], xbuf.at[slot], sem_in.at[slot])

    def out_copy(blk, slot):
        row0 = pl.multiple_of(blk * FFN_ROWS, FFN_ROWS)
        return pltpu.make_async_copy(obuf.at[slot], out_ref.at[pl.ds(row0, FFN_ROWS), :], sem_out.at[slot])

    @pl.when(e == 0)
    def _():
        for slot in range(FFN_SLOTS):
            pend_ref[slot] = 0
        for blk in range(FFN_AHEAD):
            @pl.when(blk < used)
            def _():
                in_copy(blk, blk).start()

    @pl.when(n_blk > 0)
    def _():
        wgu16_ref[...] = wgu_ref[0].astype(BF)
        wd16_ref[...] = wd_ref[0].astype(BF)

    def fetch_ahead(blk):
        @pl.when(blk + FFN_AHEAD < used)
        def _():
            in_copy(blk + FFN_AHEAD, (blk + FFN_AHEAD) % FFN_SLOTS).start()

    def ffn_rows(words, j):
        row = lax.broadcasted_iota(jnp.int32, (words.shape[0], 1), 0)
        lo, hi = _unpack_pair(jnp.where(row < n_rows_e - j * FFN_ROWS, words, jnp.uint32(0)))
        gu = _dot(lo, wgu16_ref[0:half, :]) + _dot(hi, wgu16_ref[half:, :])
        act = (jax.nn.silu(gu[:, :EXPERT_FF]) * gu[:, EXPERT_FF:]).astype(BF)
        return _pack_pair(_dot(act, wd16_ref[...]))

    def write_back(blk, slot, packed):
        @pl.when(pend_ref[slot] == 1)
        def _():
            out_copy(blk, slot).wait()

        obuf[slot] = packed
        out_copy(blk, slot).start()
        pend_ref[slot] = 1

    def single(j):
        blk = first_blk + j
        slot = blk % FFN_SLOTS
        in_copy(blk, slot).wait()
        fetch_ahead(blk)
        write_back(blk, slot, ffn_rows(xbuf[slot], j))

    def pair(p, carry):
        j = lead + 2 * p
        blk = first_blk + j
        slot = pl.multiple_of(blk % FFN_SLOTS, 2)
        in_copy(blk, slot).wait()
        in_copy(blk + 1, slot + 1).wait()
        fetch_ahead(blk)
        packed = ffn_rows(xbuf[pl.ds(slot, 2)].reshape(2 * FFN_ROWS, half), j)
        fetch_ahead(blk + 1)
        write_back(blk, slot, packed[:FFN_ROWS])
        write_back(blk + 1, slot + 1, packed[FFN_ROWS:])
        return carry

    lead = jnp.where(n_blk > 0, first_blk % 2, 0)

    @pl.when(lead == 1)
    def _():
        single(0)

    lax.fori_loop(0, (n_blk - lead) // 2, pair, 0)

    @pl.when((n_blk - lead) % 2 == 1)
    def _():
        single(n_blk - 1)

    @pl.when(e == N_EXPERTS - 1)
    def _():
        for slot in range(FFN_SLOTS):
            @pl.when(pend_ref[slot] == 1)
            def _():
                out_copy(0, slot).wait()
        obuf[0] = jnp.zeros((FFN_ROWS, half), jnp.uint32)

        def fill(blk, carry):
            cp = out_copy(blk, 0)
            cp.start()
            cp.wait()
            return carry

        lax.fori_loop(used, n_blocks, fill, 0)


def _ffn(base, counts, xs, wgu, wd):
    n_rows = xs.shape[0]
    half = D_MODEL // 2
    grid_spec = pltpu.PrefetchScalarGridSpec(
        num_scalar_prefetch=2,
        grid=(N_EXPERTS,),
        in_specs=[pl.BlockSpec(memory_space=pl.ANY),
                  pl.BlockSpec((1, D_MODEL, 2 * EXPERT_FF), lambda e, base, cnt: (e, 0, 0)),
                  pl.BlockSpec((1, EXPERT_FF, D_MODEL), lambda e, base, cnt: (e, 0, 0))],
        out_specs=pl.BlockSpec(memory_space=pl.ANY),
        scratch_shapes=[pltpu.VMEM((FFN_SLOTS, FFN_ROWS, half), jnp.uint32),
                        pltpu.VMEM((FFN_SLOTS, FFN_ROWS, half), jnp.uint32),
                        pltpu.VMEM((D_MODEL, 2 * EXPERT_FF), BF),
                        pltpu.VMEM((EXPERT_FF, D_MODEL), BF),
                        pltpu.SemaphoreType.DMA((FFN_SLOTS,)),
                        pltpu.SemaphoreType.DMA((FFN_SLOTS,)),
                        pltpu.SMEM((FFN_SLOTS,), jnp.int32)],
    )
    return pl.pallas_call(
        functools.partial(_ffn_kernel, n_blocks=n_rows // FFN_ROWS),
        grid_spec=grid_spec,
        out_shape=jax.ShapeDtypeStruct((n_rows, half), jnp.uint32),
        compiler_params=pltpu.CompilerParams(dimension_semantics=("arbitrary",)),
        name="ffn",
    )(base, counts, xs, wgu, wd)


def _final_kernel(rows_ref, gate_ref, h_ref, wsgu_ref, wsd_ref, ln2w_ref, ln2b_ref, o_ref):
    h = h_ref[...]
    hb = h.astype(BF)
    gu = _dot(hb, wsgu_ref[...])
    act = (jax.nn.silu(gu[:, :SHARED_FF]) * gu[:, SHARED_FF:]).astype(BF)
    shared = _dot(act, wsd_ref[...])

    half = D_MODEL // 2
    lo_acc = jnp.zeros((FINAL_TM, half), F32)
    hi_acc = jnp.zeros((FINAL_TM, half), F32)
    gates = gate_ref[...]
    for kk in range(TOP_K):
        words = rows_ref[kk]
        gk = gates[:, kk:kk + 1]
        lo_acc = lo_acc + gk * pltpu.bitcast(words << 16, F32)
        hi_acc = hi_acc + gk * pltpu.bitcast(words & jnp.uint32(0xFFFF0000), F32)
    routed = jnp.concatenate([lo_acc, hi_acc], axis=1)
    o_ref[...] = _layer_norm(ALPHA * h + routed + shared, ln2w_ref[...], ln2b_ref[...])


def _final(rows, gates_t, h, wsgu, wsd, ln2w, ln2b):
    n_tok = h.shape[0]
    n_tiles = n_tok // FINAL_TM
    half = D_MODEL // 2
    row = lambda i: (i, 0)
    consts = (wsgu, wsd, ln2w, ln2b)
    return pl.pallas_call(
        _final_kernel,
        grid=(n_tiles,),
        in_specs=[pl.BlockSpec((TOP_K, FINAL_TM, half), lambda i: (0, i, 0)),
                  pl.BlockSpec((FINAL_TM, TOP_K), row),
                  pl.BlockSpec((FINAL_TM, D_MODEL), row)] + [_const_spec(c.shape) for c in consts],
        out_specs=pl.BlockSpec((FINAL_TM, D_MODEL), row),
        out_shape=jax.ShapeDtypeStruct((n_tok, D_MODEL), F32),
        compiler_params=pltpu.CompilerParams(dimension_semantics=("arbitrary",)),
        name="final",
    )(rows, gates_t, h, *consts)


def _np_consts():
    idx = np.arange(QUAD)
    g256 = (idx[:, None] // HEAD_DIM == idx[None, :] // HEAD_DIM).astype(np.float32)
    t = np.arange(CUM_TILE)
    same = t[:, None] // SCAN_CHUNK == t[None, :] // SCAN_CHUNK
    ltri = (same & (t[None, :] <= t[:, None])).astype(np.float32)
    return jnp.asarray(g256, BF), jnp.asarray(ltri, BF)


def _block_plan(counts, n_assign):
    padded = (counts + FFN_ROWS - 1) // FFN_ROWS * FFN_ROWS
    base = jnp.cumsum(padded) - padded
    n_blocks = (n_assign + N_EXPERTS * (FFN_ROWS - 1) + FFN_ROWS - 1) // FFN_ROWS
    return base.astype(jnp.int32), n_blocks * FFN_ROWS


def _layer(x2, seq, w_in, shift_mu, w0, decay_lora, a0, iclr_lora, gate_lora, k_k, k_a, r_k, gn_w, gn_b,
           lnw, lnb, spatial_w, spatial_b, w_o_rwkv, w_o_gmlp, w_out, ln1_w, ln1_b,
           w_router, router_bias, w_exp_gate_up, w_exp_down, w_sh_gate_up, w_sh_down, ln2_w, ln2_b):
    n_tok = x2.shape[0]
    c_rwkv = C_RKV + C_LORA
    pad = C_LORA_PAD - C_LORA
    win = jnp.concatenate([w_in[:, :c_rwkv], jnp.zeros((D_MODEL, pad), F32), w_in[:, c_rwkv:]], axis=1).astype(BF)
    mu = jnp.pad(shift_mu, (0, pad)).reshape(1, C_RWKV_PAD)
    row1 = lambda t: t.reshape(1, -1)
    w1 = jnp.zeros((LANES, 2 * D_MODEL), F32)
    w1 = w1.at[:DECAY_LORA, :D_MODEL].set(decay_lora).at[DECAY_LORA:, D_MODEL:].set(iclr_lora).astype(BF)
    w2 = jnp.zeros((C_LORA_PAD - LANES, D_MODEL), F32).at[:GATE_LORA].set(gate_lora).astype(BF)
    g256, ltri = _np_consts()
    sb_map = jnp.repeat(spatial_b.T, D_MODEL // GMLP_GROUPS, axis=1)

    outs = _front(x2, win, mu, row1(w0), row1(a0), w1, w2, row1(k_k), row1(k_a), row1(r_k),
                  g256, ltri, row1(lnw), row1(lnb), spatial_w, sb_map, w_o_gmlp.astype(BF), seq)
    quads, wc, (g, bg, ga, zb) = outs[:7], outs[7], outs[8:]
    h1, h1p = _scan(quads, wc, g, bg, ga, zb, x2, row1(gn_w), row1(gn_b), g256,
                    w_o_rwkv.astype(BF), w_out.astype(BF), row1(ln1_w), row1(ln1_b), seq)

    wrt = w_router.T
    wrh = wrt.astype(BF)
    wrl = (wrt - wrh.astype(F32)).astype(BF)
    tt = np.arange(ROUTE_TM)
    ustrict = jnp.asarray((tt[:, None] < tt[None, :]).astype(np.float32), BF)
    top_e, gates, rank, counts = _route(h1, wrh, wrl, router_bias.reshape(N_EXPERTS, 1), ustrict)
    counts = counts[:, 0]
    base, n_rows = _block_plan(counts, n_tok * TOP_K)
    pos = _rows(top_e, rank, base.reshape(N_EXPERTS, 1))
    xs = _sc_scatter_rows(h1p, pos, n_rows)
    rows = _ffn(base, counts, xs, w_exp_gate_up, w_exp_down)
    picked = _sc_gather_rows(rows, pos.reshape(-1)).reshape(TOP_K, n_tok, D_MODEL // 2)
    return _final(picked, gates.T, h1, w_sh_gate_up.astype(BF), w_sh_down.astype(BF),
                  row1(ln2_w), row1(ln2_b))


def kernel(x, w_in, shift_mu, rwkv_w0, rwkv_decay_lora, rwkv_a0, rwkv_iclr_lora, rwkv_gate_lora, rwkv_k_k, rwkv_k_a, rwkv_r_k, rwkv_gn_w, rwkv_gn_b, gmlp_ln_w, gmlp_ln_b, gmlp_spatial_w, gmlp_spatial_b, w_o_rwkv, w_o_gmlp, w_out, ln1_w, ln1_b, w_router, router_bias, w_exp_gate_up, w_exp_down, w_sh_gate_up, w_sh_down, ln2_w, ln2_b):
    bsz, seq, d = x.shape
    h = x.reshape(bsz * seq, d)
    for l in range(DEPTH):
        h = _layer(h, seq, w_in[l], shift_mu[l], rwkv_w0[l], rwkv_decay_lora[l], rwkv_a0[l],
                   rwkv_iclr_lora[l], rwkv_gate_lora[l], rwkv_k_k[l], rwkv_k_a[l], rwkv_r_k[l],
                   rwkv_gn_w[l], rwkv_gn_b[l], gmlp_ln_w[l], gmlp_ln_b[l], gmlp_spatial_w[l],
                   gmlp_spatial_b[l], w_o_rwkv[l], w_o_gmlp[l], w_out[l], ln1_w[l], ln1_b[l],
                   w_router[l], router_bias[l], w_exp_gate_up[l], w_exp_down[l], w_sh_gate_up[l],
                   w_sh_down[l], ln2_w[l], ln2_b[l])
    return h.reshape(bsz, seq, d)
```

```python
import functools

import numpy as np
import jax
import jax.numpy as jnp
from jax import lax
from jax.experimental import pallas as pl
from jax.experimental.pallas import tpu as pltpu
from jax.experimental.pallas import tpu_sc as plsc

F32 = jnp.float32
BF = jnp.bfloat16

D_MODEL = 1024
HEAD_DIM = 64
N_HEADS = 16
DECAY_LORA = 64
ICLR_LORA = 64
GATE_LORA = 160
GN_EPS = 64e-5
GMLP_GROUPS = 8
GMLP_CHUNK = 128
N_EXPERTS = 256
TOP_K = 8
N_EXPERT_GROUPS = 8
TOPK_GROUPS = 4
EXPERT_FF = 256
SHARED_FF = 256
ROUTED_SCALE = 2.5
LN_EPS = 1e-5
DEPTH = 1
ALPHA = (2 * DEPTH) ** 0.25

LANES = 128
V7X_VMEM_LIMIT_BYTES = 56 * 1024 * 1024

C_RKV = 3 * D_MODEL
C_LORA = DECAY_LORA + ICLR_LORA + GATE_LORA
C_LORA_PAD = 384
C_RWKV_PAD = C_RKV + C_LORA_PAD
C_GMLP = 2 * D_MODEL
C_GATES = 2 * D_MODEL
C_IN_PAD = C_RWKV_PAD + C_GMLP + C_GATES

SCAN_CHUNK = 64
QUAD = 4 * HEAD_DIM
N_QUADS = D_MODEL // QUAD
TM = 512
CUM_TILE = 256
SCAN_TM = 512
ROUTE_TM = 1024
FFN_ROWS = 256
FINAL_TM = 512


def _dot(a, b):
    return jnp.dot(a, b, preferred_element_type=F32)


def _dot_nt(a, b):
    return lax.dot_general(a, b, (((1,), (1,)), ((), ())), preferred_element_type=F32)


def _dot_tn(a, b):
    return lax.dot_general(a, b, (((0,), (0,)), ((), ())), preferred_element_type=F32)


def _split2(z):
    hi = z.astype(BF)
    lo = (z - hi.astype(F32)).astype(BF)
    return hi, lo


def _headsum(z, g256, passes=2):
    outs = []
    for j in range(N_QUADS):
        zj = z[:, QUAD * j:QUAD * (j + 1)]
        if passes == 1:
            outs.append(_dot(zj.astype(BF), g256))
        else:
            hi, lo = _split2(zj)
            outs.append(_dot(hi, g256) + _dot(lo, g256))
    return jnp.concatenate(outs, axis=1)


def _layer_norm(v, w, b):
    mu = jnp.mean(v, axis=-1, keepdims=True)
    d = v - mu
    var = jnp.mean(d * d, axis=-1, keepdims=True)
    return d * lax.rsqrt(var + LN_EPS) * w + b


def _const_spec(shape):
    nd = len(shape)
    return pl.BlockSpec(shape, lambda *_: (0,) * nd, pipeline_mode=pl.Buffered(1))


def _front_kernel(x_ref, win_ref, mu_ref, w0_ref, a0_ref, w1_ref, w2_ref, kk_ref, ka_ref, rk_ref,
                  g256_ref, ltri_ref, lnw_ref, lnb_ref, ws_ref, sb_ref, wog_ref,
                  rt_ref, at_ref, bt_ref, kt_ref, vv_ref, bh_ref, kh_ref, wc_ref,
                  g_ref, bg_ref, ga_ref, zb_ref, prev_ref, xb_ref, *, tiles_per_seq):
    i = pl.program_id(0)
    first = (i % tiles_per_seq) == 0
    xb_ref[...] = x_ref[...].astype(BF)
    is_row0 = lax.broadcasted_iota(jnp.int32, (TM, 1), 0) == 0
    W = QUAD

    def proj(lo, width=W):
        return _dot(xb_ref[...], win_ref[:, lo:lo + width])

    def shift_mix(lo, width=W):
        p = proj(lo, width)
        prev = jnp.where(first, 0.0, prev_ref[:, lo:lo + width])
        sh = jnp.where(is_row0, prev, pltpu.roll(p, 1, 0))
        prev_ref[:, lo:lo + width] = p[TM - 1:TM, :]
        return p + mu_ref[:, lo:lo + width] * (sh - p)

    lora = shift_mix(C_RKV, C_LORA_PAD)
    lane = lax.broadcasted_iota(jnp.int32, (1, LANES), 1)
    act1 = jnp.where(lane < DECAY_LORA, jnp.tanh(lora[:, :LANES]), lora[:, :LANES]).astype(BF)
    act2 = jax.nn.sigmoid(lora[:, LANES:]).astype(BF)
    g256 = g256_ref[...]
    for q in range(N_QUADS):
        c0 = W * q
        cols = slice(c0, c0 + W)
        r = shift_mix(c0)
        k = shift_mix(D_MODEL + c0)
        v = shift_mix(2 * D_MODEL + c0)
        w = -jax.nn.softplus(-(w0_ref[:, cols] + _dot(act1, w1_ref[:, cols]))) - 0.5
        logd = -jnp.exp(w)
        a = jax.nn.sigmoid(a0_ref[:, cols] + _dot(act1, w1_ref[:, D_MODEL + c0:D_MODEL + c0 + W]))
        g = _dot(act2, w2_ref[:, cols])

        kkr = k * kk_ref[:, cols]
        kk = kkr * lax.rsqrt(jnp.maximum(_dot((kkr * kkr).astype(BF), g256), 1e-24))
        kmod = k * (1.0 + (a - 1.0) * ka_ref[:, cols])
        bonus = _dot((r * kmod * rk_ref[:, cols]).astype(BF), g256) * v
        kka = kk * a

        parts = _split2(logd)
        cum = jnp.concatenate(
            [sum(_dot(ltri_ref[...], part[CUM_TILE * hh:CUM_TILE * (hh + 1)]) for part in parts)
             for hh in range(TM // CUM_TILE)], axis=0)
        e_inv = jnp.exp(-cum)
        bt = kka * e_inv
        kt = kmod * e_inv
        ends = []
        for c in range(TM // SCAN_CHUNK):
            last = SCAN_CHUNK * (c + 1) - 1
            ewc_row = jnp.exp(cum[last:last + 1, :])
            wc_ref[c, q:q + 1, :] = ewc_row
            ends.append(jnp.broadcast_to(ewc_row, (SCAN_CHUNK, W)))
        ewc = jnp.concatenate(ends, axis=0)
        rt_ref[q] = (r * jnp.exp(cum)).astype(BF)
        at_ref[q] = (-kk * jnp.exp(cum - logd)).astype(BF)
        bt_ref[q] = bt.astype(BF)
        kt_ref[q] = kt.astype(BF)
        vv_ref[q] = v.astype(BF)
        bh_ref[q] = (bt * ewc).astype(BF)
        kh_ref[q] = (kt * ewc).astype(BF)
        g_ref[:, cols] = g.astype(BF)
        bg_ref[:, cols] = (bonus * g).astype(BF)

    gelu = lambda t: 0.5 * t * (1.0 + lax.erf(t * np.float32(np.sqrt(0.5))))
    u0 = C_RWKV_PAD
    zv = [gelu(proj(u0 + D_MODEL + W * j)) for j in range(N_QUADS)]
    mean = sum(jnp.sum(z, axis=-1, keepdims=True) for z in zv) * (1.0 / D_MODEL)
    dv = [z - mean for z in zv]
    var = sum(jnp.sum(d * d, axis=-1, keepdims=True) for d in dv) * (1.0 / D_MODEL)
    rstd = lax.rsqrt(var + LN_EPS)
    trow = lax.broadcasted_iota(jnp.int32, (GMLP_CHUNK, GMLP_CHUNK), 0)
    tcol = lax.broadcasted_iota(jnp.int32, (GMLP_CHUNK, GMLP_CHUNK), 1)
    causal = tcol <= trow
    n_time = TM // GMLP_CHUNK
    yb = jnp.zeros((TM, D_MODEL), F32)
    for j in range(N_QUADS):
        cols = slice(W * j, W * (j + 1))
        vn = (dv[j] * rstd * lnw_ref[:, cols] + lnb_ref[:, cols]).astype(BF)
        svs = []
        for gl in range(W // LANES):
            wsg = jnp.where(causal, ws_ref[(W // LANES) * j + gl], 0.0).astype(BF)
            svs.append(jnp.concatenate(
                [_dot(wsg, vn[GMLP_CHUNK * cc:GMLP_CHUNK * (cc + 1), LANES * gl:LANES * (gl + 1)])
                 for cc in range(n_time)], axis=0))
        sb = sb_ref[:, cols]
        sv = jnp.concatenate(svs, axis=1) + jnp.concatenate([sb] * n_time, axis=0)
        u = gelu(proj(u0 + W * j))
        yb = yb + _dot((u * sv).astype(BF), wog_ref[W * j:W * (j + 1), :])

    g0 = C_RWKV_PAD + C_GMLP
    for j in range(N_QUADS):
        cols = slice(W * j, W * (j + 1))
        ga_ref[:, cols] = jax.nn.sigmoid(proj(g0 + W * j)).astype(BF)
        zb_ref[:, cols] = (jax.nn.sigmoid(proj(g0 + D_MODEL + W * j)) * yb[:, cols]).astype(BF)


def _front(x2, win, mu, w0, a0, w1, w2, k_k, k_a, r_k, g256, ltri, lnw, lnb, ws, sb, wog, seq):
    n_tok = x2.shape[0]
    n_tiles = n_tok // TM
    row = lambda i: (i, 0)
    quad_spec = pl.BlockSpec((N_QUADS, TM, QUAD), lambda i: (0, i, 0))
    quad_shape = jax.ShapeDtypeStruct((N_QUADS, n_tok, QUAD), BF)
    full_spec = pl.BlockSpec((TM, D_MODEL), row)
    full_shape = jax.ShapeDtypeStruct((n_tok, D_MODEL), BF)
    wc_spec = pl.BlockSpec((TM // SCAN_CHUNK, N_QUADS, QUAD), lambda i: (i, 0, 0))
    wc_shape = jax.ShapeDtypeStruct((n_tok // SCAN_CHUNK, N_QUADS, QUAD), F32)
    consts = (win, mu, w0, a0, w1, w2, k_k, k_a, r_k, g256, ltri, lnw, lnb, ws, sb, wog)
    return pl.pallas_call(
        functools.partial(_front_kernel, tiles_per_seq=seq // TM),
        grid=(n_tiles,),
        in_specs=[pl.BlockSpec((TM, D_MODEL), row)] + [_const_spec(c.shape) for c in consts],
        out_specs=[quad_spec] * 7 + [wc_spec] + [full_spec] * 4,
        out_shape=[quad_shape] * 7 + [wc_shape] + [full_shape] * 4,
        scratch_shapes=[pltpu.VMEM((1, C_RWKV_PAD), F32), pltpu.VMEM((TM, D_MODEL), BF)],
        compiler_params=pltpu.CompilerParams(
            dimension_semantics=("arbitrary",), vmem_limit_bytes=V7X_VMEM_LIMIT_BYTES),
        name="front",
    )(x2, *consts)


def _scan_kernel(rt_ref, at_ref, bt_ref, kt_ref, vv_ref, bh_ref, kh_ref, wc_ref,
                 g_ref, bg_ref, ga_ref, zb_ref, x_ref, gnw_ref, gnb_ref, g256_ref,
                 wor_ref, wout_ref, ln1w_ref, ln1b_ref,
                 h_ref, hp_ref, s_ref, *, tiles_per_seq):
    i = pl.program_id(0)

    @pl.when((i % tiles_per_seq) == 0)
    def _():
        s_ref[...] = jnp.zeros_like(s_ref)

    lane = lax.broadcasted_iota(jnp.int32, (1, QUAD), 1)
    lane_head = lane // HEAD_DIM
    lane_pos = lane % HEAD_DIM
    trow = lax.broadcasted_iota(jnp.int32, (SCAN_CHUNK, 1), 0)
    strict = lane_pos < trow
    incl = lane_pos <= trow
    row_head = lax.broadcasted_iota(jnp.int32, (QUAD, 1), 0) // HEAD_DIM
    diag_blocks = row_head == lane_head

    def bd4(m):
        zero = jnp.zeros_like(m)
        return jnp.concatenate([jnp.where(lane_head == j, m, zero) for j in range(4)], axis=0)

    n_chunks = SCAN_TM // SCAN_CHUNK
    items = [(c, q) for c in range(n_chunks) for q in range(N_QUADS)]
    rows_of = lambda c: slice(SCAN_CHUNK * c, SCAN_CHUNK * (c + 1))
    C = SCAN_CHUNK

    at, rt, bdv, a_ab, a_akrk, a_rb = {}, {}, {}, {}, {}, {}
    for it in items:
        c, q = it
        at[it] = at_ref[q, rows_of(c), :]
        rt[it] = rt_ref[q, rows_of(c), :]
        ra = jnp.concatenate([at[it], rt[it]], axis=0)
        gb = _dot_nt(ra, bd4(bt_ref[q, rows_of(c), :]))
        gk = _dot_nt(ra, bd4(kt_ref[q, rows_of(c), :]))
        a_ab[it] = jnp.where(strict, gb[:C], 0.0)
        a_rb[it] = jnp.where(incl, gb[C:], 0.0).astype(BF)
        a_akrk[it] = jnp.concatenate(
            [jnp.where(strict, gk[:C], 0.0), jnp.where(incl, gk[C:], 0.0)], axis=0).astype(BF)
        bdv[it] = bd4(vv_ref[q, rows_of(c), :])

    tm, qpow = {}, {}
    for it in items:
        qb = a_ab[it].astype(BF)
        tm[it] = a_ab[it]
        qpow[it] = _dot(qb, bd4(qb))
    for level in range(1, 6):
        for it in items:
            qb = qpow[it].astype(BF)
            bdq = bd4(qb)
            if level < 5:
                both = _dot(jnp.concatenate([tm[it].astype(BF), qb], axis=0), bdq)
                tm[it] = tm[it] + qpow[it] + both[:C]
                qpow[it] = both[C:]
            else:
                tm[it] = tm[it] + qpow[it] + _dot(tm[it].astype(BF), bdq)

    awr, u0, rkv = {}, {}, {}
    for it in items:
        tb = tm[it].astype(BF)
        aw = at[it].astype(F32) + _dot(tb, bd4(at[it]))
        awr[it] = jnp.concatenate([aw.astype(BF), rt[it]], axis=0)
        kv = _dot(a_akrk[it], bdv[it])
        u0[it] = kv[:C] + _dot(tb, bd4(kv[:C].astype(BF)))
        rkv[it] = kv[C:]

    state = [s_ref[q] for q in range(N_QUADS)]
    ys = {}
    for c in range(n_chunks):
        for q in range(N_QUADS):
            it = (c, q)
            us = _dot_nt(awr[it], state[q].astype(BF))
            ub = (u0[it] + us[:C]).astype(BF)
            ys[it] = us[C:] + rkv[it] + _dot(a_rb[it], bd4(ub))
            uv = jnp.concatenate([ub, vv_ref[q, rows_of(c), :]], axis=0)
            bk = jnp.concatenate([bh_ref[q, rows_of(c), :], kh_ref[q, rows_of(c), :]], axis=0)
            wc = wc_ref[c, q:q + 1, :]
            state[q] = state[q] * wc + jnp.where(diag_blocks, _dot_tn(uv, bk), 0.0)
    for q in range(N_QUADS):
        s_ref[q] = state[q]

    g256 = g256_ref[...]
    y = jnp.concatenate(
        [jnp.concatenate([ys[(c, q)] for c in range(n_chunks)], axis=0) for q in range(N_QUADS)], axis=1)
    inv_n = 1.0 / HEAD_DIM
    d = y - _headsum(y, g256) * inv_n
    var = _headsum(d * d, g256) * inv_n
    yn = d * lax.rsqrt(var + GN_EPS) * gnw_ref[...] + gnb_ref[...]
    ya = _dot((yn * g_ref[...].astype(F32) + bg_ref[...].astype(F32)).astype(BF), wor_ref[...])
    mix_in = ga_ref[...].astype(F32) * ya + zb_ref[...].astype(F32)
    mix = _dot(mix_in.astype(BF), wout_ref[...])
    h = _layer_norm(ALPHA * x_ref[...] + mix, ln1w_ref[...], ln1b_ref[...])
    h_ref[...] = h
    hb = pltpu.bitcast(h.astype(BF).astype(F32), jnp.uint32)
    half = D_MODEL // 2
    hp_ref[...] = (hb[:, :half] >> 16) | (hb[:, half:] & jnp.uint32(0xFFFF0000))


def _scan(quads, wc, g, bg, ga, zb, x2, gnw, gnb, g256, wor, wout, ln1w, ln1b, seq):
    n_tok = x2.shape[0]
    n_tiles = n_tok // SCAN_TM
    row = lambda i: (i, 0)
    quad_spec = pl.BlockSpec((N_QUADS, SCAN_TM, QUAD), lambda i: (0, i, 0))
    full_spec = pl.BlockSpec((SCAN_TM, D_MODEL), row)
    wc_spec = pl.BlockSpec((SCAN_TM // SCAN_CHUNK, N_QUADS, QUAD), lambda i: (i, 0, 0))
    consts = (gnw, gnb, g256, wor, wout, ln1w, ln1b)
    return pl.pallas_call(
        functools.partial(_scan_kernel, tiles_per_seq=seq // SCAN_TM),
        grid=(n_tiles,),
        in_specs=[quad_spec] * 7 + [wc_spec]
        + [full_spec] * 5 + [_const_spec(c.shape) for c in consts],
        out_specs=[full_spec, pl.BlockSpec((SCAN_TM, D_MODEL // 2), row)],
        out_shape=[jax.ShapeDtypeStruct((n_tok, D_MODEL), F32),
                   jax.ShapeDtypeStruct((n_tok, D_MODEL // 2), jnp.uint32)],
        scratch_shapes=[pltpu.VMEM((N_QUADS, QUAD, QUAD), F32)],
        compiler_params=pltpu.CompilerParams(
            dimension_semantics=("arbitrary",), vmem_limit_bytes=V7X_VMEM_LIMIT_BYTES),
        name="scan",
    )(*quads, wc, g, bg, ga, zb, x2, *consts)


def _first_max(x, idx, n):
    m = jnp.max(x, axis=0, keepdims=True)
    first = jnp.min(jnp.where(x == m, idx, n), axis=0, keepdims=True)
    return m, first


def _route_kernel(h_ref, wrh_ref, wrl_ref, bias_ref, ustrict_ref, e_ref, gate_ref, rank_ref, cnt_ref,
                  carry_ref):
    @pl.when(pl.program_id(0) == 0)
    def _():
        carry_ref[...] = jnp.zeros_like(carry_ref)

    hh, hl = _split2(h_ref[...])
    wrh = wrh_ref[...]
    logits = _dot_nt(wrh, hh) + _dot_nt(wrh, hl) + _dot_nt(wrl_ref[...], hh)
    scores = jax.nn.sigmoid(logits)
    choice = scores + bias_ref[...]
    n_tok = choice.shape[1]
    per = N_EXPERTS // N_EXPERT_GROUPS
    eidx = lax.broadcasted_iota(jnp.int32, (N_EXPERTS, 1), 0)
    gidx = lax.broadcasted_iota(jnp.int32, (N_EXPERT_GROUPS, 1), 0)
    lidx = lax.broadcasted_iota(jnp.int32, (per, 1), 0)
    neg = -jnp.inf

    gsc = []
    for gi in range(N_EXPERT_GROUPS):
        blk = choice[per * gi:per * (gi + 1), :]
        m1, i1 = _first_max(blk, lidx, per)
        m2 = jnp.max(jnp.where(lidx == i1, neg, blk), axis=0, keepdims=True)
        gsc.append(m1 + m2)
    gscore = jnp.concatenate(gsc, axis=0)
    gsel = jnp.zeros(gscore.shape, jnp.bool_)
    for _ in range(TOPK_GROUPS):
        _, gi1 = _first_max(gscore, gidx, N_EXPERT_GROUPS)
        hit = gidx == gi1
        gsel = gsel | hit
        gscore = jnp.where(hit, neg, gscore)
    emask = jnp.concatenate(
        [jnp.broadcast_to(gsel[gi:gi + 1, :], (per, n_tok)) for gi in range(N_EXPERT_GROUPS)], axis=0)
    cand = jnp.where(emask, choice, neg)
    es, gs, hits = [], [], []
    for _ in range(TOP_K):
        _, e1 = _first_max(cand, eidx, N_EXPERTS)
        hit = eidx == e1
        es.append(e1)
        hits.append(hit)
        gs.append(jnp.sum(jnp.where(hit, scores, 0.0), axis=0, keepdims=True))
        cand = jnp.where(hit, neg, cand)
    gates = jnp.concatenate(gs, axis=0)
    gates = gates / jnp.sum(gates, axis=0, keepdims=True) * ROUTED_SCALE
    e_ref[...] = jnp.concatenate(es, axis=0)
    gate_ref[...] = gates

    chosen = functools.reduce(jnp.logical_or, hits)
    sel = jnp.where(chosen, 1.0, 0.0)
    before = _dot(sel.astype(BF), ustrict_ref[...]) + carry_ref[:, 0:1]
    rank_ref[...] = jnp.concatenate(
        [jnp.sum(jnp.where(hit, before, 0.0), axis=0, keepdims=True) for hit in hits],
        axis=0).astype(jnp.int32)
    total = before[:, n_tok - 1:n_tok] + sel[:, n_tok - 1:n_tok]
    carry_ref[...] = jnp.broadcast_to(total, carry_ref.shape)
    cnt_ref[...] = jnp.broadcast_to(total, cnt_ref.shape).astype(jnp.int32)


def _route(h, wrh, wrl, bias_col, ustrict):
    n_tok = h.shape[0]
    col = lambda i: (0, i)
    tok_spec = pl.BlockSpec((TOP_K, ROUTE_TM), col)
    return pl.pallas_call(
        _route_kernel,
        grid=(n_tok // ROUTE_TM,),
        in_specs=[pl.BlockSpec((ROUTE_TM, D_MODEL), lambda i: (i, 0)),
                  _const_spec(wrh.shape), _const_spec(wrl.shape), _const_spec(bias_col.shape),
                  _const_spec(ustrict.shape)],
        out_specs=[tok_spec, tok_spec, tok_spec, pl.BlockSpec((N_EXPERTS, LANES), lambda i: (0, 0))],
        out_shape=[jax.ShapeDtypeStruct((TOP_K, n_tok), jnp.int32),
                   jax.ShapeDtypeStruct((TOP_K, n_tok), F32),
                   jax.ShapeDtypeStruct((TOP_K, n_tok), jnp.int32),
                   jax.ShapeDtypeStruct((N_EXPERTS, LANES), jnp.int32)],
        scratch_shapes=[pltpu.VMEM((N_EXPERTS, LANES), F32)],
        compiler_params=pltpu.CompilerParams(dimension_semantics=("arbitrary",)),
        name="route",
    )(h, wrh, wrl, bias_col, ustrict)


def _rows_kernel(e_ref, rank_ref, base_ref, pos_ref):
    eidx = lax.broadcasted_iota(jnp.int32, (N_EXPERTS, 1), 0)
    base = base_ref[...]
    pos_ref[...] = rank_ref[...] + jnp.concatenate(
        [jnp.sum(jnp.where(eidx == e_ref[kk:kk + 1, :], base, 0), axis=0, keepdims=True)
         for kk in range(TOP_K)], axis=0)


def _rows(top_e, rank, base_col):
    n_tok = top_e.shape[1]
    tok_spec = pl.BlockSpec((TOP_K, ROUTE_TM), lambda i: (0, i))
    return pl.pallas_call(
        _rows_kernel,
        grid=(n_tok // ROUTE_TM,),
        in_specs=[tok_spec, tok_spec, _const_spec(base_col.shape)],
        out_specs=tok_spec,
        out_shape=jax.ShapeDtypeStruct((TOP_K, n_tok), jnp.int32),
        compiler_params=pltpu.CompilerParams(dimension_semantics=("arbitrary",)),
        name="rows",
    )(top_e, rank, base_col)


SC_WINDOW = 128


def _sc_mesh():
    return plsc.VectorSubcoreMesh(core_axis_name="c", subcore_axis_name="s")


def _sc_workers():
    info = plsc.get_sparse_core_info()
    return info.num_cores, info.num_cores * info.num_subcores


def _sc_scatter_rows(src, pos, n_rows):
    n_tok, width = src.shape
    n_k = pos.shape[0]
    n_cores, n_workers = _sc_workers()
    n_win = n_tok // (n_workers * SC_WINDOW)

    @functools.partial(
        pl.kernel, out_type=jax.ShapeDtypeStruct((n_rows, width), src.dtype), mesh=_sc_mesh(),
        scratch_types=[pltpu.VMEM((n_k * n_win, SC_WINDOW), jnp.int32),
                       pltpu.VMEM((SC_WINDOW, width), src.dtype),
                       pltpu.SemaphoreType.DMA])
    def scatter(src_hbm, pos_hbm, out_hbm, idx_v, rows_v, sem):
        wid = lax.axis_index("s") * n_cores + lax.axis_index("c")
        pltpu.sync_copy(pos_hbm.at[wid], idx_v)

        @pl.loop(0, n_win)
        def _(j):
            pltpu.sync_copy(src_hbm.at[pl.ds((wid * n_win + j) * SC_WINDOW, SC_WINDOW)], rows_v)
            copies = [pltpu.async_copy(rows_v, out_hbm.at[idx_v.at[kk * n_win + j]], sem)
                      for kk in range(n_k)]
            for cp in copies:
                cp.wait()

    pos4 = pos.reshape(n_k, n_workers, n_win, SC_WINDOW).transpose(1, 0, 2, 3)
    return scatter(src, pos4.reshape(n_workers, n_k * n_win, SC_WINDOW))


def _sc_gather_rows(src, idx):
    n_idx = idx.shape[0]
    width = src.shape[1]
    n_cores, n_workers = _sc_workers()
    n_win = n_idx // (n_workers * SC_WINDOW)

    @functools.partial(
        pl.kernel, out_type=jax.ShapeDtypeStruct((n_idx, width), src.dtype), mesh=_sc_mesh(),
        scratch_types=[pltpu.VMEM((n_win, SC_WINDOW), jnp.int32),
                       pltpu.VMEM((SC_WINDOW, width), src.dtype),
                       pltpu.SemaphoreType.DMA])
    def gather(src_hbm, idx_hbm, out_hbm, idx_v, rows_v, sem):
        wid = lax.axis_index("s") * n_cores + lax.axis_index("c")
        pltpu.sync_copy(idx_hbm.at[wid], idx_v)

        @pl.loop(0, n_win)
        def _(j):
            pltpu.async_copy(src_hbm.at[idx_v.at[j]], rows_v, sem).wait()
            pltpu.sync_copy(rows_v, out_hbm.at[pl.ds((wid * n_win + j) * SC_WINDOW, SC_WINDOW)])

    return gather(src, idx.reshape(n_workers, n_win, SC_WINDOW))


def _unpack_pair(words):
    lo = pltpu.bitcast(words << 16, F32).astype(BF)
    hi = pltpu.bitcast(words & jnp.uint32(0xFFFF0000), F32).astype(BF)
    return lo, hi


def _pack_pair(val):
    bits = pltpu.bitcast(val.astype(BF).astype(F32), jnp.uint32)
    half = val.shape[1] // 2
    return (bits[:, :half] >> 16) | (bits[:, half:] & jnp.uint32(0xFFFF0000))


FFN_SLOTS = 6
FFN_AHEAD = FFN_SLOTS - 1


def _ffn_kernel(base_ref, cnt_ref, xs_ref, wgu_ref, wd_ref, out_ref,
                xbuf, obuf, wgu16_ref, wd16_ref, sem_in, sem_out, pend_ref, *, n_blocks):
    e = pl.program_id(0)
    n_rows_e = cnt_ref[e]
    n_blk = (n_rows_e + FFN_ROWS - 1) // FFN_ROWS
    first_blk = base_ref[e] // FFN_ROWS
    last_cnt = cnt_ref[N_EXPERTS - 1]
    used = base_ref[N_EXPERTS - 1] // FFN_ROWS + (last_cnt + FFN_ROWS - 1) // FFN_ROWS
    half = D_MODEL // 2

    def in_copy(blk, slot):
        row0 = pl.multiple_of(blk * FFN_ROWS, FFN_ROWS)
        return pltpu.make_async_copy(xs_ref.at[pl.ds(row0, FFN_ROWS), :], xbuf.at[slot], sem_in.at[slot])

    def out_copy(blk, slot):
        row0 = pl.multiple_of(blk * FFN_ROWS, FFN_ROWS)
        return pltpu.make_async_copy(obuf.at[slot], out_ref.at[pl.ds(row0, FFN_ROWS), :], sem_out.at[slot])

    @pl.when(e == 0)
    def _():
        for slot in range(FFN_SLOTS):
            pend_ref[slot] = 0
        for blk in range(FFN_AHEAD):
            @pl.when(blk < used)
            def _():
                in_copy(blk, blk).start()

    @pl.when(n_blk > 0)
    def _():
        wgu16_ref[...] = wgu_ref[0].astype(BF)
        wd16_ref[...] = wd_ref[0].astype(BF)

    def fetch_ahead(blk):
        @pl.when(blk + FFN_AHEAD < used)
        def _():
            in_copy(blk + FFN_AHEAD, (blk + FFN_AHEAD) % FFN_SLOTS).start()

    def ffn_rows(words, j):
        row = lax.broadcasted_iota(jnp.int32, (words.shape[0], 1), 0)
        lo, hi = _unpack_pair(jnp.where(row < n_rows_e - j * FFN_ROWS, words, jnp.uint32(0)))
        gu = _dot(lo, wgu16_ref[0:half, :]) + _dot(hi, wgu16_ref[half:, :])
        act = (jax.nn.silu(gu[:, :EXPERT_FF]) * gu[:, EXPERT_FF:]).astype(BF)
        return _pack_pair(_dot(act, wd16_ref[...]))

    def write_back(blk, slot, packed):
        @pl.when(pend_ref[slot] == 1)
        def _():
            out_copy(blk, slot).wait()

        obuf[slot] = packed
        out_copy(blk, slot).start()
        pend_ref[slot] = 1

    def single(j):
        blk = first_blk + j
        slot = blk % FFN_SLOTS
        in_copy(blk, slot).wait()
        fetch_ahead(blk)
        write_back(blk, slot, ffn_rows(xbuf[slot], j))

    def pair(p, carry):
        j = lead + 2 * p
        blk = first_blk + j
        slot = pl.multiple_of(blk % FFN_SLOTS, 2)
        in_copy(blk, slot).wait()
        in_copy(blk + 1, slot + 1).wait()
        fetch_ahead(blk)
        packed = ffn_rows(xbuf[pl.ds(slot, 2)].reshape(2 * FFN_ROWS, half), j)
        fetch_ahead(blk + 1)
        write_back(blk, slot, packed[:FFN_ROWS])
        write_back(blk + 1, slot + 1, packed[FFN_ROWS:])
        return carry

    lead = jnp.where(n_blk > 0, first_blk % 2, 0)

    @pl.when(lead == 1)
    def _():
        single(0)

    lax.fori_loop(0, (n_blk - lead) // 2, pair, 0)

    @pl.when((n_blk - lead) % 2 == 1)
    def _():
        single(n_blk - 1)

    @pl.when(e == N_EXPERTS - 1)
    def _():
        for slot in range(FFN_SLOTS):
            @pl.when(pend_ref[slot] == 1)
            def _():
                out_copy(0, slot).wait()
        obuf[0] = jnp.zeros((FFN_ROWS, half), jnp.uint32)

        def fill(blk, carry):
            cp = out_copy(blk, 0)
            cp.start()
            cp.wait()
            return carry

        lax.fori_loop(used, n_blocks, fill, 0)


def _ffn(base, counts, xs, wgu, wd):
    n_rows = xs.shape[0]
    half = D_MODEL // 2
    grid_spec = pltpu.PrefetchScalarGridSpec(
        num_scalar_prefetch=2,
        grid=(N_EXPERTS,),
        in_specs=[pl.BlockSpec(memory_space=pl.ANY),
                  pl.BlockSpec((1, D_MODEL, 2 * EXPERT_FF), lambda e, base, cnt: (e, 0, 0)),
                  pl.BlockSpec((1, EXPERT_FF, D_MODEL), lambda e, base, cnt: (e, 0, 0))],
        out_specs=pl.BlockSpec(memory_space=pl.ANY),
        scratch_shapes=[pltpu.VMEM((FFN_SLOTS, FFN_ROWS, half), jnp.uint32),
                        pltpu.VMEM((FFN_SLOTS, FFN_ROWS, half), jnp.uint32),
                        pltpu.VMEM((D_MODEL, 2 * EXPERT_FF), BF),
                        pltpu.VMEM((EXPERT_FF, D_MODEL), BF),
                        pltpu.SemaphoreType.DMA((FFN_SLOTS,)),
                        pltpu.SemaphoreType.DMA((FFN_SLOTS,)),
                        pltpu.SMEM((FFN_SLOTS,), jnp.int32)],
    )
    return pl.pallas_call(
        functools.partial(_ffn_kernel, n_blocks=n_rows // FFN_ROWS),
        grid_spec=grid_spec,
        out_shape=jax.ShapeDtypeStruct((n_rows, half), jnp.uint32),
        compiler_params=pltpu.CompilerParams(dimension_semantics=("arbitrary",)),
        name="ffn",
    )(base, counts, xs, wgu, wd)


def _final_kernel(rows_ref, gate_ref, h_ref, wsgu_ref, wsd_ref, ln2w_ref, ln2b_ref, o_ref):
    h = h_ref[...]
    hb = h.astype(BF)
    gu = _dot(hb, wsgu_ref[...])
    act = (jax.nn.silu(gu[:, :SHARED_FF]) * gu[:, SHARED_FF:]).astype(BF)
    shared = _dot(act, wsd_ref[...])

    half = D_MODEL // 2
    lo_acc = jnp.zeros((FINAL_TM, half), F32)
    hi_acc = jnp.zeros((FINAL_TM, half), F32)
    gates = gate_ref[...]
    for kk in range(TOP_K):
        words = rows_ref[kk]
        gk = gates[:, kk:kk + 1]
        lo_acc = lo_acc + gk * pltpu.bitcast(words << 16, F32)
        hi_acc = hi_acc + gk * pltpu.bitcast(words & jnp.uint32(0xFFFF0000), F32)
    routed = jnp.concatenate([lo_acc, hi_acc], axis=1)
    o_ref[...] = _layer_norm(ALPHA * h + routed + shared, ln2w_ref[...], ln2b_ref[...])


def _final(rows, gates_t, h, wsgu, wsd, ln2w, ln2b):
    n_tok = h.shape[0]
    n_tiles = n_tok // FINAL_TM
    half = D_MODEL // 2
    row = lambda i: (i, 0)
    consts = (wsgu, wsd, ln2w, ln2b)
    return pl.pallas_call(
        _final_kernel,
        grid=(n_tiles,),
        in_specs=[pl.BlockSpec((TOP_K, FINAL_TM, half), lambda i: (0, i, 0)),
                  pl.BlockSpec((FINAL_TM, TOP_K), row),
                  pl.BlockSpec((FINAL_TM, D_MODEL), row)] + [_const_spec(c.shape) for c in consts],
        out_specs=pl.BlockSpec((FINAL_TM, D_MODEL), row),
        out_shape=jax.ShapeDtypeStruct((n_tok, D_MODEL), F32),
        compiler_params=pltpu.CompilerParams(dimension_semantics=("arbitrary",)),
        name="final",
    )(rows, gates_t, h, *consts)


def _np_consts():
    idx = np.arange(QUAD)
    g256 = (idx[:, None] // HEAD_DIM == idx[None, :] // HEAD_DIM).astype(np.float32)
    t = np.arange(CUM_TILE)
    same = t[:, None] // SCAN_CHUNK == t[None, :] // SCAN_CHUNK
    ltri = (same & (t[None, :] <= t[:, None])).astype(np.float32)
    return jnp.asarray(g256, BF), jnp.asarray(ltri, BF)


def _block_plan(counts, n_assign):
    padded = (counts + FFN_ROWS - 1) // FFN_ROWS * FFN_ROWS
    base = jnp.cumsum(padded) - padded
    n_blocks = (n_assign + N_EXPERTS * (FFN_ROWS - 1) + FFN_ROWS - 1) // FFN_ROWS
    return base.astype(jnp.int32), n_blocks * FFN_ROWS


def _layer(x2, seq, w_in, shift_mu, w0, decay_lora, a0, iclr_lora, gate_lora, k_k, k_a, r_k, gn_w, gn_b,
           lnw, lnb, spatial_w, spatial_b, w_o_rwkv, w_o_gmlp, w_out, ln1_w, ln1_b,
           w_router, router_bias, w_exp_gate_up, w_exp_down, w_sh_gate_up, w_sh_down, ln2_w, ln2_b):
    n_tok = x2.shape[0]
    c_rwkv = C_RKV + C_LORA
    pad = C_LORA_PAD - C_LORA
    win = jnp.concatenate([w_in[:, :c_rwkv], jnp.zeros((D_MODEL, pad), F32), w_in[:, c_rwkv:]], axis=1).astype(BF)
    mu = jnp.pad(shift_mu, (0, pad)).reshape(1, C_RWKV_PAD)
    row1 = lambda t: t.reshape(1, -1)
    w1 = jnp.zeros((LANES, 2 * D_MODEL), F32)
    w1 = w1.at[:DECAY_LORA, :D_MODEL].set(decay_lora).at[DECAY_LORA:, D_MODEL:].set(iclr_lora).astype(BF)
    w2 = jnp.zeros((C_LORA_PAD - LANES, D_MODEL), F32).at[:GATE_LORA].set(gate_lora).astype(BF)
    g256, ltri = _np_consts()
    sb_map = jnp.repeat(spatial_b.T, D_MODEL // GMLP_GROUPS, axis=1)

    outs = _front(x2, win, mu, row1(w0), row1(a0), w1, w2, row1(k_k), row1(k_a), row1(r_k),
                  g256, ltri, row1(lnw), row1(lnb), spatial_w, sb_map, w_o_gmlp.astype(BF), seq)
    quads, wc, (g, bg, ga, zb) = outs[:7], outs[7], outs[8:]
    h1, h1p = _scan(quads, wc, g, bg, ga, zb, x2, row1(gn_w), row1(gn_b), g256,
                    w_o_rwkv.astype(BF), w_out.astype(BF), row1(ln1_w), row1(ln1_b), seq)

    wrt = w_router.T
    wrh = wrt.astype(BF)
    wrl = (wrt - wrh.astype(F32)).astype(BF)
    tt = np.arange(ROUTE_TM)
    ustrict = jnp.asarray((tt[:, None] < tt[None, :]).astype(np.float32), BF)
    top_e, gates, rank, counts = _route(h1, wrh, wrl, router_bias.reshape(N_EXPERTS, 1), ustrict)
    counts = counts[:, 0]
    base, n_rows = _block_plan(counts, n_tok * TOP_K)
    pos = _rows(top_e, rank, base.reshape(N_EXPERTS, 1))
    xs = _sc_scatter_rows(h1p, pos, n_rows)
    rows = _ffn(base, counts, xs, w_exp_gate_up, w_exp_down)
    picked = _sc_gather_rows(rows, pos.reshape(-1)).reshape(TOP_K, n_tok, D_MODEL // 2)
    return _final(picked, gates.T, h1, w_sh_gate_up.astype(BF), w_sh_down.astype(BF),
                  row1(ln2_w), row1(ln2_b))


def kernel(x, w_in, shift_mu, rwkv_w0, rwkv_decay_lora, rwkv_a0, rwkv_iclr_lora, rwkv_gate_lora, rwkv_k_k, rwkv_k_a, rwkv_r_k, rwkv_gn_w, rwkv_gn_b, gmlp_ln_w, gmlp_ln_b, gmlp_spatial_w, gmlp_spatial_b, w_o_rwkv, w_o_gmlp, w_out, ln1_w, ln1_b, w_router, router_bias, w_exp_gate_up, w_exp_down, w_sh_gate_up, w_sh_down, ln2_w, ln2_b):
    bsz, seq, d = x.shape
    h = x.reshape(bsz * seq, d)
    for l in range(DEPTH):
        h = _layer(h, seq, w_in[l], shift_mu[l], rwkv_w0[l], rwkv_decay_lora[l], rwkv_a0[l],
                   rwkv_iclr_lora[l], rwkv_gate_lora[l], rwkv_k_k[l], rwkv_k_a[l], rwkv_r_k[l],
                   rwkv_gn_w[l], rwkv_gn_b[l], gmlp_ln_w[l], gmlp_ln_b[l], gmlp_spatial_w[l],
                   gmlp_spatial_b[l], w_o_rwkv[l], w_o_gmlp[l], w_out[l], ln1_w[l], ln1_b[l],
                   w_router[l], router_bias[l], w_exp_gate_up[l], w_exp_down[l], w_sh_gate_up[l],
                   w_sh_down[l], ln2_w[l], ln2_b[l])
    return h.reshape(bsz, seq, d)
```

```python
import functools

import numpy as np
import jax
import jax.numpy as jnp
from jax import lax
from jax.experimental import pallas as pl
from jax.experimental.pallas import tpu as pltpu
from jax.experimental.pallas import tpu_sc as plsc

F32 = jnp.float32
BF = jnp.bfloat16

D_MODEL = 1024
HEAD_DIM = 64
N_HEADS = 16
DECAY_LORA = 64
ICLR_LORA = 64
GATE_LORA = 160
GN_EPS = 64e-5
GMLP_GROUPS = 8
GMLP_CHUNK = 128
N_EXPERTS = 256
TOP_K = 8
N_EXPERT_GROUPS = 8
TOPK_GROUPS = 4
EXPERT_FF = 256
SHARED_FF = 256
ROUTED_SCALE = 2.5
LN_EPS = 1e-5
DEPTH = 1
ALPHA = (2 * DEPTH) ** 0.25

LANES = 128
V7X_VMEM_LIMIT_BYTES = 56 * 1024 * 1024

C_RKV = 3 * D_MODEL
C_LORA = DECAY_LORA + ICLR_LORA + GATE_LORA
C_LORA_PAD = 384
C_RWKV_PAD = C_RKV + C_LORA_PAD
C_GMLP = 2 * D_MODEL
C_GATES = 2 * D_MODEL
C_IN_PAD = C_RWKV_PAD + C_GMLP + C_GATES

SCAN_CHUNK = 64
QUAD = 4 * HEAD_DIM
N_QUADS = D_MODEL // QUAD
TM = 512
CUM_TILE = 256
SCAN_TM = 512
ROUTE_TM = 1024
FFN_ROWS = 256
FINAL_TM = 512


def _dot(a, b):
    return jnp.dot(a, b, preferred_element_type=F32)


def _dot_nt(a, b):
    return lax.dot_general(a, b, (((1,), (1,)), ((), ())), preferred_element_type=F32)


def _dot_tn(a, b):
    return lax.dot_general(a, b, (((0,), (0,)), ((), ())), preferred_element_type=F32)


def _split2(z):
    hi = z.astype(BF)
    lo = (z - hi.astype(F32)).astype(BF)
    return hi, lo


def _headsum(z, g256, passes=2):
    outs = []
    for j in range(N_QUADS):
        zj = z[:, QUAD * j:QUAD * (j + 1)]
        if passes == 1:
            outs.append(_dot(zj.astype(BF), g256))
        else:
            hi, lo = _split2(zj)
            outs.append(_dot(hi, g256) + _dot(lo, g256))
    return jnp.concatenate(outs, axis=1)


def _layer_norm(v, w, b):
    mu = jnp.mean(v, axis=-1, keepdims=True)
    d = v - mu
    var = jnp.mean(d * d, axis=-1, keepdims=True)
    return d * lax.rsqrt(var + LN_EPS) * w + b


def _const_spec(shape):
    nd = len(shape)
    return pl.BlockSpec(shape, lambda *_: (0,) * nd, pipeline_mode=pl.Buffered(1))


def _front_kernel(x_ref, win_ref, mu_ref, w0_ref, a0_ref, w1_ref, w2_ref, kk_ref, ka_ref, rk_ref,
                  g256_ref, ltri_ref, lnw_ref, lnb_ref, ws_ref, sb_ref, wog_ref,
                  rt_ref, at_ref, bt_ref, kt_ref, vv_ref, bh_ref, kh_ref, wc_ref,
                  g_ref, bg_ref, ga_ref, zb_ref, prev_ref, xb_ref, *, tiles_per_seq):
    i = pl.program_id(0)
    first = (i % tiles_per_seq) == 0
    xb_ref[...] = x_ref[...].astype(BF)
    is_row0 = lax.broadcasted_iota(jnp.int32, (TM, 1), 0) == 0
    W = QUAD

    def proj(lo, width=W):
        return _dot(xb_ref[...], win_ref[:, lo:lo + width])

    def shift_mix(lo, width=W):
        p = proj(lo, width)
        prev = jnp.where(first, 0.0, prev_ref[:, lo:lo + width])
        sh = jnp.where(is_row0, prev, pltpu.roll(p, 1, 0))
        prev_ref[:, lo:lo + width] = p[TM - 1:TM, :]
        return p + mu_ref[:, lo:lo + width] * (sh - p)

    lora = shift_mix(C_RKV, C_LORA_PAD)
    lane = lax.broadcasted_iota(jnp.int32, (1, LANES), 1)
    act1 = jnp.where(lane < DECAY_LORA, jnp.tanh(lora[:, :LANES]), lora[:, :LANES]).astype(BF)
    act2 = jax.nn.sigmoid(lora[:, LANES:]).astype(BF)
    g256 = g256_ref[...]
    for q in range(N_QUADS):
        c0 = W * q
        cols = slice(c0, c0 + W)
        r = shift_mix(c0)
        k = shift_mix(D_MODEL + c0)
        v = shift_mix(2 * D_MODEL + c0)
        w = -jax.nn.softplus(-(w0_ref[:, cols] + _dot(act1, w1_ref[:, cols]))) - 0.5
        logd = -jnp.exp(w)
        a = jax.nn.sigmoid(a0_ref[:, cols] + _dot(act1, w1_ref[:, D_MODEL + c0:D_MODEL + c0 + W]))
        g = _dot(act2, w2_ref[:, cols])

        kkr = k * kk_ref[:, cols]
        kk = kkr * lax.rsqrt(jnp.maximum(_dot((kkr * kkr).astype(BF), g256), 1e-24))
        kmod = k * (1.0 + (a - 1.0) * ka_ref[:, cols])
        bonus = _dot((r * kmod * rk_ref[:, cols]).astype(BF), g256) * v
        kka = kk * a

        parts = _split2(logd)
        cum = jnp.concatenate(
            [sum(_dot(ltri_ref[...], part[CUM_TILE * hh:CUM_TILE * (hh + 1)]) for part in parts)
             for hh in range(TM // CUM_TILE)], axis=0)
        e_inv = jnp.exp(-cum)
        bt = kka * e_inv
        kt = kmod * e_inv
        ends = []
        for c in range(TM // SCAN_CHUNK):
            last = SCAN_CHUNK * (c + 1) - 1
            ewc_row = jnp.exp(cum[last:last + 1, :])
            wc_ref[c, q:q + 1, :] = ewc_row
            ends.append(jnp.broadcast_to(ewc_row, (SCAN_CHUNK, W)))
        ewc = jnp.concatenate(ends, axis=0)
        rt_ref[q] = (r * jnp.exp(cum)).astype(BF)
        at_ref[q] = (-kk * jnp.exp(cum - logd)).astype(BF)
        bt_ref[q] = bt.astype(BF)
        kt_ref[q] = kt.astype(BF)
        vv_ref[q] = v.astype(BF)
        bh_ref[q] = (bt * ewc).astype(BF)
        kh_ref[q] = (kt * ewc).astype(BF)
        g_ref[:, cols] = g.astype(BF)
        bg_ref[:, cols] = (bonus * g).astype(BF)

    gelu = lambda t: 0.5 * t * (1.0 + lax.erf(t * np.float32(np.sqrt(0.5))))
    u0 = C_RWKV_PAD
    zv = [gelu(proj(u0 + D_MODEL + W * j)) for j in range(N_QUADS)]
    mean = sum(jnp.sum(z, axis=-1, keepdims=True) for z in zv) * (1.0 / D_MODEL)
    dv = [z - mean for z in zv]
    var = sum(jnp.sum(d * d, axis=-1, keepdims=True) for d in dv) * (1.0 / D_MODEL)
    rstd = lax.rsqrt(var + LN_EPS)
    trow = lax.broadcasted_iota(jnp.int32, (GMLP_CHUNK, GMLP_CHUNK), 0)
    tcol = lax.broadcasted_iota(jnp.int32, (GMLP_CHUNK, GMLP_CHUNK), 1)
    causal = tcol <= trow
    n_time = TM // GMLP_CHUNK
    yb = jnp.zeros((TM, D_MODEL), F32)
    for j in range(N_QUADS):
        cols = slice(W * j, W * (j + 1))
        vn = (dv[j] * rstd * lnw_ref[:, cols] + lnb_ref[:, cols]).astype(BF)
        svs = []
        for gl in range(W // LANES):
            wsg = jnp.where(causal, ws_ref[(W // LANES) * j + gl], 0.0).astype(BF)
            svs.append(jnp.concatenate(
                [_dot(wsg, vn[GMLP_CHUNK * cc:GMLP_CHUNK * (cc + 1), LANES * gl:LANES * (gl + 1)])
                 for cc in range(n_time)], axis=0))
        sb = sb_ref[:, cols]
        sv = jnp.concatenate(svs, axis=1) + jnp.concatenate([sb] * n_time, axis=0)
        u = gelu(proj(u0 + W * j))
        yb = yb + _dot((u * sv).astype(BF), wog_ref[W * j:W * (j + 1), :])

    g0 = C_RWKV_PAD + C_GMLP
    for j in range(N_QUADS):
        cols = slice(W * j, W * (j + 1))
        ga_ref[:, cols] = jax.nn.sigmoid(proj(g0 + W * j)).astype(BF)
        zb_ref[:, cols] = (jax.nn.sigmoid(proj(g0 + D_MODEL + W * j)) * yb[:, cols]).astype(BF)


def _front(x2, win, mu, w0, a0, w1, w2, k_k, k_a, r_k, g256, ltri, lnw, lnb, ws, sb, wog, seq):
    n_tok = x2.shape[0]
    n_tiles = n_tok // TM
    row = lambda i: (i, 0)
    quad_spec = pl.BlockSpec((N_QUADS, TM, QUAD), lambda i: (0, i, 0))
    quad_shape = jax.ShapeDtypeStruct((N_QUADS, n_tok, QUAD), BF)
    full_spec = pl.BlockSpec((TM, D_MODEL), row)
    full_shape = jax.ShapeDtypeStruct((n_tok, D_MODEL), BF)
    wc_spec = pl.BlockSpec((TM // SCAN_CHUNK, N_QUADS, QUAD), lambda i: (i, 0, 0))
    wc_shape = jax.ShapeDtypeStruct((n_tok // SCAN_CHUNK, N_QUADS, QUAD), F32)
    consts = (win, mu, w0, a0, w1, w2, k_k, k_a, r_k, g256, ltri, lnw, lnb, ws, sb, wog)
    return pl.pallas_call(
        functools.partial(_front_kernel, tiles_per_seq=seq // TM),
        grid=(n_tiles,),
        in_specs=[pl.BlockSpec((TM, D_MODEL), row)] + [_const_spec(c.shape) for c in consts],
        out_specs=[quad_spec] * 7 + [wc_spec] + [full_spec] * 4,
        out_shape=[quad_shape] * 7 + [wc_shape] + [full_shape] * 4,
        scratch_shapes=[pltpu.VMEM((1, C_RWKV_PAD), F32), pltpu.VMEM((TM, D_MODEL), BF)],
        compiler_params=pltpu.CompilerParams(
            dimension_semantics=("arbitrary",), vmem_limit_bytes=V7X_VMEM_LIMIT_BYTES),
        name="front",
    )(x2, *consts)


def _scan_kernel(rt_ref, at_ref, bt_ref, kt_ref, vv_ref, bh_ref, kh_ref, wc_ref,
                 g_ref, bg_ref, ga_ref, zb_ref, x_ref, gnw_ref, gnb_ref, g256_ref,
                 wor_ref, wout_ref, ln1w_ref, ln1b_ref,
                 h_ref, hp_ref, s_ref, *, tiles_per_seq):
    i = pl.program_id(0)

    @pl.when((i % tiles_per_seq) == 0)
    def _():
        s_ref[...] = jnp.zeros_like(s_ref)

    lane = lax.broadcasted_iota(jnp.int32, (1, QUAD), 1)
    lane_head = lane // HEAD_DIM
    lane_pos = lane % HEAD_DIM
    trow = lax.broadcasted_iota(jnp.int32, (SCAN_CHUNK, 1), 0)
    strict = lane_pos < trow
    incl = lane_pos <= trow
    row_head = lax.broadcasted_iota(jnp.int32, (QUAD, 1), 0) // HEAD_DIM
    diag_blocks = row_head == lane_head

    def bd4(m):
        zero = jnp.zeros_like(m)
        return jnp.concatenate([jnp.where(lane_head == j, m, zero) for j in range(4)], axis=0)

    n_chunks = SCAN_TM // SCAN_CHUNK
    items = [(c, q) for c in range(n_chunks) for q in range(N_QUADS)]
    rows_of = lambda c: slice(SCAN_CHUNK * c, SCAN_CHUNK * (c + 1))
    C = SCAN_CHUNK

    at, rt, bdv, a_ab, a_akrk, a_rb = {}, {}, {}, {}, {}, {}
    for it in items:
        c, q = it
        at[it] = at_ref[q, rows_of(c), :]
        rt[it] = rt_ref[q, rows_of(c), :]
        ra = jnp.concatenate([at[it], rt[it]], axis=0)
        gb = _dot_nt(ra, bd4(bt_ref[q, rows_of(c), :]))
        gk = _dot_nt(ra, bd4(kt_ref[q, rows_of(c), :]))
        a_ab[it] = jnp.where(strict, gb[:C], 0.0)
        a_rb[it] = jnp.where(incl, gb[C:], 0.0).astype(BF)
        a_akrk[it] = jnp.concatenate(
            [jnp.where(strict, gk[:C], 0.0), jnp.where(incl, gk[C:], 0.0)], axis=0).astype(BF)
        bdv[it] = bd4(vv_ref[q, rows_of(c), :])

    tm, qpow = {}, {}
    for it in items:
        qb = a_ab[it].astype(BF)
        tm[it] = a_ab[it]
        qpow[it] = _dot(qb, bd4(qb))
    for level in range(1, 6):
        for it in items:
            qb = qpow[it].astype(BF)
            bdq = bd4(qb)
            if level < 5:
                both = _dot(jnp.concatenate([tm[it].astype(BF), qb], axis=0), bdq)
                tm[it] = tm[it] + qpow[it] + both[:C]
                qpow[it] = both[C:]
            else:
                tm[it] = tm[it] + qpow[it] + _dot(tm[it].astype(BF), bdq)

    awr, u0, rkv = {}, {}, {}
    for it in items:
        tb = tm[it].astype(BF)
        aw = at[it].astype(F32) + _dot(tb, bd4(at[it]))
        awr[it] = jnp.concatenate([aw.astype(BF), rt[it]], axis=0)
        kv = _dot(a_akrk[it], bdv[it])
        u0[it] = kv[:C] + _dot(tb, bd4(kv[:C].astype(BF)))
        rkv[it] = kv[C:]

    state = [s_ref[q] for q in range(N_QUADS)]
    ys = {}
    for c in range(n_chunks):
        for q in range(N_QUADS):
            it = (c, q)
            us = _dot_nt(awr[it], state[q].astype(BF))
            ub = (u0[it] + us[:C]).astype(BF)
            ys[it] = us[C:] + rkv[it] + _dot(a_rb[it], bd4(ub))
            uv = jnp.concatenate([ub, vv_ref[q, rows_of(c), :]], axis=0)
            bk = jnp.concatenate([bh_ref[q, rows_of(c), :], kh_ref[q, rows_of(c), :]], axis=0)
            wc = wc_ref[c, q:q + 1, :]
            state[q] = state[q] * wc + jnp.where(diag_blocks, _dot_tn(uv, bk), 0.0)
    for q in range(N_QUADS):
        s_ref[q] = state[q]

    g256 = g256_ref[...]
    y = jnp.concatenate(
        [jnp.concatenate([ys[(c, q)] for c in range(n_chunks)], axis=0) for q in range(N_QUADS)], axis=1)
    inv_n = 1.0 / HEAD_DIM
    d = y - _headsum(y, g256) * inv_n
    var = _headsum(d * d, g256) * inv_n
    yn = d * lax.rsqrt(var + GN_EPS) * gnw_ref[...] + gnb_ref[...]
    ya = _dot((yn * g_ref[...].astype(F32) + bg_ref[...].astype(F32)).astype(BF), wor_ref[...])
    mix_in = ga_ref[...].astype(F32) * ya + zb_ref[...].astype(F32)
    mix = _dot(mix_in.astype(BF), wout_ref[...])
    h = _layer_norm(ALPHA * x_ref[...] + mix, ln1w_ref[...], ln1b_ref[...])
    h_ref[...] = h
    hb = pltpu.bitcast(h.astype(BF).astype(F32), jnp.uint32)
    half = D_MODEL // 2
    hp_ref[...] = (hb[:, :half] >> 16) | (hb[:, half:] & jnp.uint32(0xFFFF0000))


def _scan(quads, wc, g, bg, ga, zb, x2, gnw, gnb, g256, wor, wout, ln1w, ln1b, seq):
    n_tok = x2.shape[0]
    n_tiles = n_tok // SCAN_TM
    row = lambda i: (i, 0)
    quad_spec = pl.BlockSpec((N_QUADS, SCAN_TM, QUAD), lambda i: (0, i, 0))
    full_spec = pl.BlockSpec((SCAN_TM, D_MODEL), row)
    wc_spec = pl.BlockSpec((SCAN_TM // SCAN_CHUNK, N_QUADS, QUAD), lambda i: (i, 0, 0))
    consts = (gnw, gnb, g256, wor, wout, ln1w, ln1b)
    return pl.pallas_call(
        functools.partial(_scan_kernel, tiles_per_seq=seq // SCAN_TM),
        grid=(n_tiles,),
        in_specs=[quad_spec] * 7 + [wc_spec]
        + [full_spec] * 5 + [_const_spec(c.shape) for c in consts],
        out_specs=[full_spec, pl.BlockSpec((SCAN_TM, D_MODEL // 2), row)],
        out_shape=[jax.ShapeDtypeStruct((n_tok, D_MODEL), F32),
                   jax.ShapeDtypeStruct((n_tok, D_MODEL // 2), jnp.uint32)],
        scratch_shapes=[pltpu.VMEM((N_QUADS, QUAD, QUAD), F32)],
        compiler_params=pltpu.CompilerParams(
            dimension_semantics=("arbitrary",), vmem_limit_bytes=V7X_VMEM_LIMIT_BYTES),
        name="scan",
    )(*quads, wc, g, bg, ga, zb, x2, *consts)


def _first_max(x, idx, n):
    m = jnp.max(x, axis=0, keepdims=True)
    first = jnp.min(jnp.where(x == m, idx, n), axis=0, keepdims=True)
    return m, first


def _route_kernel(h_ref, wrh_ref, wrl_ref, bias_ref, ustrict_ref, e_ref, gate_ref, rank_ref, cnt_ref,
                  carry_ref):
    @pl.when(pl.program_id(0) == 0)
    def _():
        carry_ref[...] = jnp.zeros_like(carry_ref)

    hh, hl = _split2(h_ref[...])
    wrh = wrh_ref[...]
    logits = _dot_nt(wrh, hh) + _dot_nt(wrh, hl) + _dot_nt(wrl_ref[...], hh)
    scores = jax.nn.sigmoid(logits)
    choice = scores + bias_ref[...]
    n_tok = choice.shape[1]
    per = N_EXPERTS // N_EXPERT_GROUPS
    eidx = lax.broadcasted_iota(jnp.int32, (N_EXPERTS, 1), 0)
    gidx = lax.broadcasted_iota(jnp.int32, (N_EXPERT_GROUPS, 1), 0)
    lidx = lax.broadcasted_iota(jnp.int32, (per, 1), 0)
    neg = -jnp.inf

    gsc = []
    for gi in range(N_EXPERT_GROUPS):
        blk = choice[per * gi:per * (gi + 1), :]
        m1, i1 = _first_max(blk, lidx, per)
        m2 = jnp.max(jnp.where(lidx == i1, neg, blk), axis=0, keepdims=True)
        gsc.append(m1 + m2)
    gscore = jnp.concatenate(gsc, axis=0)
    gsel = jnp.zeros(gscore.shape, jnp.bool_)
    for _ in range(TOPK_GROUPS):
        _, gi1 = _first_max(gscore, gidx, N_EXPERT_GROUPS)
        hit = gidx == gi1
        gsel = gsel | hit
        gscore = jnp.where(hit, neg, gscore)
    emask = jnp.concatenate(
        [jnp.broadcast_to(gsel[gi:gi + 1, :], (per, n_tok)) for gi in range(N_EXPERT_GROUPS)], axis=0)
    cand = jnp.where(emask, choice, neg)
    es, gs, hits = [], [], []
    for _ in range(TOP_K):
        _, e1 = _first_max(cand, eidx, N_EXPERTS)
        hit = eidx == e1
        es.append(e1)
        hits.append(hit)
        gs.append(jnp.sum(jnp.where(hit, scores, 0.0), axis=0, keepdims=True))
        cand = jnp.where(hit, neg, cand)
    gates = jnp.concatenate(gs, axis=0)
    gates = gates / jnp.sum(gates, axis=0, keepdims=True) * ROUTED_SCALE
    e_ref[...] = jnp.concatenate(es, axis=0)
    gate_ref[...] = gates

    chosen = functools.reduce(jnp.logical_or, hits)
    sel = jnp.where(chosen, 1.0, 0.0)
    before = _dot(sel.astype(BF), ustrict_ref[...]) + carry_ref[:, 0:1]
    rank_ref[...] = jnp.concatenate(
        [jnp.sum(jnp.where(hit, before, 0.0), axis=0, keepdims=True) for hit in hits],
        axis=0).astype(jnp.int32)
    total = before[:, n_tok - 1:n_tok] + sel[:, n_tok - 1:n_tok]
    carry_ref[...] = jnp.broadcast_to(total, carry_ref.shape)
    cnt_ref[...] = jnp.broadcast_to(total, cnt_ref.shape).astype(jnp.int32)


def _route(h, wrh, wrl, bias_col, ustrict):
    n_tok = h.shape[0]
    col = lambda i: (0, i)
    tok_spec = pl.BlockSpec((TOP_K, ROUTE_TM), col)
    return pl.pallas_call(
        _route_kernel,
        grid=(n_tok // ROUTE_TM,),
        in_specs=[pl.BlockSpec((ROUTE_TM, D_MODEL), lambda i: (i, 0)),
                  _const_spec(wrh.shape), _const_spec(wrl.shape), _const_spec(bias_col.shape),
                  _const_spec(ustrict.shape)],
        out_specs=[tok_spec, tok_spec, tok_spec, pl.BlockSpec((N_EXPERTS, LANES), lambda i: (0, 0))],
        out_shape=[jax.ShapeDtypeStruct((TOP_K, n_tok), jnp.int32),
                   jax.ShapeDtypeStruct((TOP_K, n_tok), F32),
                   jax.ShapeDtypeStruct((TOP_K, n_tok), jnp.int32),
                   jax.ShapeDtypeStruct((N_EXPERTS, LANES), jnp.int32)],
        scratch_shapes=[pltpu.VMEM((N_EXPERTS, LANES), F32)],
        compiler_params=pltpu.CompilerParams(dimension_semantics=("arbitrary",)),
        name="route",
    )(h, wrh, wrl, bias_col, ustrict)


def _rows_kernel(e_ref, rank_ref, base_ref, pos_ref):
    eidx = lax.broadcasted_iota(jnp.int32, (N_EXPERTS, 1), 0)
    base = base_ref[...]
    pos_ref[...] = rank_ref[...] + jnp.concatenate(
        [jnp.sum(jnp.where(eidx == e_ref[kk:kk + 1, :], base, 0), axis=0, keepdims=True)
         for kk in range(TOP_K)], axis=0)


def _rows(top_e, rank, base_col):
    n_tok = top_e.shape[1]
    tok_spec = pl.BlockSpec((TOP_K, ROUTE_TM), lambda i: (0, i))
    return pl.pallas_call(
        _rows_kernel,
        grid=(n_tok // ROUTE_TM,),
        in_specs=[tok_spec, tok_spec, _const_spec(base_col.shape)],
        out_specs=tok_spec,
        out_shape=jax.ShapeDtypeStruct((TOP_K, n_tok), jnp.int32),
        compiler_params=pltpu.CompilerParams(dimension_semantics=("arbitrary",)),
        name="rows",
    )(top_e, rank, base_col)


SC_WINDOW = 128


def _sc_mesh():
    return plsc.VectorSubcoreMesh(core_axis_name="c", subcore_axis_name="s")


def _sc_workers():
    info = plsc.get_sparse_core_info()
    return info.num_cores, info.num_cores * info.num_subcores


def _sc_scatter_rows(src, pos, n_rows):
    n_tok, width = src.shape
    n_k = pos.shape[0]
    n_cores, n_workers = _sc_workers()
    n_win = n_tok // (n_workers * SC_WINDOW)

    @functools.partial(
        pl.kernel, out_type=jax.ShapeDtypeStruct((n_rows, width), src.dtype), mesh=_sc_mesh(),
        scratch_types=[pltpu.VMEM((n_k * n_win, SC_WINDOW), jnp.int32),
                       pltpu.VMEM((SC_WINDOW, width), src.dtype),
                       pltpu.SemaphoreType.DMA])
    def scatter(src_hbm, pos_hbm, out_hbm, idx_v, rows_v, sem):
        wid = lax.axis_index("s") * n_cores + lax.axis_index("c")
        pltpu.sync_copy(pos_hbm.at[wid], idx_v)

        @pl.loop(0, n_win)
        def _(j):
            pltpu.sync_copy(src_hbm.at[pl.ds((wid * n_win + j) * SC_WINDOW, SC_WINDOW)], rows_v)
            copies = [pltpu.async_copy(rows_v, out_hbm.at[idx_v.at[kk * n_win + j]], sem)
                      for kk in range(n_k)]
            for cp in copies:
                cp.wait()

    pos4 = pos.reshape(n_k, n_workers, n_win, SC_WINDOW).transpose(1, 0, 2, 3)
    return scatter(src, pos4.reshape(n_workers, n_k * n_win, SC_WINDOW))


def _sc_gather_rows(src, idx):
    n_idx = idx.shape[0]
    width = src.shape[1]
    n_cores, n_workers = _sc_workers()
    n_win = n_idx // (n_workers * SC_WINDOW)

    @functools.partial(
        pl.kernel, out_type=jax.ShapeDtypeStruct((n_idx, width), src.dtype), mesh=_sc_mesh(),
        scratch_types=[pltpu.VMEM((n_win, SC_WINDOW), jnp.int32),
                       pltpu.VMEM((SC_WINDOW, width), src.dtype),
                       pltpu.SemaphoreType.DMA])
    def gather(src_hbm, idx_hbm, out_hbm, idx_v, rows_v, sem):
        wid = lax.axis_index("s") * n_cores + lax.axis_index("c")
        pltpu.sync_copy(idx_hbm.at[wid], idx_v)

        @pl.loop(0, n_win)
        def _(j):
            pltpu.async_copy(src_hbm.at[idx_v.at[j]], rows_v, sem).wait()
            pltpu.sync_copy(rows_v, out_hbm.at[pl.ds((wid * n_win + j) * SC_WINDOW, SC_WINDOW)])

    return gather(src, idx.reshape(n_workers, n_win, SC_WINDOW))


def _unpack_pair(words):
    lo = pltpu.bitcast(words << 16, F32).astype(BF)
    hi = pltpu.bitcast(words & jnp.uint32(0xFFFF0000), F32).astype(BF)
    return lo, hi


def _pack_pair(val):
    bits = pltpu.bitcast(val.astype(BF).astype(F32), jnp.uint32)
    half = val.shape[1] // 2
    return (bits[:, :half] >> 16) | (bits[:, half:] & jnp.uint32(0xFFFF0000))


FFN_SLOTS = 6
FFN_AHEAD = FFN_SLOTS - 1
ROW_FETCH_PRIORITY = 1


def _ffn_kernel(base_ref, cnt_ref, xs_ref, wgu_ref, wd_ref, out_ref,
                xbuf, obuf, wgu16_ref, wd16_ref, sem_in, sem_out, pend_ref, *, n_blocks):
    e = pl.program_id(0)
    n_rows_e = cnt_ref[e]
    n_blk = (n_rows_e + FFN_ROWS - 1) // FFN_ROWS
    first_blk = base_ref[e] // FFN_ROWS
    last_cnt = cnt_ref[N_EXPERTS - 1]
    used = base_ref[N_EXPERTS - 1] // FFN_ROWS + (last_cnt + FFN_ROWS - 1) // FFN_ROWS
    half = D_MODEL // 2

    def in_copy(blk, slot):
        row0 = pl.multiple_of(blk * FFN_ROWS, FFN_ROWS)
        return pltpu.make_async_copy(xs_ref.at[pl.ds(row0, FFN_ROWS), :], xbuf.at[slot], sem_in.at[slot])

    def out_copy(blk, slot):
        row0 = pl.multiple_of(blk * FFN_ROWS, FFN_ROWS)
        return pltpu.make_async_copy(obuf.at[slot], out_ref.at[pl.ds(row0, FFN_ROWS), :], sem_out.at[slot])

    @pl.when(e == 0)
    def _():
        for slot in range(FFN_SLOTS):
            pend_ref[slot] = 0
        for blk in range(FFN_AHEAD):
            @pl.when(blk < used)
            def _():
                in_copy(blk, blk).start(priority=ROW_FETCH_PRIORITY)

    @pl.when(n_blk > 0)
    def _():
        wgu16_ref[...] = wgu_ref[0].astype(BF)
        wd16_ref[...] = wd_ref[0].astype(BF)

    def fetch_ahead(blk):
        @pl.when(blk + FFN_AHEAD < used)
        def _():
            in_copy(blk + FFN_AHEAD, (blk + FFN_AHEAD) % FFN_SLOTS).start(priority=ROW_FETCH_PRIORITY)

    def ffn_rows(words, j):
        row = lax.broadcasted_iota(jnp.int32, (words.shape[0], 1), 0)
        lo, hi = _unpack_pair(jnp.where(row < n_rows_e - j * FFN_ROWS, words, jnp.uint32(0)))
        gu = _dot(lo, wgu16_ref[0:half, :]) + _dot(hi, wgu16_ref[half:, :])
        act = (jax.nn.silu(gu[:, :EXPERT_FF]) * gu[:, EXPERT_FF:]).astype(BF)
        return _pack_pair(_dot(act, wd16_ref[...]))

    def write_back(blk, slot, packed):
        @pl.when(pend_ref[slot] == 1)
        def _():
            out_copy(blk, slot).wait()

        obuf[slot] = packed
        out_copy(blk, slot).start()
        pend_ref[slot] = 1

    def single(j):
        blk = first_blk + j
        slot = blk % FFN_SLOTS
        in_copy(blk, slot).wait()
        fetch_ahead(blk)
        write_back(blk, slot, ffn_rows(xbuf[slot], j))

    def pair(p, carry):
        j = lead + 2 * p
        blk = first_blk + j
        slot = pl.multiple_of(blk % FFN_SLOTS, 2)
        in_copy(blk, slot).wait()
        in_copy(blk + 1, slot + 1).wait()
        fetch_ahead(blk)
        packed = ffn_rows(xbuf[pl.ds(slot, 2)].reshape(2 * FFN_ROWS, half), j)
        fetch_ahead(blk + 1)
        write_back(blk, slot, packed[:FFN_ROWS])
        write_back(blk + 1, slot + 1, packed[FFN_ROWS:])
        return carry

    lead = jnp.where(n_blk > 0, first_blk % 2, 0)

    @pl.when(lead == 1)
    def _():
        single(0)

    lax.fori_loop(0, (n_blk - lead) // 2, pair, 0)

    @pl.when((n_blk - lead) % 2 == 1)
    def _():
        single(n_blk - 1)

    @pl.when(e == N_EXPERTS - 1)
    def _():
        for slot in range(FFN_SLOTS):
            @pl.when(pend_ref[slot] == 1)
            def _():
                out_copy(0, slot).wait()
        obuf[0] = jnp.zeros((FFN_ROWS, half), jnp.uint32)

        def fill(blk, carry):
            cp = out_copy(blk, 0)
            cp.start()
            cp.wait()
            return carry

        lax.fori_loop(used, n_blocks, fill, 0)


def _ffn(base, counts, xs, wgu, wd):
    n_rows = xs.shape[0]
    half = D_MODEL // 2
    grid_spec = pltpu.PrefetchScalarGridSpec(
        num_scalar_prefetch=2,
        grid=(N_EXPERTS,),
        in_specs=[pl.BlockSpec(memory_space=pl.ANY),
                  pl.BlockSpec((1, D_MODEL, 2 * EXPERT_FF), lambda e, base, cnt: (e, 0, 0)),
                  pl.BlockSpec((1, EXPERT_FF, D_MODEL), lambda e, base, cnt: (e, 0, 0))],
        out_specs=pl.BlockSpec(memory_space=pl.ANY),
        scratch_shapes=[pltpu.VMEM((FFN_SLOTS, FFN_ROWS, half), jnp.uint32),
                        pltpu.VMEM((FFN_SLOTS, FFN_ROWS, half), jnp.uint32),
                        pltpu.VMEM((D_MODEL, 2 * EXPERT_FF), BF),
                        pltpu.VMEM((EXPERT_FF, D_MODEL), BF),
                        pltpu.SemaphoreType.DMA((FFN_SLOTS,)),
                        pltpu.SemaphoreType.DMA((FFN_SLOTS,)),
                        pltpu.SMEM((FFN_SLOTS,), jnp.int32)],
    )
    return pl.pallas_call(
        functools.partial(_ffn_kernel, n_blocks=n_rows // FFN_ROWS),
        grid_spec=grid_spec,
        out_shape=jax.ShapeDtypeStruct((n_rows, half), jnp.uint32),
        compiler_params=pltpu.CompilerParams(dimension_semantics=("arbitrary",)),
        name="ffn",
    )(base, counts, xs, wgu, wd)


def _final_kernel(rows_ref, gate_ref, h_ref, wsgu_ref, wsd_ref, ln2w_ref, ln2b_ref, o_ref):
    h = h_ref[...]
    hb = h.astype(BF)
    gu = _dot(hb, wsgu_ref[...])
    act = (jax.nn.silu(gu[:, :SHARED_FF]) * gu[:, SHARED_FF:]).astype(BF)
    shared = _dot(act, wsd_ref[...])

    half = D_MODEL // 2
    lo_acc = jnp.zeros((FINAL_TM, half), F32)
    hi_acc = jnp.zeros((FINAL_TM, half), F32)
    gates = gate_ref[...]
    for kk in range(TOP_K):
        words = rows_ref[kk]
        gk = gates[:, kk:kk + 1]
        lo_acc = lo_acc + gk * pltpu.bitcast(words << 16, F32)
        hi_acc = hi_acc + gk * pltpu.bitcast(words & jnp.uint32(0xFFFF0000), F32)
    routed = jnp.concatenate([lo_acc, hi_acc], axis=1)
    o_ref[...] = _layer_norm(ALPHA * h + routed + shared, ln2w_ref[...], ln2b_ref[...])


def _final(rows, gates_t, h, wsgu, wsd, ln2w, ln2b):
    n_tok = h.shape[0]
    n_tiles = n_tok // FINAL_TM
    half = D_MODEL // 2
    row = lambda i: (i, 0)
    consts = (wsgu, wsd, ln2w, ln2b)
    return pl.pallas_call(
        _final_kernel,
        grid=(n_tiles,),
        in_specs=[pl.BlockSpec((TOP_K, FINAL_TM, half), lambda i: (0, i, 0)),
                  pl.BlockSpec((FINAL_TM, TOP_K), row),
                  pl.BlockSpec((FINAL_TM, D_MODEL), row)] + [_const_spec(c.shape) for c in consts],
        out_specs=pl.BlockSpec((FINAL_TM, D_MODEL), row),
        out_shape=jax.ShapeDtypeStruct((n_tok, D_MODEL), F32),
        compiler_params=pltpu.CompilerParams(dimension_semantics=("arbitrary",)),
        name="final",
    )(rows, gates_t, h, *consts)


def _np_consts():
    idx = np.arange(QUAD)
    g256 = (idx[:, None] // HEAD_DIM == idx[None, :] // HEAD_DIM).astype(np.float32)
    t = np.arange(CUM_TILE)
    same = t[:, None] // SCAN_CHUNK == t[None, :] // SCAN_CHUNK
    ltri = (same & (t[None, :] <= t[:, None])).astype(np.float32)
    return jnp.asarray(g256, BF), jnp.asarray(ltri, BF)


def _block_plan(counts, n_assign):
    padded = (counts + FFN_ROWS - 1) // FFN_ROWS * FFN_ROWS
    base = jnp.cumsum(padded) - padded
    n_blocks = (n_assign + N_EXPERTS * (FFN_ROWS - 1) + FFN_ROWS - 1) // FFN_ROWS
    return base.astype(jnp.int32), n_blocks * FFN_ROWS


def _layer(x2, seq, w_in, shift_mu, w0, decay_lora, a0, iclr_lora, gate_lora, k_k, k_a, r_k, gn_w, gn_b,
           lnw, lnb, spatial_w, spatial_b, w_o_rwkv, w_o_gmlp, w_out, ln1_w, ln1_b,
           w_router, router_bias, w_exp_gate_up, w_exp_down, w_sh_gate_up, w_sh_down, ln2_w, ln2_b):
    n_tok = x2.shape[0]
    c_rwkv = C_RKV + C_LORA
    pad = C_LORA_PAD - C_LORA
    win = jnp.concatenate([w_in[:, :c_rwkv], jnp.zeros((D_MODEL, pad), F32), w_in[:, c_rwkv:]], axis=1).astype(BF)
    mu = jnp.pad(shift_mu, (0, pad)).reshape(1, C_RWKV_PAD)
    row1 = lambda t: t.reshape(1, -1)
    w1 = jnp.zeros((LANES, 2 * D_MODEL), F32)
    w1 = w1.at[:DECAY_LORA, :D_MODEL].set(decay_lora).at[DECAY_LORA:, D_MODEL:].set(iclr_lora).astype(BF)
    w2 = jnp.zeros((C_LORA_PAD - LANES, D_MODEL), F32).at[:GATE_LORA].set(gate_lora).astype(BF)
    g256, ltri = _np_consts()
    sb_map = jnp.repeat(spatial_b.T, D_MODEL // GMLP_GROUPS, axis=1)

    outs = _front(x2, win, mu, row1(w0), row1(a0), w1, w2, row1(k_k), row1(k_a), row1(r_k),
                  g256, ltri, row1(lnw), row1(lnb), spatial_w, sb_map, w_o_gmlp.astype(BF), seq)
    quads, wc, (g, bg, ga, zb) = outs[:7], outs[7], outs[8:]
    h1, h1p = _scan(quads, wc, g, bg, ga, zb, x2, row1(gn_w), row1(gn_b), g256,
                    w_o_rwkv.astype(BF), w_out.astype(BF), row1(ln1_w), row1(ln1_b), seq)

    wrt = w_router.T
    wrh = wrt.astype(BF)
    wrl = (wrt - wrh.astype(F32)).astype(BF)
    tt = np.arange(ROUTE_TM)
    ustrict = jnp.asarray((tt[:, None] < tt[None, :]).astype(np.float32), BF)
    top_e, gates, rank, counts = _route(h1, wrh, wrl, router_bias.reshape(N_EXPERTS, 1), ustrict)
    counts = counts[:, 0]
    base, n_rows = _block_plan(counts, n_tok * TOP_K)
    pos = _rows(top_e, rank, base.reshape(N_EXPERTS, 1))
    xs = _sc_scatter_rows(h1p, pos, n_rows)
    rows = _ffn(base, counts, xs, w_exp_gate_up, w_exp_down)
    picked = _sc_gather_rows(rows, pos.reshape(-1)).reshape(TOP_K, n_tok, D_MODEL // 2)
    return _final(picked, gates.T, h1, w_sh_gate_up.astype(BF), w_sh_down.astype(BF),
                  row1(ln2_w), row1(ln2_b))


def kernel(x, w_in, shift_mu, rwkv_w0, rwkv_decay_lora, rwkv_a0, rwkv_iclr_lora, rwkv_gate_lora, rwkv_k_k, rwkv_k_a, rwkv_r_k, rwkv_gn_w, rwkv_gn_b, gmlp_ln_w, gmlp_ln_b, gmlp_spatial_w, gmlp_spatial_b, w_o_rwkv, w_o_gmlp, w_out, ln1_w, ln1_b, w_router, router_bias, w_exp_gate_up, w_exp_down, w_sh_gate_up, w_sh_down, ln2_w, ln2_b):
    bsz, seq, d = x.shape
    h = x.reshape(bsz * seq, d)
    for l in range(DEPTH):
        h = _layer(h, seq, w_in[l], shift_mu[l], rwkv_w0[l], rwkv_decay_lora[l], rwkv_a0[l],
                   rwkv_iclr_lora[l], rwkv_gate_lora[l], rwkv_k_k[l], rwkv_k_a[l], rwkv_r_k[l],
                   rwkv_gn_w[l], rwkv_gn_b[l], gmlp_ln_w[l], gmlp_ln_b[l], gmlp_spatial_w[l],
                   gmlp_spatial_b[l], w_o_rwkv[l], w_o_gmlp[l], w_out[l], ln1_w[l], ln1_b[l],
                   w_router[l], router_bias[l], w_exp_gate_up[l], w_exp_down[l], w_sh_gate_up[l],
                   w_sh_down[l], ln2_w[l], ln2_b[l])
    return h.reshape(bsz, seq, d)
```
